```python
import jax, jax.numpy as jnp
from jax import lax
import numpy as np

D_MODEL = 1024
BATCH = 32
SEQ = 256
DEPTH = 2
DEC_BATCH = 8
DEC_SEQ = 1024
PAST_LEN = 256

F32 = jnp.float32
GRID_W = 64
EPS = 1e-6
N_MOD = 6
SSD_HEADS = 16
SSD_HEAD_DIM = 64
SSD_INNER = SSD_HEADS * SSD_HEAD_DIM
SSD_STATE = 128
SSD_GROUPS = 2
SSD_CONV = 5
SSD_CONV_DIM = SSD_INNER + 2 * SSD_GROUPS * SSD_STATE
SSD_CHUNK = 64
HG_HEADS = 8
HG_KEY = 128
HG_VAL = 128
HG_QK = HG_HEADS * HG_KEY
HG_IV = HG_HEADS * HG_VAL
HG_CHUNK = 32
SSM_IN_DIM = SSD_INNER + SSD_CONV_DIM + 2 * SSD_HEADS + 3 * HG_QK + 2 * HG_IV
SSM_MIX_DIM = SSD_INNER + HG_IV
ATT_HEADS = 16
KV_HEADS = 4
Q_PER_KV = ATT_HEADS // KV_HEADS
HEAD_DIM = 64
ATT_IN_DIM = (ATT_HEADS + 2 * KV_HEADS) * HEAD_DIM
ATT_BLOCK = 128
ROPE_THETA = 10000.0
N_EXPERTS = 32
TOP_K = 4
EXPERT_FF = 1024
SWIGLU_ALPHA = 1.702
SWIGLU_LIMIT = 7.0
N_SSM_LAYERS = (DEPTH + 1) // 2
N_ATTN_LAYERS = DEPTH // 2

kernel_name = 'hybrid_ssd_hgrn2_gqa_moe_diffusion_step'


def rmsnorm(x, w):
    xf = x.astype(F32)
    y = xf * lax.rsqrt(jnp.mean(xf * xf, axis=-1, keepdims=True) + EPS)
    return (y * w.astype(F32)).astype(x.dtype)


def modulate(h, shift, scale):
    return h * (1.0 + scale) + shift


def split_sizes(x, sizes):
    return jnp.split(x, np.cumsum(sizes)[:-1].tolist(), axis=-1)


def flip(a):
    return jnp.flip(a, axis=1)


def centred_depthwise_conv(x, w, b):
    pad = w.shape[0] // 2
    y = lax.conv_general_dilated(x, w[:, None, :].astype(x.dtype), window_strides=(1,),
                                 padding=[(pad, pad)], dimension_numbers=('NWC', 'WIO', 'NWC'),
                                 feature_group_count=x.shape[-1])
    return y + b


def ssd_chunk_scan(x, dt, a, bm, cm, s0):
    b, L, H, P = x.shape
    G, N = bm.shape[-2:]
    E = H // G
    T = SSD_CHUNK
    nc = L // T
    x = x.astype(F32).reshape(b, nc, T, G, E, P)
    dt = dt.astype(F32).reshape(b, nc, T, G, E)
    bm = bm.astype(F32).reshape(b, nc, T, G, N)
    cm = cm.astype(F32).reshape(b, nc, T, G, N)
    cum = jnp.cumsum(dt * a.reshape(G, E), axis=2)
    causal = jnp.tril(jnp.ones((T, T), bool))
    seg = cum[:, :, :, None] - cum[:, :, None, :]
    lmat = jnp.exp(jnp.where(causal[:, :, None, None], seg, -jnp.inf))
    xdt = x * dt[..., None]
    cb = jnp.einsum('bclgn,bcsgn->bclsg', cm, bm)
    y_diag = jnp.einsum('bclsge,bcsgep->bclgep', cb[..., None] * lmat, xdt)
    decay_to_end = jnp.exp(cum[:, :, -1:] - cum)
    chunk_states = jnp.einsum('bcsgn,bcsge,bcsgep->bcgepn', bm, decay_to_end, xdt)
    chunk_decay = jnp.exp(cum[:, :, -1])

    def step(s, inp):
        dec, st = inp
        return dec[..., None, None] * s + st, s

    s_fin, s_prev = lax.scan(step, s0.astype(F32).reshape(b, G, E, P, N),
                             (jnp.moveaxis(chunk_decay, 1, 0), jnp.moveaxis(chunk_states, 1, 0)))
    s_prev = jnp.moveaxis(s_prev, 0, 1)
    y_off = jnp.einsum('bclgn,bcgepn,bclge->bclgep', cm, s_prev, jnp.exp(cum))
    return (y_diag + y_off).reshape(b, L, H, P), s_fin.reshape(b, H, P, N)


def gla_chunk_scan(q, k, v, log_f, s0):
    b, L, H, K = q.shape
    V = v.shape[-1]
    T = HG_CHUNK
    nc = L // T
    q = q.astype(F32).reshape(b, nc, T, H, K)
    k = k.astype(F32).reshape(b, nc, T, H, K)
    v = v.astype(F32).reshape(b, nc, T, H, V)
    cum = jnp.cumsum(log_f.astype(F32).reshape(b, nc, T, H, K), axis=2)
    q_dec = q * jnp.exp(cum)
    k_inv = k * jnp.exp(-cum)
    causal = jnp.tril(jnp.ones((T, T), bool))
    att = jnp.where(causal, jnp.einsum('bclhk,bcshk->bchls', q_dec, k_inv), 0.0)
    o_intra = jnp.einsum('bchls,bcshv->bclhv', att, v)
    chunk_states = jnp.einsum('bcshk,bcshv->bchkv', k * jnp.exp(cum[:, :, -1:] - cum), v)
    chunk_decay = jnp.exp(cum[:, :, -1])

    def step(s, inp):
        dec, st = inp
        return dec[..., None] * s + st, s

    s_fin, s_prev = lax.scan(step, s0.astype(F32),
                             (jnp.moveaxis(chunk_decay, 1, 0), jnp.moveaxis(chunk_states, 1, 0)))
    s_prev = jnp.moveaxis(s_prev, 0, 1)
    o_inter = jnp.einsum('bclhk,bchkv->bclhv', q_dec, s_prev)
    return (o_intra + o_inter).reshape(b, L, H, V), s_fin


def ssm_mixer(h, w_in, conv_w, conv_b, dt_bias, a_log, d_skip, ssd_norm_w, lb, hg_norm_w, w_out, s0_ssd, s0_hg):
    b, L, _ = h.shape
    z, xbc, dt_raw, hq, hf, hi, hg = split_sizes(
        h @ w_in, [SSD_INNER, SSD_CONV_DIM, 2 * SSD_HEADS, HG_QK, 2 * HG_QK, HG_IV, HG_IV])
    xbc = jax.nn.silu(centred_depthwise_conv(xbc, conv_w, conv_b))
    xs, bm, cm = split_sizes(xbc, [SSD_INNER, SSD_GROUPS * SSD_STATE, SSD_GROUPS * SSD_STATE])
    xs = xs.reshape(b, L, SSD_HEADS, SSD_HEAD_DIM)
    bm = bm.reshape(b, L, SSD_GROUPS, SSD_STATE)
    cm = cm.reshape(b, L, SSD_GROUPS, SSD_STATE)
    dt = jax.nn.softplus(dt_raw.astype(F32).reshape(b, L, 2, SSD_HEADS) + dt_bias.astype(F32))
    a = -jnp.exp(a_log.astype(F32))
    y_f, sf_f = ssd_chunk_scan(xs, dt[:, :, 0], a[0], bm, cm, s0_ssd[:, 0])
    y_b, sf_b = ssd_chunk_scan(flip(xs), flip(dt[:, :, 1]), a[1], flip(bm), flip(cm), s0_ssd[:, 1])
    y = y_f + flip(y_b) + xs.astype(F32) * d_skip.astype(F32)[:, None]
    y = rmsnorm(y.reshape(b, L, SSD_INNER) * jax.nn.silu(z.astype(F32)), ssd_norm_w)
    q = jax.nn.silu(hq.astype(F32)).reshape(b, L, HG_HEADS, HG_KEY)
    logit = hf.astype(F32).reshape(b, L, 2, HG_QK)
    lbf = lb.astype(F32)
    f = lbf + (1.0 - lbf) * jax.nn.sigmoid(logit)
    key = ((1.0 - lbf) * jax.nn.sigmoid(-logit)).reshape(b, L, 2, HG_HEADS, HG_KEY)
    log_f = jnp.log(f).reshape(b, L, 2, HG_HEADS, HG_KEY)
    iv = hi.astype(F32).reshape(b, L, HG_HEADS, HG_VAL)
    o_f, st_f = gla_chunk_scan(q, key[:, :, 0], iv, log_f[:, :, 0], s0_hg[:, 0])
    o_b, st_b = gla_chunk_scan(flip(q), flip(key[:, :, 1]), flip(iv), flip(log_f[:, :, 1]), s0_hg[:, 1])
    o = rmsnorm(o_f + flip(o_b), hg_norm_w.reshape(HG_HEADS, HG_VAL)).reshape(b, L, HG_IV)
    o = o * jax.nn.silu(hg.astype(F32))
    out = jnp.concatenate([y, o], axis=-1).astype(h.dtype) @ w_out
    return (out, jnp.stack([sf_f, sf_b], axis=1).astype(h.dtype),
            jnp.stack([st_f, st_b], axis=1).astype(h.dtype))


def axial_rope_tables(n_tok):
    rows = n_tok // GRID_W
    row = jnp.repeat(jnp.arange(rows), GRID_W)
    col = jnp.tile(jnp.arange(GRID_W), rows)
    pos = jnp.stack([row, col], axis=-1).astype(F32)
    half = HEAD_DIM // 2
    inv_freq = ROPE_THETA ** (-jnp.arange(0, half, 2, dtype=F32) / half)
    ang = pos[:, :, None] * inv_freq
    return jnp.cos(ang), jnp.sin(ang)


def apply_axial_rope(x, cos, sin):
    b, L, H, D = x.shape
    xr = x.astype(F32).reshape(b, L, H, 2, D // 2)
    x1, x2 = jnp.split(xr, 2, axis=-1)
    c = cos[None, :, None]
    s = sin[None, :, None]
    out = jnp.concatenate([x1 * c - x2 * s, x2 * c + x1 * s], axis=-1)
    return out.reshape(b, L, H, D).astype(x.dtype)


def attn_project(h, w_in, q_norm_w, k_norm_w):
    b, L, _ = h.shape
    q, k, v = split_sizes(h @ w_in, [ATT_HEADS * HEAD_DIM, KV_HEADS * HEAD_DIM, KV_HEADS * HEAD_DIM])
    q = rmsnorm(q.reshape(b, L, ATT_HEADS, HEAD_DIM), q_norm_w)
    k = rmsnorm(k.reshape(b, L, KV_HEADS, HEAD_DIM), k_norm_w)
    return q, k, v.reshape(b, L, KV_HEADS, HEAD_DIM)


def block_attention(q, k, v):
    b, lq = q.shape[:2]
    nb = lq // ATT_BLOCK
    qb = jnp.moveaxis(q.reshape(b, nb, ATT_BLOCK, KV_HEADS, Q_PER_KV, HEAD_DIM), 1, 0)
    scale = HEAD_DIM ** -0.5

    def one_block(qblk):
        s = jnp.einsum('bqhgd,bkhd->bhgqk', qblk, k).astype(F32) * scale
        p = jax.nn.softmax(s, axis=-1)
        return jnp.einsum('bhgqk,bkhd->bqhgd', p.astype(v.dtype), v)

    o = lax.map(one_block, qb)
    return jnp.moveaxis(o, 0, 1).reshape(b, lq, ATT_HEADS * HEAD_DIM)


def moe_ffn(h, router_w, router_b, w1, b1, w2, b2):
    shp = h.shape
    t = h.reshape(-1, shp[-1])
    logits = (t @ router_w + router_b).astype(F32)
    top_val, top_idx = lax.top_k(logits, TOP_K)
    gates = jax.nn.softmax(top_val, axis=-1)
    combine = jnp.einsum('tk,tke->te', gates,
                         jax.nn.one_hot(top_idx, N_EXPERTS, dtype=F32)).astype(h.dtype)
    out = jnp.zeros_like(t)
    for e in range(N_EXPERTS):
        gu = t @ w1[e] + b1[e]
        gate = jnp.minimum(gu[:, :EXPERT_FF], SWIGLU_LIMIT)
        up = jnp.clip(gu[:, EXPERT_FF:], -SWIGLU_LIMIT, SWIGLU_LIMIT)
        act = (up + 1.0) * gate * jax.nn.sigmoid(SWIGLU_ALPHA * gate)
        out = out + combine[:, e:e + 1] * (act @ w2[e] + b2[e])
    return out.reshape(shp)


def setup_inputs(seed: int = 0) -> dict:
    key = jax.random.key(seed)
    ks = jax.random.split(key, 34)
    nrm = jax.random.normal
    D = D_MODEL
    dt0 = jnp.exp(jax.random.uniform(ks[15], (N_SSM_LAYERS, 2, SSD_HEADS))
                  * (jnp.log(0.1) - jnp.log(0.001)) + jnp.log(0.001))
    return {
        'x_prompt': nrm(ks[0], (BATCH, SEQ, D), F32),
        'x_sample': nrm(ks[1], (DEC_BATCH, DEC_SEQ, D), F32),
        'c': nrm(ks[2], (DEC_BATCH, D), F32),
        'c_ctx': nrm(ks[3], (D,), F32),
        'state_ssd': 0.5 * nrm(ks[4], (DEC_BATCH, N_SSM_LAYERS, 2, SSD_HEADS, SSD_HEAD_DIM, SSD_STATE), F32),
        'state_hgrn': 0.5 * nrm(ks[5], (DEC_BATCH, N_SSM_LAYERS, 2, HG_HEADS, HG_KEY, HG_VAL), F32),
        'cache_k': nrm(ks[6], (DEC_BATCH, N_ATTN_LAYERS, PAST_LEN, KV_HEADS, HEAD_DIM), F32),
        'cache_v': nrm(ks[7], (DEC_BATCH, N_ATTN_LAYERS, PAST_LEN, KV_HEADS, HEAD_DIM), F32),
        'mod_w': nrm(ks[8], (DEPTH, D, N_MOD * D), F32) * D ** -0.5,
        'mod_b': 0.01 * nrm(ks[9], (DEPTH, N_MOD * D), F32),
        'norm1_w': 1.0 + 0.02 * nrm(ks[10], (DEPTH, D), F32),
        'norm2_w': 1.0 + 0.02 * nrm(ks[11], (DEPTH, D), F32),
        'ssm_in_w': nrm(ks[12], (N_SSM_LAYERS, D, SSM_IN_DIM), F32) * D ** -0.5,
        'ssd_conv_w': nrm(ks[13], (N_SSM_LAYERS, SSD_CONV, SSD_CONV_DIM), F32) * SSD_CONV ** -0.5,
        'ssd_conv_b': 0.01 * nrm(ks[14], (N_SSM_LAYERS, SSD_CONV_DIM), F32),
        'ssd_dt_bias': dt0 + jnp.log(-jnp.expm1(-dt0)),
        'ssd_a_log': jnp.log(jax.random.uniform(ks[16], (N_SSM_LAYERS, 2, SSD_HEADS), F32, 1.0, 16.0)),
        'ssd_d': 1.0 + 0.1 * nrm(ks[17], (N_SSM_LAYERS, SSD_HEADS), F32),
        'ssd_norm_w': 1.0 + 0.02 * nrm(ks[18], (N_SSM_LAYERS, SSD_INNER), F32),
        'hg_lb': 0.1 * nrm(ks[19], (DEPTH + 1, 2, HG_QK), F32),
        'hg_norm_w': 1.0 + 0.02 * nrm(ks[20], (N_SSM_LAYERS, HG_IV), F32),
        'ssm_out_w': nrm(ks[21], (N_SSM_LAYERS, SSM_MIX_DIM, D), F32) * SSM_MIX_DIM ** -0.5,
        'attn_in_w': nrm(ks[22], (N_ATTN_LAYERS, D, ATT_IN_DIM), F32) * D ** -0.5,
        'q_norm_w': 1.0 + 0.02 * nrm(ks[23], (N_ATTN_LAYERS, HEAD_DIM), F32),
        'k_norm_w': 1.0 + 0.02 * nrm(ks[24], (N_ATTN_LAYERS, HEAD_DIM), F32),
        'attn_out_w': nrm(ks[25], (N_ATTN_LAYERS, ATT_HEADS * HEAD_DIM, D), F32) * (ATT_HEADS * HEAD_DIM) ** -0.5,
        'router_w': nrm(ks[26], (DEPTH, D, N_EXPERTS), F32) * D ** -0.5,
        'router_b': 0.01 * nrm(ks[27], (DEPTH, N_EXPERTS), F32),
        'exp_w1': nrm(ks[28], (DEPTH, N_EXPERTS, D, 2 * EXPERT_FF), F32) * D ** -0.5,
        'exp_b1': 0.01 * nrm(ks[29], (DEPTH, N_EXPERTS, 2 * EXPERT_FF), F32),
        'exp_w2': nrm(ks[30], (DEPTH, N_EXPERTS, EXPERT_FF, D), F32) * EXPERT_FF ** -0.5,
        'exp_b2': 0.01 * nrm(ks[31], (DEPTH, N_EXPERTS, D), F32),
        'final_norm_w': 1.0 + 0.02 * nrm(ks[32], (D,), F32),
    }


def reference(x_prompt, x_sample, c, c_ctx, state_ssd, state_hgrn, cache_k, cache_v,
              mod_w, mod_b, norm1_w, norm2_w, ssm_in_w, ssd_conv_w, ssd_conv_b, ssd_dt_bias,
              ssd_a_log, ssd_d, ssd_norm_w, hg_lb, hg_norm_w, ssm_out_w, attn_in_w, q_norm_w,
              k_norm_w, attn_out_w, router_w, router_b, exp_w1, exp_b1, exp_w2, exp_b2, final_norm_w):
    bp = x_prompt.shape[0]
    n_lat = x_sample.shape[1]
    cos, sin = axial_rope_tables(n_lat)
    lb_all = jnp.cumsum(jax.nn.softmax(hg_lb.astype(F32), axis=0), axis=0)
    zero_ssd = jnp.zeros((bp, 2, SSD_HEADS, SSD_HEAD_DIM, SSD_STATE), F32)
    zero_hg = jnp.zeros((bp, 2, HG_HEADS, HG_KEY, HG_VAL), F32)
    xp, xs = x_prompt, x_sample
    new_ssd, new_hg, new_k, new_v = [], [], [], []
    for l in range(DEPTH):
        mod_p = jnp.split(jax.nn.silu(c_ctx) @ mod_w[l] + mod_b[l], N_MOD, axis=-1)
        mod_s = jnp.split((jax.nn.silu(c) @ mod_w[l] + mod_b[l])[:, None, :], N_MOD, axis=-1)
        hp = modulate(rmsnorm(xp, norm1_w[l]), mod_p[0], mod_p[1])
        hs = modulate(rmsnorm(xs, norm1_w[l]), mod_s[0], mod_s[1])
        i = l // 2
        if l % 2 == 0:
            args = (ssm_in_w[i], ssd_conv_w[i], ssd_conv_b[i], ssd_dt_bias[i], ssd_a_log[i], ssd_d[i],
                    ssd_norm_w[i], lb_all[l], hg_norm_w[i], ssm_out_w[i])
            mix_p, st_ssd, st_hg = ssm_mixer(hp, *args, zero_ssd, zero_hg)
            mix_s, _, _ = ssm_mixer(hs, *args, state_ssd[:, i], state_hgrn[:, i])
            new_ssd.append(st_ssd)
            new_hg.append(st_hg)
        else:
            qp, kp, vp = attn_project(hp, attn_in_w[i], q_norm_w[i], k_norm_w[i])
            mix_p = block_attention(qp, kp, vp) @ attn_out_w[i]
            qs, ksl, vsl = attn_project(hs, attn_in_w[i], q_norm_w[i], k_norm_w[i])
            qs = apply_axial_rope(qs, cos, sin)
            ksl = apply_axial_rope(ksl, cos, sin)
            k_all = jnp.concatenate([cache_k[:, i].astype(ksl.dtype), ksl], axis=1)
            v_all = jnp.concatenate([cache_v[:, i].astype(vsl.dtype), vsl], axis=1)
            mix_s = block_attention(qs, k_all, v_all) @ attn_out_w[i]
            new_k.append(kp)
            new_v.append(vp)
        xp = xp + mod_p[2] * mix_p
        xs = xs + mod_s[2] * mix_s
        moe_args = (router_w[l], router_b[l], exp_w1[l], exp_b1[l], exp_w2[l], exp_b2[l])
        hp = modulate(rmsnorm(xp, norm2_w[l]), mod_p[3], mod_p[4])
        hs = modulate(rmsnorm(xs, norm2_w[l]), mod_s[3], mod_s[4])
        xp = xp + mod_p[5] * moe_ffn(hp, *moe_args)
        xs = xs + mod_s[5] * moe_ffn(hs, *moe_args)
    y_prompt = rmsnorm(xp, final_norm_w)
    y_sample = rmsnorm(xs, final_norm_w)
    new_state_ssd = jnp.stack(new_ssd, axis=1)
    new_state_hgrn = jnp.stack(new_hg, axis=1)
    new_cache_k = jnp.stack(new_k, axis=1)
    new_cache_v = jnp.stack(new_v, axis=1)
    return (y_prompt, y_sample, new_state_ssd, new_state_hgrn, new_cache_k, new_cache_v)
```

```python
import functools

import jax
import jax.numpy as jnp
import numpy as np
from jax import lax
from jax.experimental import pallas as pl
from jax.experimental.pallas import tpu as pltpu

F32 = jnp.float32
BF16 = jnp.bfloat16
I32 = jnp.int32

D = 1024
BATCH, SEQ = 32, 256
DEC_BATCH, DEC_SEQ = 8, 1024
PAST = 256
NP_TOK = BATCH * SEQ
NS_TOK = DEC_BATCH * DEC_SEQ
T = NP_TOK + NS_TOK
DEPTH = 2
N_MOD = 6
EPS = 1e-6
GRID_W = 64
ROPE_THETA = 10000.0

SSD_HEADS, SSD_P, SSD_N, SSD_G = 16, 64, 128, 2
SSD_INNER = SSD_HEADS * SSD_P
SSD_CONV = 5
HPG = SSD_HEADS // SSD_G
GW = HPG * SSD_P
HG_HEADS, HG_K, HG_V = 8, 128, 128
HG_QK = HG_HEADS * HG_K
HG_IV = HG_HEADS * HG_V
ATT_HEADS, KV_HEADS, HEAD_DIM = 16, 4, 64
Q_PER_KV = ATT_HEADS // KV_HEADS
N_EXP, TOP_K, FF = 32, 4, 1024
SWIGLU_ALPHA, SWIGLU_LIMIT = 1.702, 7.0

LANES = 128
SUBLANES = 8
VMEM_LIMIT = 56 * 1024 * 1024

COL_Z, COL_HQ, COL_HI, COL_HG, COL_HF, COL_XBC = 0, 1024, 2048, 3072, 4096, 6144
PROJ_W = 7680

SSD_TC = 256
GLA_TC = 64
TM_E = 256
N_SLOTS = T * TOP_K
P_ROWS = N_SLOTS + N_EXP * TM_E
N_ETILES = P_ROWS // TM_E
TM_R = 256


def _cparams(sem):
    return pltpu.CompilerParams(dimension_semantics=sem, vmem_limit_bytes=VMEM_LIMIT)


def _silu(x):
    return x * jax.nn.sigmoid(x)


def _bdot(a, b):
    return jnp.dot(a.astype(BF16), b.astype(BF16), preferred_element_type=F32)


def _bdot_nt(a, b):
    return lax.dot_general(a.astype(BF16), b.astype(BF16), (((1,), (1,)), ((), ())),
                           preferred_element_type=F32)


def _xdot(a, b):
    return jnp.dot(a, b, preferred_element_type=F32, precision=lax.Precision.HIGHEST)


def _mod_row(i, tm):
    start = i * tm
    return jnp.where(start < NP_TOK, 0, 1 + (start - NP_TOK) // DEC_SEQ)


def _mod_vec(mod_ref, r, k):
    return mod_ref[pl.ds(r, 1), k * D:(k + 1) * D]


def _rms(x, w):
    return x * lax.rsqrt(jnp.mean(x * x, axis=-1, keepdims=True) + EPS) * w


def _mod_kernel(c_ref, w_ref, b_ref, o_ref):
    o_ref[0] = _bdot(_silu(c_ref[...]), w_ref[0]) + b_ref[0]


def _mod_table(cvec, mod_w, mod_b):
    tn = 1536
    return pl.pallas_call(
        _mod_kernel,
        grid=(DEPTH, N_MOD * D // tn),
        in_specs=[pl.BlockSpec((16, D), lambda l, n: (0, 0)),
                  pl.BlockSpec((1, D, tn), lambda l, n: (l, 0, n)),
                  pl.BlockSpec((1, 1, tn), lambda l, n: (l, 0, n))],
        out_specs=pl.BlockSpec((1, 16, tn), lambda l, n: (l, 0, n)),
        out_shape=jax.ShapeDtypeStruct((DEPTH, 16, N_MOD * D), F32),
        compiler_params=_cparams(("arbitrary", "arbitrary")),
        name="mod_table",
    )(cvec, mod_w, mod_b.reshape(DEPTH, 1, N_MOD * D))


def _inproj_kernel(xp_ref, xs_ref, mod_ref, nw_ref, w_ref, wx_ref, o_ref, ox_ref, h_ref, *, tm):
    i = pl.program_id(0)

    @pl.when(pl.program_id(1) == 0)
    def _():
        x = jnp.where(i * tm < NP_TOK, xp_ref[...], xs_ref[...])
        r = _mod_row(i, tm)
        h = _rms(x, nw_ref[...]) * (1.0 + _mod_vec(mod_ref, r, 1)) + _mod_vec(mod_ref, r, 0)
        hb = h.astype(BF16)
        h_ref[...] = hb
        ox_ref[...] = jnp.dot(hb, wx_ref[...], preferred_element_type=F32)

    o_ref[...] = jnp.dot(h_ref[...], w_ref[...], preferred_element_type=F32)


def _two_src_specs(tm):
    npt = NP_TOK // tm
    return [pl.BlockSpec((tm, D), lambda i, *_: (jnp.minimum(i, npt - 1), 0)),
            pl.BlockSpec((tm, D), lambda i, *_: (jnp.maximum(i - npt, 0), 0))]


def _inproj(xp, xs, mod_l, nw, w, wx, *, tm, tn):
    n_out, nx = w.shape[1], wx.shape[1]
    return pl.pallas_call(
        functools.partial(_inproj_kernel, tm=tm),
        grid=(T // tm, n_out // tn),
        in_specs=_two_src_specs(tm) + [
            pl.BlockSpec((16, N_MOD * D), lambda i, n: (0, 0)),
            pl.BlockSpec((1, D), lambda i, n: (0, 0)),
            pl.BlockSpec((D, tn), lambda i, n: (0, n)),
            pl.BlockSpec((D, nx), lambda i, n: (0, 0))],
        out_specs=[pl.BlockSpec((tm, tn), lambda i, n: (i, n)),
                   pl.BlockSpec((tm, nx), lambda i, n: (i, 0))],
        out_shape=[jax.ShapeDtypeStruct((T, n_out), F32), jax.ShapeDtypeStruct((T, nx), F32)],
        scratch_shapes=[pltpu.VMEM((tm, D), BF16)],
        compiler_params=_cparams(("arbitrary", "arbitrary")),
        name="inproj",
    )(xp, xs, mod_l, nw, w, wx)


def _tri(n, upper):
    r = lax.broadcasted_iota(I32, (n, n), 0)
    c = lax.broadcasted_iota(I32, (n, n), 1)
    return (c >= r) if upper else (c <= r)


def _conv_silu(x, w, b, L):
    row = lax.broadcasted_iota(I32, (L, 1), 0)
    acc = x * w[2:3] + b
    for k in (0, 1, 3, 4):
        off = k - 2
        shifted = pltpu.roll(x, (-off) % L, axis=0)
        ok = (row + off >= 0) & (row + off < L)
        acc = acc + jnp.where(ok, shifted, 0.0) * w[k:k + 1]
    return _silu(acc)


def _split_hi_lo(x):
    hi = x.astype(BF16)
    return hi, (x - hi.astype(F32)).astype(BF16)


def _ssd_kernel(*refs, L, has_init, want_state):
    (x_ref, b_ref, c_ref, dt_ref, cwx_ref, cwb_ref, cwc_ref, cbx_ref, cbb_ref, cbc_ref,
     dtb_ref, alog_ref, dcol_ref, rsel_ref) = refs[:14]
    pos = 14
    s0_ref = None
    if has_init:
        s0_ref = refs[pos]
        pos += 1
    y_ref = refs[pos]
    pos += 1
    st_ref = None
    if want_state:
        st_ref = refs[pos]
        pos += 1
    xs_s, bs_s, cs_s, y_s = refs[pos:pos + 4]

    tc = SSD_TC
    nc = L // tc
    xs_s[...] = _conv_silu(x_ref[...], cwx_ref[...], cbx_ref[...], L)
    bs_s[...] = _conv_silu(b_ref[...], cwb_ref[...], cbb_ref[...], L)
    cs_s[...] = _conv_silu(c_ref[...], cwc_ref[...], cbc_ref[...], L)
    y_s[...] = xs_s[...] * dcol_ref[...]

    z = dt_ref[...] + dtb_ref[...]
    dt_all = jnp.maximum(z, 0.0) + jnp.log(1.0 + jnp.exp(-jnp.abs(z)))
    a_row = -jnp.exp(alog_ref[...])

    for d in (0, 1):
        upper = d == 1
        tri = _tri(tc, upper)
        tri_f = tri.astype(F32)
        rsel = rsel_ref[d]
        st = s0_ref[d].T if has_init else None
        order = range(nc) if d == 0 else range(nc - 1, -1, -1)
        for c in order:
            sl = slice(c * tc, (c + 1) * tc)
            x = xs_s[sl]
            bm = bs_s[sl]
            cm = cs_s[sl]
            dt = dt_all[sl]
            cum = _xdot(tri_f, dt * a_row)
            cum_t = cum.T
            end = cum[0:1] if upper else cum[tc - 1:tc]
            stack = jnp.concatenate(
                [dt, jnp.exp(cum), jnp.exp(end - cum), jnp.broadcast_to(jnp.exp(end), (SUBLANES, LANES))],
                axis=0)
            hi, lo = _split_hi_lo(stack)
            ex = (jnp.dot(hi, rsel, preferred_element_type=F32)
                  + jnp.dot(lo, rsel, preferred_element_type=F32))
            dt_e, cum_e, dec_e = ex[0:tc], ex[tc:2 * tc], ex[2 * tc:3 * tc]
            end_e = ex[3 * tc:3 * tc + 1]
            xdt = x * dt_e
            cb = _bdot_nt(cm, bm)
            ys = []
            for h in range(HPG):
                j = d * HPG + h
                seg = cum[:, j:j + 1] - cum_t[j:j + 1, :]
                lm = jnp.exp(jnp.where(tri, seg, -jnp.inf))
                ys.append(_bdot(cb * lm, xdt[:, h * SSD_P:(h + 1) * SSD_P]))
            y = jnp.concatenate(ys, axis=1)
            if st is not None:
                y = y + _bdot(cm, st) * cum_e
            y_s[sl] = y_s[sl] + y
            sc = _bdot(bm.T, xdt * dec_e)
            st = sc if st is None else st * end_e + sc
        if want_state:
            st_ref[d] = st.T

    y_ref[...] = y_s[...]


def _ssd_rsel():
    r = np.zeros((2, LANES, GW), np.float32)
    for d in range(2):
        for h in range(HPG):
            r[d, d * HPG + h, h * SSD_P:(h + 1) * SSD_P] = 1.0
    return jnp.asarray(r, BF16)


def _ssd_call(proj, dtp, prm, *, L, n_seq, row0, s0, y_prev, want_state):
    rb = row0 // L
    has_init = s0 is not None
    xbc = COL_XBC
    in_specs = [
        pl.BlockSpec((L, GW), lambda b, g: (rb + b, xbc // GW + g)),
        pl.BlockSpec((L, SSD_N), lambda b, g: (rb + b, (xbc + SSD_INNER) // SSD_N + g)),
        pl.BlockSpec((L, SSD_N), lambda b, g: (rb + b, (xbc + SSD_INNER + SSD_G * SSD_N) // SSD_N + g)),
        pl.BlockSpec((L, LANES), lambda b, g: (rb + b, g)),
        pl.BlockSpec((SSD_CONV, GW), lambda b, g: (0, g)),
        pl.BlockSpec((SSD_CONV, SSD_N), lambda b, g: (0, g)),
        pl.BlockSpec((SSD_CONV, SSD_N), lambda b, g: (0, g)),
        pl.BlockSpec((1, GW), lambda b, g: (0, g)),
        pl.BlockSpec((1, SSD_N), lambda b, g: (0, g)),
        pl.BlockSpec((1, SSD_N), lambda b, g: (0, g)),
        pl.BlockSpec((1, LANES), lambda b, g: (0, g)),
        pl.BlockSpec((1, LANES), lambda b, g: (0, g)),
        pl.BlockSpec((1, GW), lambda b, g: (0, g)),
        pl.BlockSpec((2, LANES, GW), lambda b, g: (0, 0, 0)),
    ]
    args = [proj, proj, proj, dtp, prm["cwx"], prm["cwb"], prm["cwc"], prm["cbx"], prm["cbb"], prm["cbc"],
            prm["dtb"], prm["alog"], prm["dcol"], prm["rsel"]]
    if has_init:
        in_specs.append(pl.BlockSpec((None, 2, None, GW, SSD_N), lambda b, g: (b, 0, g, 0, 0)))
        args.append(s0)
    out_specs = [pl.BlockSpec((L, GW), lambda b, g: (rb + b, g))]
    out_shape = [jax.ShapeDtypeStruct((T, SSD_INNER), F32)]
    aliases = {}
    if y_prev is not None:
        in_specs.append(pl.BlockSpec(memory_space=pl.ANY))
        args.append(y_prev)
        aliases = {len(args) - 1: 0}
    if want_state:
        out_specs.append(pl.BlockSpec((None, 2, None, GW, SSD_N), lambda b, g: (b, 0, g, 0, 0)))
        out_shape.append(jax.ShapeDtypeStruct((n_seq, 2, SSD_G, GW, SSD_N), F32))

    def body(*refs):
        if y_prev is not None:
            k = len(args) - 1
            refs = refs[:k] + refs[k + 1:]
        _ssd_kernel(*refs, L=L, has_init=has_init, want_state=want_state)

    return pl.pallas_call(
        body,
        grid=(n_seq, SSD_G),
        in_specs=in_specs, out_specs=out_specs, out_shape=out_shape,
        scratch_shapes=[pltpu.VMEM((L, GW), F32), pltpu.VMEM((L, SSD_N), F32),
                        pltpu.VMEM((L, SSD_N), F32), pltpu.VMEM((L, GW), F32)],
        input_output_aliases=aliases,
        compiler_params=_cparams(("arbitrary", "arbitrary")),
        name=f"ssd_L{L}",
    )(*args)


def _gla_kernel(*refs, L, layer, has_init, want_state):
    q_ref, ff_ref, fb_ref, v_ref, g_ref, lb_ref, nw_ref = refs[:7]
    pos = 7
    s0_ref = None
    if has_init:
        s0_ref = refs[pos]
        pos += 1
    o_ref = refs[pos]
    pos += 1
    st_ref = None
    if want_state:
        st_ref = refs[pos]
        pos += 1
    qs, ks, ls, os_ = refs[pos:pos + 4]

    tc = GLA_TC
    nc = L // tc
    lbr = lb_ref[...]
    e = jnp.exp(lbr - jnp.max(lbr, axis=0, keepdims=True))
    sm = e / jnp.sum(e, axis=0, keepdims=True)
    lb = sm[0]
    for j in range(1, layer + 1):
        lb = lb + sm[j]

    qs[...] = _silu(q_ref[...])
    for d, fr in ((0, ff_ref), (1, fb_ref)):
        logit = fr[...]
        lbd = lb[d:d + 1]
        f = lbd + (1.0 - lbd) * jax.nn.sigmoid(logit)
        ks[d] = (1.0 - lbd) * jax.nn.sigmoid(-logit)
        ls[d] = jnp.log(f)
    os_[...] = jnp.zeros_like(os_)

    for d in (0, 1):
        upper = d == 1
        tri = _tri(tc, upper)
        tri_f = tri.astype(F32)
        mid = tc // 2 if upper else tc // 2 - 1
        endr = 0 if upper else tc - 1

        def chunk(ci, st, d=d, upper=upper, tri=tri, tri_f=tri_f, mid=mid, endr=endr):
            c = (nc - 1 - ci) if upper else ci
            start = pl.multiple_of(c * tc, tc)
            q = qs[pl.ds(start, tc), :]
            k = ks[d, pl.ds(start, tc), :]
            v = v_ref[pl.ds(start, tc), :]
            cum = _xdot(tri_f, ls[d, pl.ds(start, tc), :])
            ref_row = cum[mid:mid + 1]
            end = cum[endr:endr + 1]
            att = _bdot_nt(q * jnp.exp(cum - ref_row), k * jnp.exp(ref_row - cum))
            att = jnp.where(tri, att, 0.0)
            o = _bdot(att, v) + _bdot_nt(q * jnp.exp(cum), st)
            os_[pl.ds(start, tc), :] = os_[pl.ds(start, tc), :] + o
            sc = _bdot(v.T, k * jnp.exp(end - cum))
            return st * jnp.exp(end) + sc

        st0 = s0_ref[d].T if has_init else jnp.zeros((HG_V, HG_K), F32)
        st = lax.fori_loop(0, nc, chunk, st0)
        if want_state:
            st_ref[d] = st.T

    o = os_[...]
    o_ref[...] = _rms(o, nw_ref[...]) * _silu(g_ref[...])


def _gla_call(proj, hg_lb, hg_nw, *, layer, L, n_seq, row0, s0, o_prev, want_state):
    rb = row0 // L
    has_init = s0 is not None

    def col(c0):
        return lambda b, h: (rb + b, c0 // HG_K + h)

    in_specs = [
        pl.BlockSpec((L, HG_K), col(COL_HQ)),
        pl.BlockSpec((L, HG_K), col(COL_HF)),
        pl.BlockSpec((L, HG_K), col(COL_HF + HG_QK)),
        pl.BlockSpec((L, HG_V), col(COL_HI)),
        pl.BlockSpec((L, HG_V), col(COL_HG)),
        pl.BlockSpec((DEPTH + 1, 2, HG_K), lambda b, h: (0, 0, h)),
        pl.BlockSpec((1, HG_V), lambda b, h: (0, h)),
    ]
    args = [proj, proj, proj, proj, proj, hg_lb, hg_nw]
    if has_init:
        in_specs.append(pl.BlockSpec((None, 2, None, HG_K, HG_V), lambda b, h: (b, 0, h, 0, 0)))
        args.append(s0)
    out_specs = [pl.BlockSpec((L, HG_V), lambda b, h: (rb + b, h))]
    out_shape = [jax.ShapeDtypeStruct((T, HG_IV), F32)]
    aliases = {}
    if o_prev is not None:
        in_specs.append(pl.BlockSpec(memory_space=pl.ANY))
        args.append(o_prev)
        aliases = {len(args) - 1: 0}
    if want_state:
        out_specs.append(pl.BlockSpec((None, 2, None, HG_K, HG_V), lambda b, h: (b, 0, h, 0, 0)))
        out_shape.append(jax.ShapeDtypeStruct((n_seq, 2, HG_HEADS, HG_K, HG_V), F32))

    def body(*refs):
        if o_prev is not None:
            k = len(args) - 1
            refs = refs[:k] + refs[k + 1:]
        _gla_kernel(*refs, L=L, layer=layer, has_init=has_init, want_state=want_state)

    return pl.pallas_call(
        body,
        grid=(n_seq, HG_HEADS),
        in_specs=in_specs, out_specs=out_specs, out_shape=out_shape,
        scratch_shapes=[pltpu.VMEM((L, HG_K), F32), pltpu.VMEM((2, L, HG_K), F32),
                        pltpu.VMEM((2, L, HG_K), F32), pltpu.VMEM((L, HG_V), F32)],
        input_output_aliases=aliases,
        compiler_params=_cparams(("arbitrary", "arbitrary")),
        name=f"gla_L{L}",
    )(*args)


def _ssm_out_kernel(xp_ref, xs_ref, y_ref, z_ref, o_ref, mod_ref, nw_ref, w_ref, out_ref, *, tm):
    i = pl.program_id(0)
    x = jnp.where(i * tm < NP_TOK, xp_ref[...], xs_ref[...])
    g = y_ref[...] * _silu(z_ref[...])
    mix = _bdot(_rms(g, nw_ref[...]), w_ref[0:SSD_INNER, :]) + _bdot(o_ref[...], w_ref[SSD_INNER:, :])
    out_ref[...] = x + _mod_vec(mod_ref, _mod_row(i, tm), 2) * mix


def _ssm_out(xp, xs, y, proj, o, mod_l, nw, w, *, tm):
    return pl.pallas_call(
        functools.partial(_ssm_out_kernel, tm=tm),
        grid=(T // tm,),
        in_specs=_two_src_specs(tm) + [
            pl.BlockSpec((tm, SSD_INNER), lambda i: (i, 0)),
            pl.BlockSpec((tm, SSD_INNER), lambda i: (i, COL_Z // SSD_INNER)),
            pl.BlockSpec((tm, HG_IV), lambda i: (i, 0)),
            pl.BlockSpec((16, N_MOD * D), lambda i: (0, 0)),
            pl.BlockSpec((1, SSD_INNER), lambda i: (0, 0)),
            pl.BlockSpec((SSD_INNER + HG_IV, D), lambda i: (0, 0))],
        out_specs=pl.BlockSpec((tm, D), lambda i: (i, 0)),
        out_shape=jax.ShapeDtypeStruct((T, D), F32),
        compiler_params=_cparams(("arbitrary",)),
        name="ssm_out",
    )(xp, xs, y, proj, o, mod_l, nw, w)


def _attn_out_kernel(x_ref, a_ref, mod_ref, w_ref, out_ref, *, tm):
    i = pl.program_id(0)
    mix = jnp.dot(a_ref[...], w_ref[...], preferred_element_type=F32)
    out_ref[...] = x_ref[...] + _mod_vec(mod_ref, _mod_row(i, tm), 2) * mix


def _attn_out(x, a, mod_l, w, *, tm):
    return pl.pallas_call(
        functools.partial(_attn_out_kernel, tm=tm),
        grid=(T // tm,),
        in_specs=[pl.BlockSpec((tm, D), lambda i: (i, 0)),
                  pl.BlockSpec((tm, D), lambda i: (i, 0)),
                  pl.BlockSpec((16, N_MOD * D), lambda i: (0, 0)),
                  pl.BlockSpec((D, D), lambda i: (0, 0))],
        out_specs=pl.BlockSpec((tm, D), lambda i: (i, 0)),
        out_shape=jax.ShapeDtypeStruct((T, D), F32),
        compiler_params=_cparams(("arbitrary",)),
        name="attn_out",
    )(x, a, mod_l, w)


def _moe_hidden(x, mod_ref, nw_ref, i):
    r = _mod_row(i, TM_R)
    return _rms(x, nw_ref[...]) * (1.0 + _mod_vec(mod_ref, r, 4)) + _mod_vec(mod_ref, r, 3)


def _route_kernel(x_ref, mod_ref, nw_ref, rw_ref, rb_ref, meta_ref, cnt_ref, carry_ref):
    i = pl.program_id(0)

    @pl.when(i == 0)
    def _():
        carry_ref[...] = jnp.zeros_like(carry_ref)

    h = _moe_hidden(x_ref[...], mod_ref, nw_ref, i)
    logits = _bdot(h, rw_ref[...]) + rb_ref[...]
    lane = lax.broadcasted_iota(I32, (TM_R, LANES), 1)
    lane_f = lane.astype(F32)
    vals, hots, idxs = [], [], []
    for _ in range(TOP_K):
        m = jnp.max(logits, axis=-1, keepdims=True)
        idx = jnp.min(jnp.where(logits == m, lane_f, float(LANES)), axis=-1, keepdims=True)
        hot = lane_f == idx
        vals.append(m)
        idxs.append(idx)
        hots.append(hot)
        logits = jnp.where(hot, -jnp.inf, logits)
    es = [jnp.exp(v - vals[0]) for v in vals]
    den = es[0] + es[1] + es[2] + es[3]

    r_i = lax.broadcasted_iota(I32, (TM_R, TM_R), 0)
    c_i = lax.broadcasted_iota(I32, (TM_R, TM_R), 1)
    below = (c_i < r_i).astype(BF16)
    base = carry_ref[...]
    meta = jnp.zeros((TM_R, LANES), F32)
    for k in range(TOP_K):
        hot_f = hots[k].astype(F32)
        before = jnp.dot(below, hots[k].astype(BF16), preferred_element_type=F32)
        rank = jnp.sum(hot_f * (base + before), axis=-1, keepdims=True)
        base = base + jnp.sum(hot_f, axis=0, keepdims=True)
        meta = jnp.where(lane == k, idxs[k], meta)
        meta = jnp.where(lane == TOP_K + k, es[k] / den, meta)
        meta = jnp.where(lane == 2 * TOP_K + k, rank, meta)
    carry_ref[...] = base
    meta_ref[...] = meta
    cnt_ref[...] = base


def _route(x, mod_l, nw, rw, rb):
    return pl.pallas_call(
        _route_kernel,
        grid=(T // TM_R,),
        in_specs=[pl.BlockSpec((TM_R, D), lambda i: (i, 0)),
                  pl.BlockSpec((16, N_MOD * D), lambda i: (0, 0)),
                  pl.BlockSpec((1, D), lambda i: (0, 0)),
                  pl.BlockSpec((D, LANES), lambda i: (0, 0)),
                  pl.BlockSpec((1, LANES), lambda i: (0, 0))],
        out_specs=[pl.BlockSpec((TM_R, LANES), lambda i: (i, 0)),
                   pl.BlockSpec((1, LANES), lambda i: (0, 0))],
        out_shape=[jax.ShapeDtypeStruct((T, LANES), F32), jax.ShapeDtypeStruct((1, LANES), F32)],
        scratch_shapes=[pltpu.VMEM((1, LANES), F32)],
        compiler_params=_cparams(("arbitrary",)),
        name="moe_route",
    )(x, mod_l, nw, rw, rb)


def _row_copy(src, dst, sem):
    return pltpu.make_async_copy(src, dst, sem)


def _dispatch_kernel(dest_ref, x_ref, mod_ref, nw_ref, xs_hbm, h_s, sem):
    i = pl.program_id(0)
    h = _moe_hidden(x_ref[...], mod_ref, nw_ref, i)
    for c in range(D // LANES):
        h_s[:, c, :] = h[:, c * LANES:(c + 1) * LANES]

    def issue(s, carry):
        _row_copy(h_s.at[pl.ds(s // TOP_K, 1)], xs_hbm.at[pl.ds(dest_ref[0, s], 1)], sem).start()
        return carry

    lax.fori_loop(0, TM_R * TOP_K, issue, 0)

    def drain(s, carry):
        _row_copy(h_s.at[pl.ds(0, 1)], xs_hbm.at[pl.ds(0, 1)], sem).wait()
        return carry

    lax.fori_loop(0, TM_R * TOP_K, drain, 0)


def _dispatch(dest, x, mod_l, nw):
    return pl.pallas_call(
        _dispatch_kernel,
        grid=(T // TM_R,),
        in_specs=[pl.BlockSpec((None, 1, TM_R * TOP_K), lambda i: (i, 0, 0), memory_space=pltpu.SMEM),
                  pl.BlockSpec((TM_R, D), lambda i: (i, 0)),
                  pl.BlockSpec((16, N_MOD * D), lambda i: (0, 0)),
                  pl.BlockSpec((1, D), lambda i: (0, 0))],
        out_specs=pl.BlockSpec(memory_space=pl.ANY),
        out_shape=jax.ShapeDtypeStruct((P_ROWS, D // LANES, LANES), F32),
        scratch_shapes=[pltpu.VMEM((TM_R, D // LANES, LANES), F32), pltpu.SemaphoreType.DMA(())],
        compiler_params=_cparams(("arbitrary",)),
        name="moe_dispatch",
    )(dest, x, mod_l, nw)


def _expert_kernel(te_ref, tr_ref, nv_ref, x_ref, w1_ref, b1_ref, w2_ref, b2_ref, y_ref, w1_s, w2_s):
    j = pl.program_id(0)
    valid = j < nv_ref[0]
    fresh = jnp.logical_or(j == 0, te_ref[j] != te_ref[jnp.maximum(j - 1, 0)])

    @pl.when(jnp.logical_and(valid, fresh))
    def _():
        w1_s[...] = w1_ref[0].astype(BF16)
        w2_s[...] = w2_ref[0].astype(BF16)

    @pl.when(valid)
    def _():
        rows = lax.broadcasted_iota(I32, (TM_E, 1), 0)
        live = rows < tr_ref[j]
        x = jnp.concatenate([x_ref[:, c, :] for c in range(D // LANES)], axis=1)
        x = jnp.where(live, x, 0.0).astype(BF16)
        gu = jnp.dot(x, w1_s[...], preferred_element_type=F32) + b1_ref[0]
        gate = jnp.minimum(gu[:, :FF], SWIGLU_LIMIT)
        up = jnp.clip(gu[:, FF:], -SWIGLU_LIMIT, SWIGLU_LIMIT)
        act = (up + 1.0) * gate * jax.nn.sigmoid(SWIGLU_ALPHA * gate)
        y = jnp.dot(act.astype(BF16), w2_s[...], preferred_element_type=F32) + b2_ref[0]
        for c in range(D // LANES):
            y_ref[:, c, :] = y[:, c * LANES:(c + 1) * LANES]


def _experts(tile_exp, tile_rows, n_valid, xs, w1, b1, w2, b2):
    def xmap(j, te, tr, nv):
        return (jnp.minimum(j, nv[0] - 1), 0, 0)

    def wmap(j, te, tr, nv):
        return (te[j], 0, 0)

    return pl.pallas_call(
        _expert_kernel,
        grid_spec=pltpu.PrefetchScalarGridSpec(
            num_scalar_prefetch=3,
            grid=(N_ETILES,),
            in_specs=[pl.BlockSpec((TM_E, D // LANES, LANES), xmap),
                      pl.BlockSpec((1, D, 2 * FF), wmap),
                      pl.BlockSpec((1, 1, 2 * FF), wmap),
                      pl.BlockSpec((1, FF, D), wmap),
                      pl.BlockSpec((1, 1, D), wmap)],
            out_specs=pl.BlockSpec((TM_E, D // LANES, LANES), xmap),
            scratch_shapes=[pltpu.VMEM((D, 2 * FF), BF16), pltpu.VMEM((FF, D), BF16)]),
        out_shape=jax.ShapeDtypeStruct((P_ROWS, D // LANES, LANES), F32),
        compiler_params=_cparams(("arbitrary",)),
        name="moe_experts",
    )(tile_exp, tile_rows, n_valid, xs, w1, b1.reshape(N_EXP, 1, 2 * FF), w2, b2.reshape(N_EXP, 1, D))


def _combine_kernel(dest_ref, y_hbm, x_ref, meta_ref, mod_ref, out_ref, buf, sem):
    i = pl.program_id(0)

    def issue(s, carry):
        _row_copy(y_hbm.at[pl.ds(dest_ref[0, s], 1)], buf.at[s % TOP_K, pl.ds(s // TOP_K, 1)], sem).start()
        return carry

    lax.fori_loop(0, TM_R * TOP_K, issue, 0)

    def drain(s, carry):
        _row_copy(y_hbm.at[pl.ds(0, 1)], buf.at[0, pl.ds(0, 1)], sem).wait()
        return carry

    lax.fori_loop(0, TM_R * TOP_K, drain, 0)

    meta = meta_ref[...]
    acc = jnp.zeros((TM_R, D), F32)
    for k in range(TOP_K):
        yk = jnp.concatenate([buf[k, :, c, :] for c in range(D // LANES)], axis=1)
        acc = acc + meta[:, TOP_K + k:TOP_K + k + 1] * yk
    out_ref[...] = x_ref[...] + _mod_vec(mod_ref, _mod_row(i, TM_R), 5) * acc


def _combine(dest, y, x, meta, mod_l):
    return pl.pallas_call(
        _combine_kernel,
        grid=(T // TM_R,),
        in_specs=[pl.BlockSpec((None, 1, TM_R * TOP_K), lambda i: (i, 0, 0), memory_space=pltpu.SMEM),
                  pl.BlockSpec(memory_space=pl.ANY),
                  pl.BlockSpec((TM_R, D), lambda i: (i, 0)),
                  pl.BlockSpec((TM_R, LANES), lambda i: (i, 0)),
                  pl.BlockSpec((16, N_MOD * D), lambda i: (0, 0))],
        out_specs=pl.BlockSpec((TM_R, D), lambda i: (i, 0)),
        out_shape=jax.ShapeDtypeStruct((T, D), F32),
        scratch_shapes=[pltpu.VMEM((TOP_K, TM_R, D // LANES, LANES), F32), pltpu.SemaphoreType.DMA(())],
        compiler_params=_cparams(("arbitrary",)),
        name="moe_combine",
    )(dest, y, x, meta, mod_l)


def _moe_layer(x, mod_l, nw, rw, rb, w1, b1, w2, b2):
    rw_p = jnp.zeros((D, LANES), BF16).at[:, :N_EXP].set(rw.astype(BF16))
    rb_p = jnp.full((1, LANES), -jnp.inf, F32).at[0, :N_EXP].set(rb)
    meta, cnt = _route(x, mod_l, nw, rw_p, rb_p)

    counts = cnt[0, :N_EXP].astype(I32)
    padded = (counts + TM_E - 1) // TM_E * TM_E
    ends = jnp.cumsum(padded)
    offs = ends - padded
    eidx = meta[:, :TOP_K].astype(I32)
    rank = meta[:, 2 * TOP_K:3 * TOP_K].astype(I32)
    dest = (offs[eidx] + rank).reshape(T // TM_R, 1, TM_R * TOP_K)
    n_valid = ends[-1] // TM_E
    starts = jnp.arange(N_ETILES, dtype=I32) * TM_E
    te = jnp.minimum(jnp.searchsorted(ends, starts, side="right").astype(I32), N_EXP - 1)
    te = jnp.where(starts < ends[-1], te, te[jnp.maximum(n_valid - 1, 0)])
    tr = jnp.clip(offs[te] + counts[te] - starts, 0, TM_E)

    xs = _dispatch(dest, x, mod_l, nw)
    y = _experts(te, tr, n_valid.reshape(1), xs, w1, b1, w2, b2)
    return _combine(dest, y, x, meta, mod_l)


def _head_sel():
    g = np.zeros((D, LANES), np.float32)
    for c in range(D):
        g[c, c // HEAD_DIM] = 1.0
    return jnp.asarray(g, BF16)


def _head_rms(x, sel, sel_t, w):
    n = x.shape[1]
    hi, lo = _split_hi_lo(x * x)
    ss = (jnp.dot(hi, sel[:n], preferred_element_type=F32)
          + jnp.dot(lo, sel[:n], preferred_element_type=F32))
    inv = lax.rsqrt(ss * (1.0 / HEAD_DIM) + EPS)
    hi, lo = _split_hi_lo(inv)
    inv_e = (jnp.dot(hi, sel_t[:, :n], preferred_element_type=F32)
             + jnp.dot(lo, sel_t[:, :n], preferred_element_type=F32))
    return x * inv_e * w


def _rope(x, cos, sin):
    n = x.shape[1]
    lane = lax.broadcasted_iota(I32, x.shape, 1)
    first = (lane % (HEAD_DIM // 2)) < (HEAD_DIM // 4)
    rot = jnp.where(first, -pltpu.roll(x, n - HEAD_DIM // 4, axis=1), pltpu.roll(x, HEAD_DIM // 4, axis=1))
    return x * cos + rot * sin


def _qkv_kernel(x_ref, mod_ref, nw_ref, w_ref, sel_ref, selt_ref, qw_ref, kw_ref, cos_ref, sin_ref,
                q_ref, k_ref, v_ref, kf_ref, vf_ref, *, tm):
    i = pl.program_id(0)
    r = _mod_row(i, tm)
    h = _rms(x_ref[...], nw_ref[...]) * (1.0 + _mod_vec(mod_ref, r, 1)) + _mod_vec(mod_ref, r, 0)
    qkv = _bdot(h, w_ref[...])
    nq, nk = ATT_HEADS * HEAD_DIM, KV_HEADS * HEAD_DIM
    q = _head_rms(qkv[:, :nq], sel_ref[...], selt_ref[...], qw_ref[...])
    k = _head_rms(qkv[:, nq:nq + nk], sel_ref[...], selt_ref[...], kw_ref[...])
    v = qkv[:, nq + nk:]
    kf_ref[...] = k
    vf_ref[...] = v
    latent = i * tm >= NP_TOK
    cos = jnp.where(latent, cos_ref[...], 1.0)
    sin = jnp.where(latent, sin_ref[...], 0.0)
    q = _rope(q, cos, sin) * (HEAD_DIM ** -0.5)
    k = _rope(k, cos[:, :nk], sin[:, :nk])
    for hd in range(ATT_HEADS):
        q_ref[hd] = q[:, hd * HEAD_DIM:(hd + 1) * HEAD_DIM].astype(BF16)
    for hd in range(KV_HEADS):
        k_ref[hd] = k[:, hd * HEAD_DIM:(hd + 1) * HEAD_DIM].astype(BF16)
        v_ref[hd] = v[:, hd * HEAD_DIM:(hd + 1) * HEAD_DIM].astype(BF16)


def _qkv(x, mod_l, nw, w, qw, kw, cos, sin, *, tm):
    nk = KV_HEADS * HEAD_DIM
    sel = _head_sel()
    nblk = DEC_SEQ // tm

    def cmap(i):
        return (jnp.maximum(i - NP_TOK // tm, 0) % nblk, 0)

    return pl.pallas_call(
        functools.partial(_qkv_kernel, tm=tm),
        grid=(T // tm,),
        in_specs=[pl.BlockSpec((tm, D), lambda i: (i, 0)),
                  pl.BlockSpec((16, N_MOD * D), lambda i: (0, 0)),
                  pl.BlockSpec((1, D), lambda i: (0, 0)),
                  pl.BlockSpec((D, D + 2 * nk), lambda i: (0, 0)),
                  pl.BlockSpec((D, LANES), lambda i: (0, 0)),
                  pl.BlockSpec((LANES, D), lambda i: (0, 0)),
                  pl.BlockSpec((1, D), lambda i: (0, 0)),
                  pl.BlockSpec((1, nk), lambda i: (0, 0)),
                  pl.BlockSpec((tm, D), cmap),
                  pl.BlockSpec((tm, D), cmap)],
        out_specs=[pl.BlockSpec((ATT_HEADS, tm, HEAD_DIM), lambda i: (0, i, 0)),
                   pl.BlockSpec((KV_HEADS, tm, HEAD_DIM), lambda i: (0, i, 0)),
                   pl.BlockSpec((KV_HEADS, tm, HEAD_DIM), lambda i: (0, i, 0)),
                   pl.BlockSpec((tm, nk), lambda i: (i, 0)),
                   pl.BlockSpec((tm, nk), lambda i: (i, 0))],
        out_shape=[jax.ShapeDtypeStruct((ATT_HEADS, T, HEAD_DIM), BF16),
                   jax.ShapeDtypeStruct((KV_HEADS, T, HEAD_DIM), BF16),
                   jax.ShapeDtypeStruct((KV_HEADS, T, HEAD_DIM), BF16),
                   jax.ShapeDtypeStruct((T, nk), F32),
                   jax.ShapeDtypeStruct((T, nk), F32)],
        compiler_params=_cparams(("arbitrary",)),
        name="attn_qkv",
    )(x, mod_l, nw, w, sel, sel.T, qw, kw, cos, sin)


def _attn_kernel(*refs, has_cache):
    if has_cache:
        q_ref, k_ref, v_ref, kc_ref, vc_ref, o_ref = refs
    else:
        q_ref, k_ref, v_ref, o_ref = refs
    k = k_ref[0]
    v = v_ref[0]
    outs = []
    for j in range(Q_PER_KV):
        q = q_ref[j]
        s = lax.dot_general(q, k, (((1,), (1,)), ((), ())), preferred_element_type=F32)
        m = jnp.max(s, axis=-1, keepdims=True)
        if has_cache:
            sc = lax.dot_general(q, kc_ref[0, 0], (((1,), (1,)), ((), ())), preferred_element_type=F32)
            m = jnp.maximum(m, jnp.max(sc, axis=-1, keepdims=True))
            pc = jnp.exp(sc - m)
        p = jnp.exp(s - m)
        den = jnp.sum(p, axis=-1, keepdims=True)
        o = jnp.dot(p.astype(BF16), v, preferred_element_type=F32)
        if has_cache:
            den = den + jnp.sum(pc, axis=-1, keepdims=True)
            o = o + jnp.dot(pc.astype(BF16), vc_ref[0, 0], preferred_element_type=F32)
        outs.append(o / den)
    o_ref[...] = jnp.concatenate(outs, axis=1).astype(BF16)


def _attn_call(q, k, v, *, L, n_seq, row0, tq, cache, o_prev):
    rb_q = row0 // tq
    rb_k = row0 // L
    nq = L // tq
    has_cache = cache is not None
    in_specs = [pl.BlockSpec((Q_PER_KV, tq, HEAD_DIM), lambda b, g, t: (g, rb_q + b * nq + t, 0)),
                pl.BlockSpec((1, L, HEAD_DIM), lambda b, g, t: (g, rb_k + b, 0)),
                pl.BlockSpec((1, L, HEAD_DIM), lambda b, g, t: (g, rb_k + b, 0))]
    args = [q, k, v]
    if has_cache:
        in_specs += [pl.BlockSpec((1, 1, PAST, HEAD_DIM), lambda b, g, t: (b, g, 0, 0))] * 2
        args += list(cache)
    aliases = {}
    if o_prev is not None:
        in_specs.append(pl.BlockSpec(memory_space=pl.ANY))
        args.append(o_prev)
        aliases = {len(args) - 1: 0}

    def body(*refs):
        if o_prev is not None:
            k_ = len(args) - 1
            refs = refs[:k_] + refs[k_ + 1:]
        _attn_kernel(*refs, has_cache=has_cache)

    gw = Q_PER_KV * HEAD_DIM
    return pl.pallas_call(
        body,
        grid=(n_seq, KV_HEADS, nq),
        in_specs=in_specs,
        out_specs=pl.BlockSpec((tq, gw), lambda b, g, t: (rb_q + b * nq + t, g)),
        out_shape=jax.ShapeDtypeStruct((T, D), BF16),
        input_output_aliases=aliases,
        compiler_params=_cparams(("arbitrary", "arbitrary", "arbitrary")),
        name=f"attn_L{L}",
    )(*args)


def _rope_tables():
    rows = DEC_SEQ // GRID_W
    row = jnp.repeat(jnp.arange(rows), GRID_W)
    colp = jnp.tile(jnp.arange(GRID_W), rows)
    pos = jnp.stack([row, colp], axis=-1).astype(F32)
    half = HEAD_DIM // 2
    inv_freq = ROPE_THETA ** (-jnp.arange(0, half, 2, dtype=F32) / half)
    ang = pos[:, :, None] * inv_freq
    ang = jnp.concatenate([ang, ang], axis=-1).reshape(DEC_SEQ, HEAD_DIM)
    ang = jnp.tile(ang, (1, ATT_HEADS))
    return jnp.cos(ang), jnp.sin(ang)


def _final_kernel(x_ref, w_ref, o_ref):
    o_ref[...] = _rms(x_ref[...], w_ref[...])


def _final_norm(x, w, *, row0, n_rows, tm):
    rb = row0 // tm
    return pl.pallas_call(
        _final_kernel,
        grid=(n_rows // tm,),
        in_specs=[pl.BlockSpec((tm, D), lambda i: (rb + i, 0)),
                  pl.BlockSpec((1, D), lambda i: (0, 0))],
        out_specs=pl.BlockSpec((tm, D), lambda i: (i, 0)),
        out_shape=jax.ShapeDtypeStruct((n_rows, D), F32),
        compiler_params=_cparams(("arbitrary",)),
        name="final_norm",
    )(x, w)


def _ssd_params(conv_w, conv_b, dt_bias, a_log, d_skip):
    def dt_layout(p):
        g = p.reshape(2, SSD_G, HPG).transpose(1, 0, 2).reshape(SSD_G, 2 * HPG)
        return jnp.zeros((SSD_G, LANES), F32).at[:, :2 * HPG].set(g).reshape(1, SSD_G * LANES)

    ni = SSD_INNER
    nb = SSD_G * SSD_N
    return dict(
        cwx=conv_w[:, :ni], cwb=conv_w[:, ni:ni + nb], cwc=conv_w[:, ni + nb:],
        cbx=conv_b[None, :ni], cbb=conv_b[None, ni:ni + nb], cbc=conv_b[None, ni + nb:],
        dtb=dt_layout(dt_bias), alog=dt_layout(a_log),
        dcol=jnp.repeat(d_skip, SSD_P)[None, :], rsel=_ssd_rsel())


def _ssm_in_weights(w):
    ni, cd = SSD_INNER, SSD_INNER + 2 * SSD_G * SSD_N
    o_xbc, o_dt = ni, ni + cd
    o_hq = o_dt + 2 * SSD_HEADS
    o_hf, o_hi = o_hq + HG_QK, o_hq + 3 * HG_QK
    o_hg = o_hi + HG_IV
    main = jnp.concatenate([w[:, :ni], w[:, o_hq:o_hf], w[:, o_hi:o_hg], w[:, o_hg:o_hg + HG_IV],
                            w[:, o_hf:o_hi], w[:, o_xbc:o_dt]], axis=1).astype(BF16)
    wdt = w[:, o_dt:o_hq].reshape(D, 2, SSD_G, HPG).transpose(0, 2, 1, 3).reshape(D, SSD_G, 2 * HPG)
    wdt = jnp.zeros((D, SSD_G, LANES), F32).at[:, :, :2 * HPG].set(wdt).reshape(D, SSD_G * LANES)
    return main, wdt.astype(BF16)


def kernel(x_prompt, x_sample, c, c_ctx, state_ssd, state_hgrn, cache_k, cache_v, mod_w, mod_b, norm1_w,
           norm2_w, ssm_in_w, ssd_conv_w, ssd_conv_b, ssd_dt_bias, ssd_a_log, ssd_d, ssd_norm_w, hg_lb,
           hg_norm_w, ssm_out_w, attn_in_w, q_norm_w, k_norm_w, attn_out_w, router_w, router_b, exp_w1,
           exp_b1, exp_w2, exp_b2, final_norm_w):
    xp = x_prompt.reshape(NP_TOK, D)
    xs = x_sample.reshape(NS_TOK, D)
    cvec = jnp.zeros((16, D), F32).at[0].set(c_ctx).at[1:1 + DEC_BATCH].set(c)
    mod = _mod_table(cvec, mod_w, mod_b)

    w_main, w_dt = _ssm_in_weights(ssm_in_w[0])
    proj, dtp = _inproj(xp, xs, mod[0], norm1_w[0][None], w_main, w_dt, tm=1024, tn=1280)
    prm = _ssd_params(ssd_conv_w[0], ssd_conv_b[0], ssd_dt_bias[0], ssd_a_log[0], ssd_d[0])
    y, st_ssd = _ssd_call(proj, dtp, prm, L=SEQ, n_seq=BATCH, row0=0, s0=None, y_prev=None, want_state=True)
    s0 = state_ssd[:, 0].reshape(DEC_BATCH, 2, SSD_G, GW, SSD_N)
    (y,) = _ssd_call(proj, dtp, prm, L=DEC_SEQ, n_seq=DEC_BATCH, row0=NP_TOK, s0=s0, y_prev=y,
                     want_state=False)
    hg_nw = hg_norm_w[0][None]
    o, st_hg = _gla_call(proj, hg_lb, hg_nw, layer=0, L=SEQ, n_seq=BATCH, row0=0, s0=None, o_prev=None,
                         want_state=True)
    (o,) = _gla_call(proj, hg_lb, hg_nw, layer=0, L=DEC_SEQ, n_seq=DEC_BATCH, row0=NP_TOK,
                     s0=state_hgrn[:, 0], o_prev=o, want_state=False)
    x = _ssm_out(xp, xs, y, proj, o, mod[0], ssd_norm_w[0][None], ssm_out_w[0].astype(BF16), tm=512)
    x = _moe_layer(x, mod[0], norm2_w[0][None], router_w[0], router_b[0], exp_w1[0], exp_b1[0], exp_w2[0],
                   exp_b2[0])

    cos, sin = _rope_tables()
    nk = KV_HEADS * HEAD_DIM
    q, k, v, kf, vf = _qkv(x, mod[1], norm1_w[1][None], attn_in_w[0].astype(BF16),
                           jnp.tile(q_norm_w[0], ATT_HEADS)[None], jnp.tile(k_norm_w[0], KV_HEADS)[None],
                           cos, sin, tm=256)
    a = _attn_call(q, k, v, L=SEQ, n_seq=BATCH, row0=0, tq=SEQ, cache=None, o_prev=None)
    ck = cache_k[:, 0].transpose(0, 2, 1, 3).astype(BF16)
    cv = cache_v[:, 0].transpose(0, 2, 1, 3).astype(BF16)
    a = _attn_call(q, k, v, L=DEC_SEQ, n_seq=DEC_BATCH, row0=NP_TOK, tq=256, cache=(ck, cv), o_prev=a)
    x = _attn_out(x, a, mod[1], attn_out_w[0].astype(BF16), tm=512)
    x = _moe_layer(x, mod[1], norm2_w[1][None], router_w[1], router_b[1], exp_w1[1], exp_b1[1], exp_w2[1],
                   exp_b2[1])

    y_prompt = _final_norm(x, final_norm_w[None], row0=0, n_rows=NP_TOK, tm=512).reshape(BATCH, SEQ, D)
    y_sample = _final_norm(x, final_norm_w[None], row0=NP_TOK, n_rows=NS_TOK, tm=512).reshape(
        DEC_BATCH, DEC_SEQ, D)
    new_ssd = st_ssd.reshape(BATCH, 1, 2, SSD_HEADS, SSD_P, SSD_N)
    new_hg = st_hg.reshape(BATCH, 1, 2, HG_HEADS, HG_K, HG_V)
    new_k = kf[:NP_TOK].reshape(BATCH, 1, SEQ, KV_HEADS, HEAD_DIM)
    new_v = vf[:NP_TOK].reshape(BATCH, 1, SEQ, KV_HEADS, HEAD_DIM)
    return (y_prompt, y_sample, new_ssd, new_hg, new_k, new_v)
```

```python
import functools

import jax
import jax.numpy as jnp
import numpy as np
from jax import lax
from jax.experimental import pallas as pl
from jax.experimental.pallas import tpu as pltpu

F32 = jnp.float32
BF16 = jnp.bfloat16
I32 = jnp.int32

D = 1024
BATCH, SEQ = 32, 256
DEC_BATCH, DEC_SEQ = 8, 1024
PAST = 256
NP_TOK = BATCH * SEQ
NS_TOK = DEC_BATCH * DEC_SEQ
T = NP_TOK + NS_TOK
DEPTH = 2
N_MOD = 6
EPS = 1e-6
GRID_W = 64
ROPE_THETA = 10000.0

SSD_HEADS, SSD_P, SSD_N, SSD_G = 16, 64, 128, 2
SSD_INNER = SSD_HEADS * SSD_P
SSD_CONV = 5
HPG = SSD_HEADS // SSD_G
GW = HPG * SSD_P
HG_HEADS, HG_K, HG_V = 8, 128, 128
HG_QK = HG_HEADS * HG_K
HG_IV = HG_HEADS * HG_V
ATT_HEADS, KV_HEADS, HEAD_DIM = 16, 4, 64
Q_PER_KV = ATT_HEADS // KV_HEADS
N_EXP, TOP_K, FF = 32, 4, 1024
SWIGLU_ALPHA, SWIGLU_LIMIT = 1.702, 7.0

LANES = 128
SUBLANES = 8
VMEM_LIMIT = 56 * 1024 * 1024

COL_Z, COL_HQ, COL_HI, COL_HG, COL_HF, COL_XBC = 0, 1024, 2048, 3072, 4096, 6144
PROJ_W = 7680

SSD_TC = 256
GLA_TC = 64
GLA_GROUP = 256

TB = 512
NB = T // TB
SEG = SUBLANES
RB = TB * TOP_K + N_EXP * SEG
TM_E = 256
SEG_SIZES = tuple(SEG << i for i in range((TB // SEG).bit_length() - 1, -1, -1))
PAD_SIZES = tuple(s for s in SEG_SIZES if s < TM_E)
P_ROWS_MAX = T * TOP_K + NB * N_EXP * (SEG - 1) + N_EXP * (TM_E - SEG)
N_ETILES = -(-P_ROWS_MAX // TM_E)
P_ROWS = N_ETILES * TM_E


def _cparams(sem):
    return pltpu.CompilerParams(dimension_semantics=sem, vmem_limit_bytes=VMEM_LIMIT)


def _silu(x):
    return x * jax.nn.sigmoid(x)


def _bdot(a, b):
    return jnp.dot(a.astype(BF16), b.astype(BF16), preferred_element_type=F32)


def _bdot_nt(a, b):
    return lax.dot_general(a.astype(BF16), b.astype(BF16), (((1,), (1,)), ((), ())),
                           preferred_element_type=F32)


def _xdot(a, b):
    return jnp.dot(a, b, preferred_element_type=F32, precision=lax.Precision.HIGHEST)


def _mod_row(i, tm):
    start = i * tm
    return jnp.where(start < NP_TOK, 0, 1 + (start - NP_TOK) // DEC_SEQ)


def _mod_vec(mod_ref, r, k):
    return mod_ref[pl.ds(r, 1), k * D:(k + 1) * D]


def _rms(x, w):
    return x * lax.rsqrt(jnp.mean(x * x, axis=-1, keepdims=True) + EPS) * w


def _two_src_specs(tm, width=D):
    npt = NP_TOK // tm
    return [pl.BlockSpec((tm, width), lambda i, *_: (jnp.minimum(i, npt - 1), 0)),
            pl.BlockSpec((tm, width), lambda i, *_: (jnp.maximum(i - npt, 0), 0))]


def _pick(i, tm, p_ref, s_ref):
    return jnp.where(i * tm < NP_TOK, p_ref[...], s_ref[...])


def _mod_kernel(c_ref, w_ref, b_ref, o_ref):
    o_ref[0] = _bdot(_silu(c_ref[...]), w_ref[0]) + b_ref[0]


def _mod_table(cvec, mod_w, mod_b):
    tn = 1536
    return pl.pallas_call(
        _mod_kernel,
        grid=(DEPTH, N_MOD * D // tn),
        in_specs=[pl.BlockSpec((16, D), lambda l, n: (0, 0)),
                  pl.BlockSpec((1, D, tn), lambda l, n: (l, 0, n)),
                  pl.BlockSpec((1, 1, tn), lambda l, n: (l, 0, n))],
        out_specs=pl.BlockSpec((1, 16, tn), lambda l, n: (l, 0, n)),
        out_shape=jax.ShapeDtypeStruct((DEPTH, 16, N_MOD * D), F32),
        compiler_params=_cparams(("arbitrary", "arbitrary")),
        name="mod_table",
    )(cvec, mod_w, mod_b.reshape(DEPTH, 1, N_MOD * D))


def _inproj_kernel(xp_ref, xs_ref, mod_ref, nw_ref, w_ref, wx_ref, o_ref, ox_ref, h_ref, *, tm):
    i = pl.program_id(0)

    @pl.when(pl.program_id(1) == 0)
    def _():
        x = _pick(i, tm, xp_ref, xs_ref)
        r = _mod_row(i, tm)
        h = _rms(x, nw_ref[...]) * (1.0 + _mod_vec(mod_ref, r, 1)) + _mod_vec(mod_ref, r, 0)
        hb = h.astype(BF16)
        h_ref[...] = hb
        ox_ref[...] = jnp.dot(hb, wx_ref[...], preferred_element_type=F32)

    o_ref[...] = jnp.dot(h_ref[...], w_ref[...], preferred_element_type=F32)


def _inproj(xp, xs, mod_l, nw, w, wx, *, tm, tn):
    n_out, nx = w.shape[1], wx.shape[1]
    return pl.pallas_call(
        functools.partial(_inproj_kernel, tm=tm),
        grid=(T // tm, n_out // tn),
        in_specs=_two_src_specs(tm) + [
            pl.BlockSpec((16, N_MOD * D), lambda i, n: (0, 0)),
            pl.BlockSpec((1, D), lambda i, n: (0, 0)),
            pl.BlockSpec((D, tn), lambda i, n: (0, n)),
            pl.BlockSpec((D, nx), lambda i, n: (0, 0))],
        out_specs=[pl.BlockSpec((tm, tn), lambda i, n: (i, n)),
                   pl.BlockSpec((tm, nx), lambda i, n: (i, 0))],
        out_shape=[jax.ShapeDtypeStruct((T, n_out), F32), jax.ShapeDtypeStruct((T, nx), F32)],
        scratch_shapes=[pltpu.VMEM((tm, D), BF16)],
        compiler_params=_cparams(("arbitrary", "arbitrary")),
        name="inproj",
    )(xp, xs, mod_l, nw, w, wx)


def _tri(n, upper):
    r = lax.broadcasted_iota(I32, (n, n), 0)
    c = lax.broadcasted_iota(I32, (n, n), 1)
    return (c >= r) if upper else (c <= r)


def _conv_silu(x, w, b, L):
    row = lax.broadcasted_iota(I32, (L, 1), 0)
    acc = x * w[2:3] + b
    for k in (0, 1, 3, 4):
        off = k - 2
        shifted = pltpu.roll(x, (-off) % L, axis=0)
        ok = (row + off >= 0) & (row + off < L)
        acc = acc + jnp.where(ok, shifted, 0.0) * w[k:k + 1]
    return _silu(acc)


def _split_hi_lo(x):
    hi = x.astype(BF16)
    return hi, (x - hi.astype(F32)).astype(BF16)


def _ssd_kernel(*refs, L, has_init, want_state):
    (x_ref, b_ref, c_ref, dt_ref, cwx_ref, cwb_ref, cwc_ref, cbx_ref, cbb_ref, cbc_ref,
     dtb_ref, alog_ref, dcol_ref, rsel_ref) = refs[:14]
    pos = 14
    s0_ref = None
    if has_init:
        s0_ref = refs[pos]
        pos += 1
    y_ref = refs[pos]
    pos += 1
    st_ref = None
    if want_state:
        st_ref = refs[pos]
        pos += 1
    xs_s, bs_s, cs_s, y_s = refs[pos:pos + 4]

    tc = SSD_TC
    nc = L // tc
    xs_s[...] = _conv_silu(x_ref[...], cwx_ref[...], cbx_ref[...], L)
    bs_s[...] = _conv_silu(b_ref[...], cwb_ref[...], cbb_ref[...], L)
    cs_s[...] = _conv_silu(c_ref[...], cwc_ref[...], cbc_ref[...], L)
    y_s[...] = xs_s[...] * dcol_ref[...]

    z = dt_ref[...] + dtb_ref[...]
    dt_all = jnp.maximum(z, 0.0) + jnp.log(1.0 + jnp.exp(-jnp.abs(z)))
    a_row = -jnp.exp(alog_ref[...])

    for d in (0, 1):
        upper = d == 1
        tri = _tri(tc, upper)
        tri_f = tri.astype(F32)
        rsel = rsel_ref[d]
        st = s0_ref[d].T if has_init else None
        order = range(nc) if d == 0 else range(nc - 1, -1, -1)
        for c in order:
            sl = slice(c * tc, (c + 1) * tc)
            x = xs_s[sl]
            bm = bs_s[sl]
            cm = cs_s[sl]
            dt = dt_all[sl]
            cum = _xdot(tri_f, dt * a_row)
            cum_t = cum.T
            end = cum[0:1] if upper else cum[tc - 1:tc]
            stack = jnp.concatenate(
                [dt, jnp.exp(cum), jnp.exp(end - cum), jnp.broadcast_to(jnp.exp(end), (SUBLANES, LANES))],
                axis=0)
            hi, lo = _split_hi_lo(stack)
            ex = (jnp.dot(hi, rsel, preferred_element_type=F32)
                  + jnp.dot(lo, rsel, preferred_element_type=F32))
            dt_e, cum_e, dec_e = ex[0:tc], ex[tc:2 * tc], ex[2 * tc:3 * tc]
            end_e = ex[3 * tc:3 * tc + 1]
            xdt = x * dt_e
            cb = _bdot_nt(cm, bm)
            ys = []
            for h in range(HPG):
                j = d * HPG + h
                seg = cum[:, j:j + 1] - cum_t[j:j + 1, :]
                lm = jnp.exp(jnp.where(tri, seg, -jnp.inf))
                ys.append(_bdot(cb * lm, xdt[:, h * SSD_P:(h + 1) * SSD_P]))
            y = jnp.concatenate(ys, axis=1)
            if st is not None:
                y = y + _bdot(cm, st) * cum_e
            y_s[sl] = y_s[sl] + y
            sc = _bdot(bm.T, xdt * dec_e)
            st = sc if st is None else st * end_e + sc
        if want_state:
            st_ref[d] = st.T

    y_ref[...] = y_s[...]


def _ssd_rsel():
    r = np.zeros((2, LANES, GW), np.float32)
    for d in range(2):
        for h in range(HPG):
            r[d, d * HPG + h, h * SSD_P:(h + 1) * SSD_P] = 1.0
    return jnp.asarray(r, BF16)


def _ssd_call(proj, dtp, prm, *, L, n_seq, row0, s0, want_state):
    rb = row0 // L
    has_init = s0 is not None
    xbc = COL_XBC
    in_specs = [
        pl.BlockSpec((L, GW), lambda b, g: (rb + b, xbc // GW + g)),
        pl.BlockSpec((L, SSD_N), lambda b, g: (rb + b, (xbc + SSD_INNER) // SSD_N + g)),
        pl.BlockSpec((L, SSD_N), lambda b, g: (rb + b, (xbc + SSD_INNER + SSD_G * SSD_N) // SSD_N + g)),
        pl.BlockSpec((L, LANES), lambda b, g: (rb + b, g)),
        pl.BlockSpec((SSD_CONV, GW), lambda b, g: (0, g)),
        pl.BlockSpec((SSD_CONV, SSD_N), lambda b, g: (0, g)),
        pl.BlockSpec((SSD_CONV, SSD_N), lambda b, g: (0, g)),
        pl.BlockSpec((1, GW), lambda b, g: (0, g)),
        pl.BlockSpec((1, SSD_N), lambda b, g: (0, g)),
        pl.BlockSpec((1, SSD_N), lambda b, g: (0, g)),
        pl.BlockSpec((1, LANES), lambda b, g: (0, g)),
        pl.BlockSpec((1, LANES), lambda b, g: (0, g)),
        pl.BlockSpec((1, GW), lambda b, g: (0, g)),
        pl.BlockSpec((2, LANES, GW), lambda b, g: (0, 0, 0)),
    ]
    args = [proj, proj, proj, dtp, prm["cwx"], prm["cwb"], prm["cwc"], prm["cbx"], prm["cbb"], prm["cbc"],
            prm["dtb"], prm["alog"], prm["dcol"], prm["rsel"]]
    if has_init:
        in_specs.append(pl.BlockSpec((None, 2, None, GW, SSD_N), lambda b, g: (b, 0, g, 0, 0)))
        args.append(s0)
    out_specs = [pl.BlockSpec((L, GW), lambda b, g: (b, g))]
    out_shape = [jax.ShapeDtypeStruct((n_seq * L, SSD_INNER), F32)]
    if want_state:
        out_specs.append(pl.BlockSpec((None, 2, None, GW, SSD_N), lambda b, g: (b, 0, g, 0, 0)))
        out_shape.append(jax.ShapeDtypeStruct((n_seq, 2, SSD_G, GW, SSD_N), F32))
    return pl.pallas_call(
        functools.partial(_ssd_kernel, L=L, has_init=has_init, want_state=want_state),
        grid=(n_seq, SSD_G),
        in_specs=in_specs, out_specs=out_specs, out_shape=out_shape,
        scratch_shapes=[pltpu.VMEM((L, GW), F32), pltpu.VMEM((L, SSD_N), F32),
                        pltpu.VMEM((L, SSD_N), F32), pltpu.VMEM((L, GW), F32)],
        compiler_params=_cparams(("arbitrary", "arbitrary")),
        name=f"ssd_L{L}",
    )(*args)


def _gla_kernel(*refs, L, layer, has_init, want_state):
    q_ref, ff_ref, fb_ref, v_ref, g_ref, lb_ref, nw_ref = refs[:7]
    pos = 7
    s0_ref = None
    if has_init:
        s0_ref = refs[pos]
        pos += 1
    o_ref = refs[pos]
    pos += 1
    st_ref = None
    if want_state:
        st_ref = refs[pos]
        pos += 1
    qs, ks, ls, os_ = refs[pos:pos + 4]

    tc, gr = GLA_TC, GLA_GROUP
    ng, cpg = L // gr, gr // tc
    lbr = lb_ref[...]
    e = jnp.exp(lbr - jnp.max(lbr, axis=0, keepdims=True))
    sm = e / jnp.sum(e, axis=0, keepdims=True)
    lb = sm[0]
    for j in range(1, layer + 1):
        lb = lb + sm[j]

    qs[...] = _silu(q_ref[...])
    for d, fr in ((0, ff_ref), (1, fb_ref)):
        logit = fr[...]
        lbd = lb[d:d + 1]
        f = lbd + (1.0 - lbd) * jax.nn.sigmoid(logit)
        ks[d] = (1.0 - lbd) * jax.nn.sigmoid(-logit)
        ls[d] = jnp.log(f)

    r_i = lax.broadcasted_iota(I32, (gr, gr), 0)
    c_i = lax.broadcasted_iota(I32, (gr, gr), 1)
    sh = tc.bit_length() - 1
    same = (r_i >> sh) == (c_i >> sh)
    bd = [(same & (c_i <= r_i)).astype(BF16), (same & (c_i >= r_i)).astype(BF16)]
    tri = [_tri(tc, False), _tri(tc, True)]

    def group(d, g0, st):
        lf = ls[d, pl.ds(g0, gr), :]
        p0 = lf.astype(BF16)
        r1 = lf - p0.astype(F32)
        p1 = r1.astype(BF16)
        p2 = (r1 - p1.astype(F32)).astype(BF16)
        cum = (jnp.dot(bd[d], p0, preferred_element_type=F32) + jnp.dot(bd[d], p1, preferred_element_type=F32)
               + jnp.dot(bd[d], p2, preferred_element_type=F32))
        q = qs[pl.ds(g0, gr), :]
        k = ks[d, pl.ds(g0, gr), :]
        v = v_ref[pl.ds(g0, gr), :]
        upper = d == 1
        mid = tc // 2 if upper else tc // 2 - 1
        endr = 0 if upper else tc - 1
        outs = [None] * cpg
        for c in (range(cpg - 1, -1, -1) if upper else range(cpg)):
            sl = slice(c * tc, (c + 1) * tc)
            cc, qc, kc, vc = cum[sl], q[sl], k[sl], v[sl]
            ref_row = cc[mid:mid + 1]
            end = cc[endr:endr + 1]
            att = _bdot_nt(qc * jnp.exp(cc - ref_row), kc * jnp.exp(ref_row - cc))
            att = jnp.where(tri[d], att, 0.0)
            outs[c] = _bdot(att, vc) + _bdot_nt(qc * jnp.exp(cc), st)
            st = st * jnp.exp(end) + _bdot(vc.T, kc * jnp.exp(end - cc))
        return jnp.concatenate(outs, axis=0), st

    def init(d):
        return s0_ref[d].T if has_init else jnp.zeros((HG_V, HG_K), F32)

    if ng == 1:
        of, sf = group(0, 0, init(0))
        ob, sb = group(1, 0, init(1))
        o = of + ob
    else:
        os_[...] = jnp.zeros_like(os_)

        def body(i, carry):
            sf, sb = carry
            gf = pl.multiple_of(i * gr, gr)
            gb = pl.multiple_of((ng - 1 - i) * gr, gr)
            of, sf = group(0, gf, sf)
            ob, sb = group(1, gb, sb)
            os_[pl.ds(gf, gr), :] = os_[pl.ds(gf, gr), :] + of
            os_[pl.ds(gb, gr), :] = os_[pl.ds(gb, gr), :] + ob
            return sf, sb

        sf, sb = lax.fori_loop(0, ng, body, (init(0), init(1)))
        o = os_[...]
    if want_state:
        st_ref[0] = sf.T
        st_ref[1] = sb.T
    o_ref[...] = _rms(o, nw_ref[...]) * _silu(g_ref[...])


def _gla_call(proj, hg_lb, hg_nw, *, layer, L, n_seq, row0, s0, want_state):
    rb = row0 // L
    has_init = s0 is not None

    def col(c0):
        return lambda b, h: (rb + b, c0 // HG_K + h)

    in_specs = [
        pl.BlockSpec((L, HG_K), col(COL_HQ)),
        pl.BlockSpec((L, HG_K), col(COL_HF)),
        pl.BlockSpec((L, HG_K), col(COL_HF + HG_QK)),
        pl.BlockSpec((L, HG_V), col(COL_HI)),
        pl.BlockSpec((L, HG_V), col(COL_HG)),
        pl.BlockSpec((DEPTH + 1, 2, HG_K), lambda b, h: (0, 0, h)),
        pl.BlockSpec((1, HG_V), lambda b, h: (0, h)),
    ]
    args = [proj, proj, proj, proj, proj, hg_lb, hg_nw]
    if has_init:
        in_specs.append(pl.BlockSpec((None, 2, None, HG_K, HG_V), lambda b, h: (b, 0, h, 0, 0)))
        args.append(s0)
    out_specs = [pl.BlockSpec((L, HG_V), lambda b, h: (b, h))]
    out_shape = [jax.ShapeDtypeStruct((n_seq * L, HG_IV), F32)]
    if want_state:
        out_specs.append(pl.BlockSpec((None, 2, None, HG_K, HG_V), lambda b, h: (b, 0, h, 0, 0)))
        out_shape.append(jax.ShapeDtypeStruct((n_seq, 2, HG_HEADS, HG_K, HG_V), F32))
    return pl.pallas_call(
        functools.partial(_gla_kernel, L=L, layer=layer, has_init=has_init, want_state=want_state),
        grid=(n_seq, HG_HEADS),
        in_specs=in_specs, out_specs=out_specs, out_shape=out_shape,
        scratch_shapes=[pltpu.VMEM((L, HG_K), F32), pltpu.VMEM((2, L, HG_K), F32),
                        pltpu.VMEM((2, L, HG_K), F32), pltpu.VMEM((L, HG_V), F32)],
        compiler_params=_cparams(("arbitrary", "arbitrary")),
        name=f"gla_L{L}",
    )(*args)


def _ssm_out_kernel(xp_ref, xs_ref, yp_ref, ys_ref, op_ref, os_ref, z_ref, mod_ref, nw_ref, w_ref,
                    out_ref, *, tm):
    i = pl.program_id(0)
    g = _pick(i, tm, yp_ref, ys_ref) * _silu(z_ref[...])
    mix = (_bdot(_rms(g, nw_ref[...]), w_ref[0:SSD_INNER, :])
           + _bdot(_pick(i, tm, op_ref, os_ref), w_ref[SSD_INNER:, :]))
    out_ref[...] = _pick(i, tm, xp_ref, xs_ref) + _mod_vec(mod_ref, _mod_row(i, tm), 2) * mix


def _ssm_out(xp, xs, yp, ys, op, os_, proj, mod_l, nw, w, *, tm):
    return pl.pallas_call(
        functools.partial(_ssm_out_kernel, tm=tm),
        grid=(T // tm,),
        in_specs=_two_src_specs(tm) + _two_src_specs(tm, SSD_INNER) + _two_src_specs(tm, HG_IV) + [
            pl.BlockSpec((tm, SSD_INNER), lambda i: (i, COL_Z // SSD_INNER)),
            pl.BlockSpec((16, N_MOD * D), lambda i: (0, 0)),
            pl.BlockSpec((1, SSD_INNER), lambda i: (0, 0)),
            pl.BlockSpec((SSD_INNER + HG_IV, D), lambda i: (0, 0))],
        out_specs=pl.BlockSpec((tm, D), lambda i: (i, 0)),
        out_shape=jax.ShapeDtypeStruct((T, D), F32),
        compiler_params=_cparams(("arbitrary",)),
        name="ssm_out",
    )(xp, xs, yp, ys, op, os_, proj, mod_l, nw, w)


def _attn_out_kernel(x_ref, ap_ref, as_ref, mod_ref, w_ref, out_ref, *, tm):
    i = pl.program_id(0)
    mix = jnp.dot(_pick(i, tm, ap_ref, as_ref), w_ref[...], preferred_element_type=F32)
    out_ref[...] = x_ref[...] + _mod_vec(mod_ref, _mod_row(i, tm), 2) * mix


def _attn_out(x, ap, as_, mod_l, w, *, tm):
    return pl.pallas_call(
        functools.partial(_attn_out_kernel, tm=tm),
        grid=(T // tm,),
        in_specs=[pl.BlockSpec((tm, D), lambda i: (i, 0))] + _two_src_specs(tm) + [
            pl.BlockSpec((16, N_MOD * D), lambda i: (0, 0)),
            pl.BlockSpec((D, D), lambda i: (0, 0))],
        out_specs=pl.BlockSpec((tm, D), lambda i: (i, 0)),
        out_shape=jax.ShapeDtypeStruct((T, D), F32),
        compiler_params=_cparams(("arbitrary",)),
        name="attn_out",
    )(x, ap, as_, mod_l, w)


def _route_kernel(x_ref, mod_ref, nw_ref, rw_ref, rb_ref, h_ref, meta_ref, metat_ref, cnt_ref):
    i = pl.program_id(0)
    r = _mod_row(i, TB)
    h = _rms(x_ref[...], nw_ref[...]) * (1.0 + _mod_vec(mod_ref, r, 4)) + _mod_vec(mod_ref, r, 3)
    hb = h.astype(BF16)
    h_ref[...] = hb
    logits = jnp.dot(hb, rw_ref[...], preferred_element_type=F32) + rb_ref[...]
    lane = lax.broadcasted_iota(I32, (TB, LANES), 1)
    lane_f = lane.astype(F32)
    vals, hots, idxs = [], [], []
    for _ in range(TOP_K):
        m = jnp.max(logits, axis=-1, keepdims=True)
        idx = jnp.min(jnp.where(logits == m, lane_f, float(LANES)), axis=-1, keepdims=True)
        hot = lane_f == idx
        vals.append(m)
        idxs.append(idx)
        hots.append(hot)
        logits = jnp.where(hot, -jnp.inf, logits)
    es = [jnp.exp(v - vals[0]) for v in vals]
    den = es[0] + es[1] + es[2] + es[3]

    r_i = lax.broadcasted_iota(I32, (TB, TB), 0)
    c_i = lax.broadcasted_iota(I32, (TB, TB), 1)
    below = (c_i < r_i).astype(BF16)
    base = jnp.zeros((1, LANES), F32)
    ranks = []
    for k in range(TOP_K):
        hot_f = hots[k].astype(F32)
        before = jnp.dot(below, hots[k].astype(BF16), preferred_element_type=F32)
        ranks.append(base + before)
        base = base + jnp.sum(hot_f, axis=0, keepdims=True)
    units = jnp.floor((base + (SEG - 1)) * (1.0 / SEG))
    lr = lax.broadcasted_iota(I32, (LANES, LANES), 0)
    lc = lax.broadcasted_iota(I32, (LANES, LANES), 1)
    seg_start = SEG * jnp.dot(jnp.broadcast_to(units, (SUBLANES, LANES)).astype(BF16),
                              (lr < lc).astype(BF16), preferred_element_type=F32)[0:1]
    meta = jnp.zeros((TB, LANES), F32)
    for k in range(TOP_K):
        row = jnp.sum(hots[k].astype(F32) * (seg_start + ranks[k]), axis=-1, keepdims=True)
        meta = jnp.where(lane == k, idxs[k], meta)
        meta = jnp.where(lane == TOP_K + k, es[k] / den, meta)
        meta = jnp.where(lane == 2 * TOP_K + k, row, meta)
    meta_ref[...] = meta
    metat_ref[...] = meta.T
    cnt_ref[...] = base


def _route(x, mod_l, nw, rw, rb):
    return pl.pallas_call(
        _route_kernel,
        grid=(NB,),
        in_specs=[pl.BlockSpec((TB, D), lambda i: (i, 0)),
                  pl.BlockSpec((16, N_MOD * D), lambda i: (0, 0)),
                  pl.BlockSpec((1, D), lambda i: (0, 0)),
                  pl.BlockSpec((D, LANES), lambda i: (0, 0)),
                  pl.BlockSpec((1, LANES), lambda i: (0, 0))],
        out_specs=[pl.BlockSpec((TB, D), lambda i: (i, 0)),
                   pl.BlockSpec((TB, LANES), lambda i: (i, 0)),
                   pl.BlockSpec((LANES, TB), lambda i: (0, i)),
                   pl.BlockSpec((None, 1, LANES), lambda i: (i, 0, 0))],
        out_shape=[jax.ShapeDtypeStruct((T, D), BF16),
                   jax.ShapeDtypeStruct((T, LANES), F32),
                   jax.ShapeDtypeStruct((LANES, T), F32),
                   jax.ShapeDtypeStruct((NB, 1, LANES), F32)],
        compiler_params=_cparams(("arbitrary",)),
        name="moe_route",
    )(x, mod_l, nw, rw, rb)


def _copy(src, dst, sem):
    return pltpu.make_async_copy(src, dst, sem)


def _segment_copies(n, src, dst, sizes, make, wait):
    src = jnp.asarray(src, I32)
    dst = jnp.asarray(dst, I32)
    for size in sizes:
        take = n & size

        @pl.when(take != 0)
        def _(src=src, dst=dst, size=size):
            cp = make(pl.multiple_of(src, SEG), pl.multiple_of(dst, SEG), size)
            if wait:
                cp.wait()
            else:
                cp.start()

        src = src + take
        dst = dst + take


def _sort_push_kernel(tab_ref, ztab_ref, nv_ref, metat_ref, h_ref, xs_hbm, buf, zb, sem, zsem):
    b = pl.program_id(0)

    def zero_copies(wait):
        def per_expert(e, carry):
            _segment_copies(ztab_ref[1, e], 0, ztab_ref[0, e], PAD_SIZES,
                            lambda s, d, size: _copy(zb.at[pl.ds(0, size)], xs_hbm.at[pl.ds(d, size)], zsem), wait)
            return carry

        lax.fori_loop(0, N_EXP, per_expert, 0)

        def per_tile(j, carry):
            cp = _copy(zb, xs_hbm.at[pl.ds(pl.multiple_of(j * TM_E, TM_E), TM_E)], zsem)
            if wait:
                cp.wait()
            else:
                cp.start()
            return carry

        lax.fori_loop(nv_ref[0], N_ETILES, per_tile, 0)

    @pl.when(b == 0)
    def _():
        zb[...] = jnp.zeros_like(zb)
        zero_copies(False)

    ch = 256
    for c in range(RB // ch):
        rows = (lax.broadcasted_iota(I32, (ch, TB), 0) + c * ch).astype(F32)
        hit = rows == metat_ref[2 * TOP_K:2 * TOP_K + 1, :]
        for k in range(1, TOP_K):
            hit = hit | (rows == metat_ref[2 * TOP_K + k:2 * TOP_K + k + 1, :])
        buf[c * ch:(c + 1) * ch, :] = jnp.dot(hit.astype(BF16), h_ref[...], preferred_element_type=F32)

    def push(wait):
        def per_expert(e, carry):
            _segment_copies(tab_ref[0, e], tab_ref[1, e], tab_ref[2, e], SEG_SIZES,
                            lambda s, d, size: _copy(buf.at[pl.ds(s, size)], xs_hbm.at[pl.ds(d, size)], sem), wait)
            return carry

        lax.fori_loop(0, N_EXP, per_expert, 0)

    push(False)
    push(True)

    @pl.when(b == 0)
    def _():
        zero_copies(True)


def _sort_push(tab, ztab, n_valid, metat, h):
    return pl.pallas_call(
        _sort_push_kernel,
        grid=(NB,),
        in_specs=[pl.BlockSpec((None, 3, N_EXP), lambda b: (b, 0, 0), memory_space=pltpu.SMEM),
                  pl.BlockSpec(memory_space=pltpu.SMEM),
                  pl.BlockSpec(memory_space=pltpu.SMEM),
                  pl.BlockSpec((LANES, TB), lambda b: (0, b)),
                  pl.BlockSpec((TB, D), lambda b: (b, 0))],
        out_specs=pl.BlockSpec(memory_space=pl.ANY),
        out_shape=jax.ShapeDtypeStruct((P_ROWS, D), F32),
        scratch_shapes=[pltpu.VMEM((RB, D), F32), pltpu.VMEM((TM_E, D), F32),
                        pltpu.SemaphoreType.DMA(()), pltpu.SemaphoreType.DMA(())],
        compiler_params=_cparams(("arbitrary",)),
        name="moe_sort_push",
    )(tab, ztab, n_valid, metat, h)


def _expert_kernel(te_ref, tr_ref, nv_ref, x_ref, w1_ref, b1_ref, w2_ref, b2_ref, y_ref, w1_s, w2_s):
    j = pl.program_id(0)
    valid = j < nv_ref[0]
    fresh = jnp.logical_or(j == 0, te_ref[j] != te_ref[jnp.maximum(j - 1, 0)])

    @pl.when(jnp.logical_and(valid, fresh))
    def _():
        w1_s[...] = w1_ref[...].astype(BF16)
        w2_s[...] = w2_ref[...].astype(BF16)

    @pl.when(valid)
    def _():
        rows = lax.broadcasted_iota(I32, (TM_E, 1), 0)
        x = jnp.where(rows < tr_ref[j], x_ref[...], 0.0).astype(BF16)
        gu = jnp.dot(x, w1_s[...], preferred_element_type=F32) + b1_ref[...]
        gate = jnp.minimum(gu[:, :FF], SWIGLU_LIMIT)
        up = jnp.clip(gu[:, FF:], -SWIGLU_LIMIT, SWIGLU_LIMIT)
        act = (up + 1.0) * gate * jax.nn.sigmoid(SWIGLU_ALPHA * gate)
        y_ref[...] = jnp.dot(act.astype(BF16), w2_s[...], preferred_element_type=F32) + b2_ref[...]

    @pl.when(jnp.logical_not(valid))
    def _():
        y_ref[...] = jnp.zeros_like(y_ref)


def _experts(layer, tile_exp, tile_rows, n_valid, xs, w1, b1, w2, b2):
    def wmap(j, te, tr, nv):
        return (layer, te[j], 0, 0)

    return pl.pallas_call(
        _expert_kernel,
        grid_spec=pltpu.PrefetchScalarGridSpec(
            num_scalar_prefetch=3,
            grid=(N_ETILES,),
            in_specs=[pl.BlockSpec((TM_E, D), lambda j, te, tr, nv: (j, 0)),
                      pl.BlockSpec((None, None, D, 2 * FF), wmap),
                      pl.BlockSpec((None, None, 1, 2 * FF), wmap),
                      pl.BlockSpec((None, None, FF, D), wmap),
                      pl.BlockSpec((None, None, 1, D), wmap)],
            out_specs=pl.BlockSpec((TM_E, D), lambda j, te, tr, nv: (j, 0)),
            scratch_shapes=[pltpu.VMEM((D, 2 * FF), BF16), pltpu.VMEM((FF, D), BF16)]),
        out_shape=jax.ShapeDtypeStruct((P_ROWS, D), F32),
        compiler_params=_cparams(("arbitrary",)),
        name="moe_experts",
    )(tile_exp, tile_rows, n_valid, xs, w1, b1.reshape(DEPTH, N_EXP, 1, 2 * FF), w2,
      b2.reshape(DEPTH, N_EXP, 1, D))


def _combine_kernel(tab_ref, y_hbm, x_ref, meta_ref, mod_ref, out_ref, buf, sem):
    b = pl.program_id(0)

    def pull(wait):
        def per_expert(e, carry):
            _segment_copies(tab_ref[0, e], tab_ref[2, e], tab_ref[1, e], SEG_SIZES,
                            lambda s, d, size: _copy(y_hbm.at[pl.ds(s, size)], buf.at[pl.ds(d, size)], sem), wait)
            return carry

        lax.fori_loop(0, N_EXP, per_expert, 0)

    pull(False)
    pull(True)

    used = tab_ref[1, N_EXP - 1] + tab_ref[0, N_EXP - 1]
    meta = meta_ref[...]
    ch = 256
    acc = jnp.zeros((TB, D), F32)
    for c in range(RB // ch):
        cols = (lax.broadcasted_iota(I32, (TB, ch), 1) + c * ch).astype(F32)
        g = jnp.zeros((TB, ch), F32)
        for k in range(TOP_K):
            g = g + jnp.where(cols == meta[:, 2 * TOP_K + k:2 * TOP_K + k + 1],
                              meta[:, TOP_K + k:TOP_K + k + 1], 0.0)
        rows = lax.broadcasted_iota(I32, (ch, 1), 0) + c * ch
        y = jnp.where(rows < used, buf[c * ch:(c + 1) * ch, :], 0.0)
        acc = acc + _bdot(g, y)
    out_ref[...] = x_ref[...] + _mod_vec(mod_ref, _mod_row(b, TB), 5) * acc


def _combine(tab, y, x, meta, mod_l):
    return pl.pallas_call(
        _combine_kernel,
        grid=(NB,),
        in_specs=[pl.BlockSpec((None, 3, N_EXP), lambda b: (b, 0, 0), memory_space=pltpu.SMEM),
                  pl.BlockSpec(memory_space=pl.ANY),
                  pl.BlockSpec((TB, D), lambda b: (b, 0)),
                  pl.BlockSpec((TB, LANES), lambda b: (b, 0)),
                  pl.BlockSpec((16, N_MOD * D), lambda b: (0, 0))],
        out_specs=pl.BlockSpec((TB, D), lambda b: (b, 0)),
        out_shape=jax.ShapeDtypeStruct((T, D), F32),
        scratch_shapes=[pltpu.VMEM((RB, D), F32), pltpu.SemaphoreType.DMA(())],
        compiler_params=_cparams(("arbitrary",)),
        name="moe_combine",
    )(tab, y, x, meta, mod_l)


def _moe_layer(layer, x, mod_l, nw, rw, rb, w1, b1, w2, b2):
    rw_p = jnp.zeros((D, LANES), BF16).at[:, :N_EXP].set(rw.astype(BF16))
    rb_p = jnp.full((1, LANES), -jnp.inf, F32).at[0, :N_EXP].set(rb)
    h, meta, metat, cnt = _route(x, mod_l, nw, rw_p, rb_p)

    counts = cnt[:, 0, :N_EXP].astype(I32)
    n_seg = (counts + SEG - 1) // SEG * SEG
    seg_start = jnp.cumsum(n_seg, axis=1) - n_seg
    tot = jnp.sum(n_seg, axis=0)
    padded = (tot + TM_E - 1) // TM_E * TM_E
    ends = jnp.cumsum(padded)
    offs = ends - padded
    glob = offs[None, :] + jnp.cumsum(n_seg, axis=0) - n_seg
    tab = jnp.stack([n_seg, seg_start, glob], axis=1)
    ztab = jnp.stack([offs + tot, padded - tot], axis=0)
    n_valid = ends[-1] // TM_E
    starts = jnp.arange(N_ETILES, dtype=I32) * TM_E
    te = jnp.minimum(jnp.sum((ends[None, :] <= starts[:, None]).astype(I32), axis=1), N_EXP - 1)
    last = jnp.sum((ends[:-1] < ends[-1]).astype(I32))
    te = jnp.where(starts < ends[-1], te, last)
    tr = jnp.clip(offs[te] + tot[te] - starts, 0, TM_E)
    nv = n_valid.reshape(1)

    xs = _sort_push(tab, ztab, nv, metat, h)
    y = _experts(layer, te, tr, nv, xs, w1, b1, w2, b2)
    return _combine(tab, y, x, meta, mod_l)


def _head_sel():
    g = np.zeros((D, LANES), np.float32)
    for c in range(D):
        g[c, c // HEAD_DIM] = 1.0
    return jnp.asarray(g, BF16)


def _head_rms(x, sel, sel_t, w):
    n = x.shape[1]
    hi, lo = _split_hi_lo(x * x)
    ss = (jnp.dot(hi, sel[:n], preferred_element_type=F32)
          + jnp.dot(lo, sel[:n], preferred_element_type=F32))
    inv = lax.rsqrt(ss * (1.0 / HEAD_DIM) + EPS)
    hi, lo = _split_hi_lo(inv)
    inv_e = (jnp.dot(hi, sel_t[:, :n], preferred_element_type=F32)
             + jnp.dot(lo, sel_t[:, :n], preferred_element_type=F32))
    return x * inv_e * w


def _rope(x, cos, sin):
    n = x.shape[1]
    lane = lax.broadcasted_iota(I32, x.shape, 1)
    first = (lane % (HEAD_DIM // 2)) < (HEAD_DIM // 4)
    rot = jnp.where(first, -pltpu.roll(x, n - HEAD_DIM // 4, axis=1), pltpu.roll(x, HEAD_DIM // 4, axis=1))
    return x * cos + rot * sin


def _qkv_kernel(x_ref, mod_ref, nw_ref, w_ref, sel_ref, selt_ref, qw_ref, kw_ref, cos_ref, sin_ref,
                q_ref, k_ref, v_ref, kf_ref, vf_ref, *, tm):
    i = pl.program_id(0)
    r = _mod_row(i, tm)
    h = _rms(x_ref[...], nw_ref[...]) * (1.0 + _mod_vec(mod_ref, r, 1)) + _mod_vec(mod_ref, r, 0)
    qkv = _bdot(h, w_ref[...])
    nq, nk = ATT_HEADS * HEAD_DIM, KV_HEADS * HEAD_DIM
    q = _head_rms(qkv[:, :nq], sel_ref[...], selt_ref[...], qw_ref[...])
    k = _head_rms(qkv[:, nq:nq + nk], sel_ref[...], selt_ref[...], kw_ref[...])
    v = qkv[:, nq + nk:]
    kf_ref[...] = k
    vf_ref[...] = v
    latent = i * tm >= NP_TOK
    cos = jnp.where(latent, cos_ref[...], 1.0)
    sin = jnp.where(latent, sin_ref[...], 0.0)
    q = _rope(q, cos, sin) * (HEAD_DIM ** -0.5)
    k = _rope(k, cos[:, :nk], sin[:, :nk])
    for hd in range(ATT_HEADS):
        q_ref[hd] = q[:, hd * HEAD_DIM:(hd + 1) * HEAD_DIM].astype(BF16)
    for hd in range(KV_HEADS):
        k_ref[hd] = k[:, hd * HEAD_DIM:(hd + 1) * HEAD_DIM].astype(BF16)
        v_ref[hd] = v[:, hd * HEAD_DIM:(hd + 1) * HEAD_DIM].astype(BF16)


def _qkv(x, mod_l, nw, w, qw, kw, cos, sin, *, tm):
    nk = KV_HEADS * HEAD_DIM
    sel = _head_sel()
    nblk = DEC_SEQ // tm

    def cmap(i):
        return (jnp.maximum(i - NP_TOK // tm, 0) % nblk, 0)

    return pl.pallas_call(
        functools.partial(_qkv_kernel, tm=tm),
        grid=(T // tm,),
        in_specs=[pl.BlockSpec((tm, D), lambda i: (i, 0)),
                  pl.BlockSpec((16, N_MOD * D), lambda i: (0, 0)),
                  pl.BlockSpec((1, D), lambda i: (0, 0)),
                  pl.BlockSpec((D, D + 2 * nk), lambda i: (0, 0)),
                  pl.BlockSpec((D, LANES), lambda i: (0, 0)),
                  pl.BlockSpec((LANES, D), lambda i: (0, 0)),
                  pl.BlockSpec((1, D), lambda i: (0, 0)),
                  pl.BlockSpec((1, nk), lambda i: (0, 0)),
                  pl.BlockSpec((tm, D), cmap),
                  pl.BlockSpec((tm, D), cmap)],
        out_specs=[pl.BlockSpec((ATT_HEADS, tm, HEAD_DIM), lambda i: (0, i, 0)),
                   pl.BlockSpec((KV_HEADS, tm, HEAD_DIM), lambda i: (0, i, 0)),
                   pl.BlockSpec((KV_HEADS, tm, HEAD_DIM), lambda i: (0, i, 0)),
                   pl.BlockSpec((tm, nk), lambda i: (i, 0)),
                   pl.BlockSpec((tm, nk), lambda i: (i, 0))],
        out_shape=[jax.ShapeDtypeStruct((ATT_HEADS, T, HEAD_DIM), BF16),
                   jax.ShapeDtypeStruct((KV_HEADS, T, HEAD_DIM), BF16),
                   jax.ShapeDtypeStruct((KV_HEADS, T, HEAD_DIM), BF16),
                   jax.ShapeDtypeStruct((T, nk), F32),
                   jax.ShapeDtypeStruct((T, nk), F32)],
        compiler_params=_cparams(("arbitrary",)),
        name="attn_qkv",
    )(x, mod_l, nw, w, sel, sel.T, qw, kw, cos, sin)


def _attn_kernel(*refs, has_cache):
    if has_cache:
        q_ref, k_ref, v_ref, kc_ref, vc_ref, o_ref = refs
    else:
        q_ref, k_ref, v_ref, o_ref = refs
    k = k_ref[0]
    v = v_ref[0]
    outs = []
    for j in range(Q_PER_KV):
        q = q_ref[j]
        s = lax.dot_general(q, k, (((1,), (1,)), ((), ())), preferred_element_type=F32)
        m = jnp.max(s, axis=-1, keepdims=True)
        if has_cache:
            sc = lax.dot_general(q, kc_ref[0, 0], (((1,), (1,)), ((), ())), preferred_element_type=F32)
            m = jnp.maximum(m, jnp.max(sc, axis=-1, keepdims=True))
            pc = jnp.exp(sc - m)
        p = jnp.exp(s - m)
        den = jnp.sum(p, axis=-1, keepdims=True)
        o = jnp.dot(p.astype(BF16), v, preferred_element_type=F32)
        if has_cache:
            den = den + jnp.sum(pc, axis=-1, keepdims=True)
            o = o + jnp.dot(pc.astype(BF16), vc_ref[0, 0], preferred_element_type=F32)
        outs.append(o / den)
    o_ref[...] = jnp.concatenate(outs, axis=1).astype(BF16)


def _attn_call(q, k, v, *, L, n_seq, row0, tq, cache):
    rb_q = row0 // tq
    rb_k = row0 // L
    nq = L // tq
    has_cache = cache is not None
    in_specs = [pl.BlockSpec((Q_PER_KV, tq, HEAD_DIM), lambda b, g, t: (g, rb_q + b * nq + t, 0)),
                pl.BlockSpec((1, L, HEAD_DIM), lambda b, g, t: (g, rb_k + b, 0)),
                pl.BlockSpec((1, L, HEAD_DIM), lambda b, g, t: (g, rb_k + b, 0))]
    args = [q, k, v]
    if has_cache:
        in_specs += [pl.BlockSpec((1, 1, PAST, HEAD_DIM), lambda b, g, t: (b, g, 0, 0))] * 2
        args += list(cache)
    gw = Q_PER_KV * HEAD_DIM
    return pl.pallas_call(
        functools.partial(_attn_kernel, has_cache=has_cache),
        grid=(n_seq, KV_HEADS, nq),
        in_specs=in_specs,
        out_specs=pl.BlockSpec((tq, gw), lambda b, g, t: (b * nq + t, g)),
        out_shape=jax.ShapeDtypeStruct((n_seq * L, D), BF16),
        compiler_params=_cparams(("arbitrary", "arbitrary", "arbitrary")),
        name=f"attn_L{L}",
    )(*args)


def _rope_tables():
    rows = DEC_SEQ // GRID_W
    row = jnp.repeat(jnp.arange(rows), GRID_W)
    colp = jnp.tile(jnp.arange(GRID_W), rows)
    pos = jnp.stack([row, colp], axis=-1).astype(F32)
    half = HEAD_DIM // 2
    inv_freq = ROPE_THETA ** (-jnp.arange(0, half, 2, dtype=F32) / half)
    ang = pos[:, :, None] * inv_freq
    ang = jnp.concatenate([ang, ang], axis=-1).reshape(DEC_SEQ, HEAD_DIM)
    ang = jnp.tile(ang, (1, ATT_HEADS))
    return jnp.cos(ang), jnp.sin(ang)


def _final_kernel(x_ref, w_ref, o_ref):
    o_ref[...] = _rms(x_ref[...], w_ref[...])


def _final_norm(x, w, *, row0, n_rows, tm):
    rb = row0 // tm
    return pl.pallas_call(
        _final_kernel,
        grid=(n_rows // tm,),
        in_specs=[pl.BlockSpec((tm, D), lambda i: (rb + i, 0)),
                  pl.BlockSpec((1, D), lambda i: (0, 0))],
        out_specs=pl.BlockSpec((tm, D), lambda i: (i, 0)),
        out_shape=jax.ShapeDtypeStruct((n_rows, D), F32),
        compiler_params=_cparams(("arbitrary",)),
        name="final_norm",
    )(x, w)


def _ssd_params(conv_w, conv_b, dt_bias, a_log, d_skip):
    def dt_layout(p):
        g = p.reshape(2, SSD_G, HPG).transpose(1, 0, 2).reshape(SSD_G, 2 * HPG)
        return jnp.zeros((SSD_G, LANES), F32).at[:, :2 * HPG].set(g).reshape(1, SSD_G * LANES)

    ni = SSD_INNER
    nb = SSD_G * SSD_N
    return dict(
        cwx=conv_w[:, :ni], cwb=conv_w[:, ni:ni + nb], cwc=conv_w[:, ni + nb:],
        cbx=conv_b[None, :ni], cbb=conv_b[None, ni:ni + nb], cbc=conv_b[None, ni + nb:],
        dtb=dt_layout(dt_bias), alog=dt_layout(a_log),
        dcol=jnp.repeat(d_skip, SSD_P)[None, :], rsel=_ssd_rsel())


def _ssm_in_weights(w):
    ni, cd = SSD_INNER, SSD_INNER + 2 * SSD_G * SSD_N
    o_xbc, o_dt = ni, ni + cd
    o_hq = o_dt + 2 * SSD_HEADS
    o_hf, o_hi = o_hq + HG_QK, o_hq + 3 * HG_QK
    o_hg = o_hi + HG_IV
    main = jnp.concatenate([w[:, :ni], w[:, o_hq:o_hf], w[:, o_hi:o_hg], w[:, o_hg:o_hg + HG_IV],
                            w[:, o_hf:o_hi], w[:, o_xbc:o_dt]], axis=1).astype(BF16)
    wdt = w[:, o_dt:o_hq].reshape(D, 2, SSD_G, HPG).transpose(0, 2, 1, 3).reshape(D, SSD_G, 2 * HPG)
    wdt = jnp.zeros((D, SSD_G, LANES), F32).at[:, :, :2 * HPG].set(wdt).reshape(D, SSD_G * LANES)
    return main, wdt.astype(BF16)


def kernel(x_prompt, x_sample, c, c_ctx, state_ssd, state_hgrn, cache_k, cache_v, mod_w, mod_b, norm1_w,
           norm2_w, ssm_in_w, ssd_conv_w, ssd_conv_b, ssd_dt_bias, ssd_a_log, ssd_d, ssd_norm_w, hg_lb,
           hg_norm_w, ssm_out_w, attn_in_w, q_norm_w, k_norm_w, attn_out_w, router_w, router_b, exp_w1,
           exp_b1, exp_w2, exp_b2, final_norm_w):
    xp = x_prompt.reshape(NP_TOK, D)
    xs = x_sample.reshape(NS_TOK, D)
    cvec = jnp.zeros((16, D), F32).at[0].set(c_ctx).at[1:1 + DEC_BATCH].set(c)
    mod = _mod_table(cvec, mod_w, mod_b)

    w_main, w_dt = _ssm_in_weights(ssm_in_w[0])
    proj, dtp = _inproj(xp, xs, mod[0], norm1_w[0][None], w_main, w_dt, tm=1024, tn=1280)
    prm = _ssd_params(ssd_conv_w[0], ssd_conv_b[0], ssd_dt_bias[0], ssd_a_log[0], ssd_d[0])
    yp, st_ssd = _ssd_call(proj, dtp, prm, L=SEQ, n_seq=BATCH, row0=0, s0=None, want_state=True)
    s0 = state_ssd[:, 0].reshape(DEC_BATCH, 2, SSD_G, GW, SSD_N)
    (ys,) = _ssd_call(proj, dtp, prm, L=DEC_SEQ, n_seq=DEC_BATCH, row0=NP_TOK, s0=s0, want_state=False)
    hg_nw = hg_norm_w[0][None]
    op, st_hg = _gla_call(proj, hg_lb, hg_nw, layer=0, L=SEQ, n_seq=BATCH, row0=0, s0=None, want_state=True)
    (os_,) = _gla_call(proj, hg_lb, hg_nw, layer=0, L=DEC_SEQ, n_seq=DEC_BATCH, row0=NP_TOK,
                       s0=state_hgrn[:, 0], want_state=False)
    x = _ssm_out(xp, xs, yp, ys, op, os_, proj, mod[0], ssd_norm_w[0][None], ssm_out_w[0].astype(BF16), tm=512)
    x = _moe_layer(0, x, mod[0], norm2_w[0][None], router_w[0], router_b[0], exp_w1, exp_b1, exp_w2, exp_b2)

    cos, sin = _rope_tables()
    q, k, v, kf, vf = _qkv(x, mod[1], norm1_w[1][None], attn_in_w[0].astype(BF16),
                           jnp.tile(q_norm_w[0], ATT_HEADS)[None], jnp.tile(k_norm_w[0], KV_HEADS)[None],
                           cos, sin, tm=256)
    ap = _attn_call(q, k, v, L=SEQ, n_seq=BATCH, row0=0, tq=SEQ, cache=None)
    ck = cache_k[:, 0].transpose(0, 2, 1, 3).astype(BF16)
    cv = cache_v[:, 0].transpose(0, 2, 1, 3).astype(BF16)
    as_ = _attn_call(q, k, v, L=DEC_SEQ, n_seq=DEC_BATCH, row0=NP_TOK, tq=256, cache=(ck, cv))
    x = _attn_out(x, ap, as_, mod[1], attn_out_w[0].astype(BF16), tm=512)
    x = _moe_layer(1, x, mod[1], norm2_w[1][None], router_w[1], router_b[1], exp_w1, exp_b1, exp_w2, exp_b2)

    y_prompt = _final_norm(x, final_norm_w[None], row0=0, n_rows=NP_TOK, tm=512).reshape(BATCH, SEQ, D)
    y_sample = _final_norm(x, final_norm_w[None], row0=NP_TOK, n_rows=NS_TOK, tm=512).reshape(
        DEC_BATCH, DEC_SEQ, D)
    new_ssd = st_ssd.reshape(BATCH, 1, 2, SSD_HEADS, SSD_P, SSD_N)
    new_hg = st_hg.reshape(BATCH, 1, 2, HG_HEADS, HG_K, HG_V)
    new_k = kf[:NP_TOK].reshape(BATCH, 1, SEQ, KV_HEADS, HEAD_DIM)
    new_v = vf[:NP_TOK].reshape(BATCH, 1, SEQ, KV_HEADS, HEAD_DIM)
    return (y_prompt, y_sample, new_ssd, new_hg, new_k, new_v)
```

```python
import functools

import jax
import jax.numpy as jnp
import numpy as np
from jax import lax
from jax.experimental import pallas as pl
from jax.experimental.pallas import tpu as pltpu

F32 = jnp.float32
BF16 = jnp.bfloat16
I32 = jnp.int32

D = 1024
BATCH, SEQ = 32, 256
DEC_BATCH, DEC_SEQ = 8, 1024
PAST = 256
NP_TOK = BATCH * SEQ
NS_TOK = DEC_BATCH * DEC_SEQ
T = NP_TOK + NS_TOK
DEPTH = 2
N_MOD = 6
EPS = 1e-6
GRID_W = 64
ROPE_THETA = 10000.0

SSD_HEADS, SSD_P, SSD_N, SSD_G = 16, 64, 128, 2
SSD_INNER = SSD_HEADS * SSD_P
SSD_CONV = 5
HPG = SSD_HEADS // SSD_G
GW = HPG * SSD_P
HG_HEADS, HG_K, HG_V = 8, 128, 128
HG_QK = HG_HEADS * HG_K
HG_IV = HG_HEADS * HG_V
ATT_HEADS, KV_HEADS, HEAD_DIM = 16, 4, 64
Q_PER_KV = ATT_HEADS // KV_HEADS
N_EXP, TOP_K, FF = 32, 4, 1024
SWIGLU_ALPHA, SWIGLU_LIMIT = 1.702, 7.0

LANES = 128
SUBLANES = 8
VMEM_LIMIT = 56 * 1024 * 1024

COL_Z, COL_HQ, COL_HI, COL_HG, COL_HF, COL_XBC = 0, 1024, 2048, 3072, 4096, 6144
PROJ_W = 7680

SSD_TC = 256
GLA_TC = 64
GLA_GROUP = 256

TB = 512
NB = T // TB
SEG = SUBLANES
RB = TB * TOP_K + N_EXP * SEG
TM_E = 512
SEG_SIZES = tuple(SEG << i for i in range((TB // SEG).bit_length() - 1, -1, -1))
PAD_SIZES = tuple(s for s in SEG_SIZES if s < TM_E)
P_ROWS_MAX = T * TOP_K + NB * N_EXP * (SEG - 1) + N_EXP * (TM_E - SEG)
N_ETILES = -(-P_ROWS_MAX // TM_E)
P_ROWS = N_ETILES * TM_E


def _cparams(sem):
    return pltpu.CompilerParams(dimension_semantics=sem, vmem_limit_bytes=VMEM_LIMIT)


def _silu(x):
    return x * jax.nn.sigmoid(x)


def _bdot(a, b):
    return jnp.dot(a.astype(BF16), b.astype(BF16), preferred_element_type=F32)


def _bdot_nt(a, b):
    return lax.dot_general(a.astype(BF16), b.astype(BF16), (((1,), (1,)), ((), ())),
                           preferred_element_type=F32)


def _xdot(a, b):
    return jnp.dot(a, b, preferred_element_type=F32, precision=lax.Precision.HIGHEST)


def _mod_row(i, tm):
    start = i * tm
    return jnp.where(start < NP_TOK, 0, 1 + (start - NP_TOK) // DEC_SEQ)


def _mod_vec(mod_ref, r, k):
    return mod_ref[pl.ds(r, 1), k * D:(k + 1) * D]


def _rms(x, w):
    return x * lax.rsqrt(jnp.mean(x * x, axis=-1, keepdims=True) + EPS) * w


def _two_src_specs(tm, width=D):
    npt = NP_TOK // tm
    return [pl.BlockSpec((tm, width), lambda i, *_: (jnp.minimum(i, npt - 1), 0)),
            pl.BlockSpec((tm, width), lambda i, *_: (jnp.maximum(i - npt, 0), 0))]


def _pick(i, tm, p_ref, s_ref):
    return jnp.where(i * tm < NP_TOK, p_ref[...], s_ref[...])


def _mod_kernel(c_ref, w_ref, b_ref, o_ref):
    o_ref[0] = _bdot(_silu(c_ref[...]), w_ref[0]) + b_ref[0]


def _mod_table(cvec, mod_w, mod_b):
    tn = 1536
    return pl.pallas_call(
        _mod_kernel,
        grid=(DEPTH, N_MOD * D // tn),
        in_specs=[pl.BlockSpec((16, D), lambda l, n: (0, 0)),
                  pl.BlockSpec((1, D, tn), lambda l, n: (l, 0, n)),
                  pl.BlockSpec((1, 1, tn), lambda l, n: (l, 0, n))],
        out_specs=pl.BlockSpec((1, 16, tn), lambda l, n: (l, 0, n)),
        out_shape=jax.ShapeDtypeStruct((DEPTH, 16, N_MOD * D), F32),
        compiler_params=_cparams(("arbitrary", "arbitrary")),
        name="mod_table",
    )(cvec, mod_w, mod_b.reshape(DEPTH, 1, N_MOD * D))


def _inproj_kernel(xp_ref, xs_ref, mod_ref, nw_ref, w_ref, wx_ref, o_ref, ox_ref, h_ref, *, tm):
    i = pl.program_id(0)

    @pl.when(pl.program_id(1) == 0)
    def _():
        x = _pick(i, tm, xp_ref, xs_ref)
        r = _mod_row(i, tm)
        h = _rms(x, nw_ref[...]) * (1.0 + _mod_vec(mod_ref, r, 1)) + _mod_vec(mod_ref, r, 0)
        hb = h.astype(BF16)
        h_ref[...] = hb
        ox_ref[...] = jnp.dot(hb, wx_ref[...], preferred_element_type=F32)

    o_ref[...] = jnp.dot(h_ref[...], w_ref[...], preferred_element_type=F32)


def _inproj(xp, xs, mod_l, nw, w, wx, *, tm, tn):
    n_out, nx = w.shape[1], wx.shape[1]
    return pl.pallas_call(
        functools.partial(_inproj_kernel, tm=tm),
        grid=(T // tm, n_out // tn),
        in_specs=_two_src_specs(tm) + [
            pl.BlockSpec((16, N_MOD * D), lambda i, n: (0, 0)),
            pl.BlockSpec((1, D), lambda i, n: (0, 0)),
            pl.BlockSpec((D, tn), lambda i, n: (0, n)),
            pl.BlockSpec((D, nx), lambda i, n: (0, 0))],
        out_specs=[pl.BlockSpec((tm, tn), lambda i, n: (i, n)),
                   pl.BlockSpec((tm, nx), lambda i, n: (i, 0))],
        out_shape=[jax.ShapeDtypeStruct((T, n_out), F32), jax.ShapeDtypeStruct((T, nx), F32)],
        scratch_shapes=[pltpu.VMEM((tm, D), BF16)],
        compiler_params=_cparams(("arbitrary", "arbitrary")),
        name="inproj",
    )(xp, xs, mod_l, nw, w, wx)


def _tri(n, upper):
    r = lax.broadcasted_iota(I32, (n, n), 0)
    c = lax.broadcasted_iota(I32, (n, n), 1)
    return (c >= r) if upper else (c <= r)


def _conv_silu(x, w, b, L):
    row = lax.broadcasted_iota(I32, (L, 1), 0)
    acc = x * w[2:3] + b
    for k in (0, 1, 3, 4):
        off = k - 2
        shifted = pltpu.roll(x, (-off) % L, axis=0)
        ok = (row + off >= 0) & (row + off < L)
        acc = acc + jnp.where(ok, shifted, 0.0) * w[k:k + 1]
    return _silu(acc)


def _split_hi_lo(x):
    hi = x.astype(BF16)
    return hi, (x - hi.astype(F32)).astype(BF16)


def _ssd_kernel(*refs, L, has_init, want_state):
    (x_ref, b_ref, c_ref, dt_ref, cwx_ref, cwb_ref, cwc_ref, cbx_ref, cbb_ref, cbc_ref,
     dtb_ref, alog_ref, dcol_ref, rsel_ref) = refs[:14]
    pos = 14
    s0_ref = None
    if has_init:
        s0_ref = refs[pos]
        pos += 1
    y_ref = refs[pos]
    pos += 1
    st_ref = None
    if want_state:
        st_ref = refs[pos]
        pos += 1
    xs_s, bs_s, cs_s, y_s = refs[pos:pos + 4]

    tc = SSD_TC
    nc = L // tc
    xs_s[...] = _conv_silu(x_ref[...], cwx_ref[...], cbx_ref[...], L)
    bs_s[...] = _conv_silu(b_ref[...], cwb_ref[...], cbb_ref[...], L)
    cs_s[...] = _conv_silu(c_ref[...], cwc_ref[...], cbc_ref[...], L)
    y_s[...] = xs_s[...] * dcol_ref[...]

    z = dt_ref[...] + dtb_ref[...]
    dt_all = jnp.maximum(z, 0.0) + jnp.log(1.0 + jnp.exp(-jnp.abs(z)))
    a_row = -jnp.exp(alog_ref[...])

    for d in (0, 1):
        upper = d == 1
        tri = _tri(tc, upper)
        tri_f = tri.astype(F32)
        rsel = rsel_ref[d]
        st = s0_ref[d].T if has_init else None
        order = range(nc) if d == 0 else range(nc - 1, -1, -1)
        for c in order:
            sl = slice(c * tc, (c + 1) * tc)
            x = xs_s[sl]
            bm = bs_s[sl]
            cm = cs_s[sl]
            dt = dt_all[sl]
            cum = _xdot(tri_f, dt * a_row)
            cum_t = cum.T
            end = cum[0:1] if upper else cum[tc - 1:tc]
            stack = jnp.concatenate(
                [dt, jnp.exp(cum), jnp.exp(end - cum), jnp.broadcast_to(jnp.exp(end), (SUBLANES, LANES))],
                axis=0)
            hi, lo = _split_hi_lo(stack)
            ex = (jnp.dot(hi, rsel, preferred_element_type=F32)
                  + jnp.dot(lo, rsel, preferred_element_type=F32))
            dt_e, cum_e, dec_e = ex[0:tc], ex[tc:2 * tc], ex[2 * tc:3 * tc]
            end_e = ex[3 * tc:3 * tc + 1]
            xdt = x * dt_e
            cb = _bdot_nt(cm, bm)
            ys = []
            for h in range(HPG):
                j = d * HPG + h
                seg = cum[:, j:j + 1] - cum_t[j:j + 1, :]
                lm = jnp.exp(jnp.where(tri, seg, -jnp.inf))
                ys.append(_bdot(cb * lm, xdt[:, h * SSD_P:(h + 1) * SSD_P]))
            y = jnp.concatenate(ys, axis=1)
            if st is not None:
                y = y + _bdot(cm, st) * cum_e
            y_s[sl] = y_s[sl] + y
            sc = _bdot(bm.T, xdt * dec_e)
            st = sc if st is None else st * end_e + sc
        if want_state:
            st_ref[d] = st.T

    y_ref[...] = y_s[...]


def _ssd_rsel():
    r = np.zeros((2, LANES, GW), np.float32)
    for d in range(2):
        for h in range(HPG):
            r[d, d * HPG + h, h * SSD_P:(h + 1) * SSD_P] = 1.0
    return jnp.asarray(r, BF16)


def _ssd_call(proj, dtp, prm, *, L, n_seq, row0, s0, want_state):
    rb = row0 // L
    has_init = s0 is not None
    xbc = COL_XBC
    in_specs = [
        pl.BlockSpec((L, GW), lambda b, g: (rb + b, xbc // GW + g)),
        pl.BlockSpec((L, SSD_N), lambda b, g: (rb + b, (xbc + SSD_INNER) // SSD_N + g)),
        pl.BlockSpec((L, SSD_N), lambda b, g: (rb + b, (xbc + SSD_INNER + SSD_G * SSD_N) // SSD_N + g)),
        pl.BlockSpec((L, LANES), lambda b, g: (rb + b, g)),
        pl.BlockSpec((SSD_CONV, GW), lambda b, g: (0, g)),
        pl.BlockSpec((SSD_CONV, SSD_N), lambda b, g: (0, g)),
        pl.BlockSpec((SSD_CONV, SSD_N), lambda b, g: (0, g)),
        pl.BlockSpec((1, GW), lambda b, g: (0, g)),
        pl.BlockSpec((1, SSD_N), lambda b, g: (0, g)),
        pl.BlockSpec((1, SSD_N), lambda b, g: (0, g)),
        pl.BlockSpec((1, LANES), lambda b, g: (0, g)),
        pl.BlockSpec((1, LANES), lambda b, g: (0, g)),
        pl.BlockSpec((1, GW), lambda b, g: (0, g)),
        pl.BlockSpec((2, LANES, GW), lambda b, g: (0, 0, 0)),
    ]
    args = [proj, proj, proj, dtp, prm["cwx"], prm["cwb"], prm["cwc"], prm["cbx"], prm["cbb"], prm["cbc"],
            prm["dtb"], prm["alog"], prm["dcol"], prm["rsel"]]
    if has_init:
        in_specs.append(pl.BlockSpec((None, 2, None, GW, SSD_N), lambda b, g: (b, 0, g, 0, 0)))
        args.append(s0)
    out_specs = [pl.BlockSpec((L, GW), lambda b, g: (b, g))]
    out_shape = [jax.ShapeDtypeStruct((n_seq * L, SSD_INNER), F32)]
    if want_state:
        out_specs.append(pl.BlockSpec((None, 2, None, GW, SSD_N), lambda b, g: (b, 0, g, 0, 0)))
        out_shape.append(jax.ShapeDtypeStruct((n_seq, 2, SSD_G, GW, SSD_N), F32))
    return pl.pallas_call(
        functools.partial(_ssd_kernel, L=L, has_init=has_init, want_state=want_state),
        grid=(n_seq, SSD_G),
        in_specs=in_specs, out_specs=out_specs, out_shape=out_shape,
        scratch_shapes=[pltpu.VMEM((L, GW), F32), pltpu.VMEM((L, SSD_N), F32),
                        pltpu.VMEM((L, SSD_N), F32), pltpu.VMEM((L, GW), F32)],
        compiler_params=_cparams(("arbitrary", "arbitrary")),
        name=f"ssd_L{L}",
    )(*args)


def _gla_kernel(*refs, L, layer, has_init, want_state, hpb):
    q_ref, ff_ref, fb_ref, v_ref, g_ref, lb_ref, nw_ref = refs[:7]
    pos = 7
    s0_ref = None
    if has_init:
        s0_ref = refs[pos]
        pos += 1
    o_ref = refs[pos]
    pos += 1
    st_ref = None
    if want_state:
        st_ref = refs[pos]
        pos += 1
    qs, ks, ls, os_ = refs[pos:pos + 4]

    tc, gr = GLA_TC, GLA_GROUP
    ng, cpg = L // gr, gr // tc
    lbr = lb_ref[...]
    e = jnp.exp(lbr - jnp.max(lbr, axis=0, keepdims=True))
    sm = e / jnp.sum(e, axis=0, keepdims=True)
    lb = sm[0]
    for j in range(1, layer + 1):
        lb = lb + sm[j]

    qs[...] = _silu(q_ref[...])
    for d, fr in ((0, ff_ref), (1, fb_ref)):
        logit = fr[...]
        lbd = lb[d:d + 1]
        f = lbd + (1.0 - lbd) * jax.nn.sigmoid(logit)
        ks[d] = (1.0 - lbd) * jax.nn.sigmoid(-logit)
        ls[d] = jnp.log(f)

    r_i = lax.broadcasted_iota(I32, (gr, gr), 0)
    c_i = lax.broadcasted_iota(I32, (gr, gr), 1)
    sh = tc.bit_length() - 1
    same = (r_i >> sh) == (c_i >> sh)
    causal = [same & (c_i <= r_i), same & (c_i >= r_i)]
    bd = [m.astype(BF16) for m in causal]
    row_chunk = lax.broadcasted_iota(I32, (gr, 1), 0) >> sh
    col_chunk = lax.broadcasted_iota(I32, (1, gr), 1) >> sh

    def rows_of(x, r):
        return jnp.concatenate(
            [jnp.broadcast_to(x[c * tc + r:c * tc + r + 1], (tc, x.shape[1])) for c in range(cpg)], axis=0)

    def group(d, j, g0, st):
        cs = slice(j * HG_K, (j + 1) * HG_K)
        hi, lo = _split_hi_lo(ls[d, pl.ds(g0, gr), cs])
        cum = jnp.dot(bd[d], hi, preferred_element_type=F32) + jnp.dot(bd[d], lo, preferred_element_type=F32)
        q = qs[pl.ds(g0, gr), cs]
        k = ks[d, pl.ds(g0, gr), cs]
        v = v_ref[pl.ds(g0, gr), cs]
        upper = d == 1
        ref_b = rows_of(cum, tc // 2 if upper else tc // 2 - 1)
        end_b = rows_of(cum, 0 if upper else tc - 1)
        att = _bdot_nt(q * jnp.exp(cum - ref_b), k * jnp.exp(ref_b - cum))
        o = _bdot(jnp.where(causal[d], att, 0.0), v)
        kd_t = (k * jnp.exp(end_b - cum)).T
        inc = _bdot(jnp.concatenate([jnp.where(col_chunk == c, kd_t, 0.0) for c in range(cpg)], axis=0), v)
        dec_t = jnp.exp(end_b).T
        s_in = [None] * cpg
        for c in (range(cpg - 1, -1, -1) if upper else range(cpg)):
            s_in[c] = st
            st = st * dec_t[:, c * tc:c * tc + 1] + inc[c * HG_K:(c + 1) * HG_K]
        qe = q * jnp.exp(cum)
        o = o + _bdot(jnp.concatenate([jnp.where(row_chunk == c, qe, 0.0) for c in range(cpg)], axis=1),
                      jnp.concatenate(s_in, axis=0))
        return o, st

    def init(d, j):
        return s0_ref[d, j] if has_init else jnp.zeros((HG_K, HG_V), F32)

    heads = range(hpb)
    if ng == 1:
        outs, finals = [], []
        for j in heads:
            of, sf = group(0, j, 0, init(0, j))
            ob, sb = group(1, j, 0, init(1, j))
            outs.append(of + ob)
            finals.append((sf, sb))
    else:
        os_[...] = jnp.zeros_like(os_)

        def body(i, carry):
            gf = pl.multiple_of(i * gr, gr)
            gb = pl.multiple_of((ng - 1 - i) * gr, gr)
            new = []
            for j in heads:
                cs = slice(j * HG_V, (j + 1) * HG_V)
                of, sf = group(0, j, gf, carry[j][0])
                ob, sb = group(1, j, gb, carry[j][1])
                os_[pl.ds(gf, gr), cs] = os_[pl.ds(gf, gr), cs] + of
                os_[pl.ds(gb, gr), cs] = os_[pl.ds(gb, gr), cs] + ob
                new.append((sf, sb))
            return tuple(new)

        finals = lax.fori_loop(0, ng, body, tuple((init(0, j), init(1, j)) for j in heads))
        outs = [os_[:, j * HG_V:(j + 1) * HG_V] for j in heads]
    for j in heads:
        cs = slice(j * HG_V, (j + 1) * HG_V)
        if want_state:
            st_ref[0, j] = finals[j][0]
            st_ref[1, j] = finals[j][1]
        o_ref[:, cs] = _rms(outs[j], nw_ref[:, cs]) * _silu(g_ref[:, cs])


def _gla_call(proj, hg_lb, hg_nw, *, layer, L, n_seq, row0, s0, want_state, hpb):
    rb = row0 // L
    has_init = s0 is not None
    w = hpb * HG_K

    def col(c0):
        return lambda b, h: (rb + b, c0 // w + h)

    st_spec = pl.BlockSpec((None, 2, hpb, HG_K, HG_V), lambda b, h: (b, 0, h, 0, 0))
    in_specs = [
        pl.BlockSpec((L, w), col(COL_HQ)),
        pl.BlockSpec((L, w), col(COL_HF)),
        pl.BlockSpec((L, w), col(COL_HF + HG_QK)),
        pl.BlockSpec((L, w), col(COL_HI)),
        pl.BlockSpec((L, w), col(COL_HG)),
        pl.BlockSpec((DEPTH + 1, 2, w), lambda b, h: (0, 0, h)),
        pl.BlockSpec((1, w), lambda b, h: (0, h)),
    ]
    args = [proj, proj, proj, proj, proj, hg_lb, hg_nw]
    if has_init:
        in_specs.append(st_spec)
        args.append(s0)
    out_specs = [pl.BlockSpec((L, w), lambda b, h: (b, h))]
    out_shape = [jax.ShapeDtypeStruct((n_seq * L, HG_IV), F32)]
    if want_state:
        out_specs.append(st_spec)
        out_shape.append(jax.ShapeDtypeStruct((n_seq, 2, HG_HEADS, HG_K, HG_V), F32))
    return pl.pallas_call(
        functools.partial(_gla_kernel, L=L, layer=layer, has_init=has_init, want_state=want_state, hpb=hpb),
        grid=(n_seq, HG_HEADS // hpb),
        in_specs=in_specs, out_specs=out_specs, out_shape=out_shape,
        scratch_shapes=[pltpu.VMEM((L, w), F32), pltpu.VMEM((2, L, w), F32),
                        pltpu.VMEM((2, L, w), F32), pltpu.VMEM((L, w), F32)],
        compiler_params=_cparams(("arbitrary", "arbitrary")),
        name=f"gla_L{L}",
    )(*args)


def _ssm_out_kernel(xp_ref, xs_ref, yp_ref, ys_ref, op_ref, os_ref, z_ref, mod_ref, nw_ref, w_ref,
                    out_ref, *, tm):
    i = pl.program_id(0)
    g = _pick(i, tm, yp_ref, ys_ref) * _silu(z_ref[...])
    mix = (_bdot(_rms(g, nw_ref[...]), w_ref[0:SSD_INNER, :])
           + _bdot(_pick(i, tm, op_ref, os_ref), w_ref[SSD_INNER:, :]))
    out_ref[...] = _pick(i, tm, xp_ref, xs_ref) + _mod_vec(mod_ref, _mod_row(i, tm), 2) * mix


def _ssm_out(xp, xs, yp, ys, op, os_, proj, mod_l, nw, w, *, tm):
    return pl.pallas_call(
        functools.partial(_ssm_out_kernel, tm=tm),
        grid=(T // tm,),
        in_specs=_two_src_specs(tm) + _two_src_specs(tm, SSD_INNER) + _two_src_specs(tm, HG_IV) + [
            pl.BlockSpec((tm, SSD_INNER), lambda i: (i, COL_Z // SSD_INNER)),
            pl.BlockSpec((16, N_MOD * D), lambda i: (0, 0)),
            pl.BlockSpec((1, SSD_INNER), lambda i: (0, 0)),
            pl.BlockSpec((SSD_INNER + HG_IV, D), lambda i: (0, 0))],
        out_specs=pl.BlockSpec((tm, D), lambda i: (i, 0)),
        out_shape=jax.ShapeDtypeStruct((T, D), F32),
        compiler_params=_cparams(("arbitrary",)),
        name="ssm_out",
    )(xp, xs, yp, ys, op, os_, proj, mod_l, nw, w)


def _attn_out_kernel(x_ref, ap_ref, as_ref, mod_ref, w_ref, out_ref, *, tm):
    i = pl.program_id(0)
    mix = jnp.dot(_pick(i, tm, ap_ref, as_ref), w_ref[...], preferred_element_type=F32)
    out_ref[...] = x_ref[...] + _mod_vec(mod_ref, _mod_row(i, tm), 2) * mix


def _attn_out(x, ap, as_, mod_l, w, *, tm):
    return pl.pallas_call(
        functools.partial(_attn_out_kernel, tm=tm),
        grid=(T // tm,),
        in_specs=[pl.BlockSpec((tm, D), lambda i: (i, 0))] + _two_src_specs(tm) + [
            pl.BlockSpec((16, N_MOD * D), lambda i: (0, 0)),
            pl.BlockSpec((D, D), lambda i: (0, 0))],
        out_specs=pl.BlockSpec((tm, D), lambda i: (i, 0)),
        out_shape=jax.ShapeDtypeStruct((T, D), F32),
        compiler_params=_cparams(("arbitrary",)),
        name="attn_out",
    )(x, ap, as_, mod_l, w)


def _route_kernel(x_ref, mod_ref, nw_ref, rw_ref, rb_ref, h_ref, meta_ref, metat_ref, cnt_ref):
    i = pl.program_id(0)
    r = _mod_row(i, TB)
    h = _rms(x_ref[...], nw_ref[...]) * (1.0 + _mod_vec(mod_ref, r, 4)) + _mod_vec(mod_ref, r, 3)
    hb = h.astype(BF16)
    h_ref[...] = hb
    logits = jnp.dot(hb, rw_ref[...], preferred_element_type=F32) + rb_ref[...]
    lane = lax.broadcasted_iota(I32, (TB, LANES), 1)
    lane_f = lane.astype(F32)
    vals, hots, idxs = [], [], []
    for _ in range(TOP_K):
        m = jnp.max(logits, axis=-1, keepdims=True)
        idx = jnp.min(jnp.where(logits == m, lane_f, float(LANES)), axis=-1, keepdims=True)
        hot = lane_f == idx
        vals.append(m)
        idxs.append(idx)
        hots.append(hot)
        logits = jnp.where(hot, -jnp.inf, logits)
    es = [jnp.exp(v - vals[0]) for v in vals]
    den = es[0] + es[1] + es[2] + es[3]

    r_i = lax.broadcasted_iota(I32, (TB, TB), 0)
    c_i = lax.broadcasted_iota(I32, (TB, TB), 1)
    below = (c_i < r_i).astype(BF16)
    base = jnp.zeros((1, LANES), F32)
    ranks = []
    for k in range(TOP_K):
        hot_f = hots[k].astype(F32)
        before = jnp.dot(below, hots[k].astype(BF16), preferred_element_type=F32)
        ranks.append(base + before)
        base = base + jnp.sum(hot_f, axis=0, keepdims=True)
    units = jnp.floor((base + (SEG - 1)) * (1.0 / SEG))
    lr = lax.broadcasted_iota(I32, (LANES, LANES), 0)
    lc = lax.broadcasted_iota(I32, (LANES, LANES), 1)
    seg_start = SEG * jnp.dot(jnp.broadcast_to(units, (SUBLANES, LANES)).astype(BF16),
                              (lr < lc).astype(BF16), preferred_element_type=F32)[0:1]
    meta = jnp.zeros((TB, LANES), F32)
    for k in range(TOP_K):
        row = jnp.sum(hots[k].astype(F32) * (seg_start + ranks[k]), axis=-1, keepdims=True)
        meta = jnp.where(lane == k, idxs[k], meta)
        meta = jnp.where(lane == TOP_K + k, es[k] / den, meta)
        meta = jnp.where(lane == 2 * TOP_K + k, row, meta)
    meta_ref[...] = meta
    metat_ref[...] = meta.T
    cnt_ref[...] = base


def _route(x, mod_l, nw, rw, rb):
    return pl.pallas_call(
        _route_kernel,
        grid=(NB,),
        in_specs=[pl.BlockSpec((TB, D), lambda i: (i, 0)),
                  pl.BlockSpec((16, N_MOD * D), lambda i: (0, 0)),
                  pl.BlockSpec((1, D), lambda i: (0, 0)),
                  pl.BlockSpec((D, LANES), lambda i: (0, 0)),
                  pl.BlockSpec((1, LANES), lambda i: (0, 0))],
        out_specs=[pl.BlockSpec((TB, D), lambda i: (i, 0)),
                   pl.BlockSpec((TB, LANES), lambda i: (i, 0)),
                   pl.BlockSpec((LANES, TB), lambda i: (0, i)),
                   pl.BlockSpec((None, 1, LANES), lambda i: (i, 0, 0))],
        out_shape=[jax.ShapeDtypeStruct((T, D), BF16),
                   jax.ShapeDtypeStruct((T, LANES), F32),
                   jax.ShapeDtypeStruct((LANES, T), F32),
                   jax.ShapeDtypeStruct((NB, 1, LANES), F32)],
        compiler_params=_cparams(("arbitrary",)),
        name="moe_route",
    )(x, mod_l, nw, rw, rb)


def _copy(src, dst, sem):
    return pltpu.make_async_copy(src, dst, sem)


def _segment_copies(n, src, dst, sizes, make, wait):
    src = jnp.asarray(src, I32)
    dst = jnp.asarray(dst, I32)
    for size in sizes:
        take = n & size

        @pl.when(take != 0)
        def _(src=src, dst=dst, size=size):
            cp = make(pl.multiple_of(src, SEG), pl.multiple_of(dst, SEG), size)
            if wait:
                cp.wait()
            else:
                cp.start()

        src = src + take
        dst = dst + take


def _sort_push_kernel(tab_ref, tabp_ref, ztab_ref, nv_ref, metat_ref, h_ref, xs_hbm, buf, zb, sem, zsem):
    b = pl.program_id(0)
    slot = b & 1

    def zero_copies(wait):
        def per_expert(e, carry):
            _segment_copies(ztab_ref[1, e], 0, ztab_ref[0, e], PAD_SIZES,
                            lambda s, d, size: _copy(zb.at[pl.ds(0, size)], xs_hbm.at[pl.ds(d, size)], zsem), wait)
            return carry

        lax.fori_loop(0, N_EXP, per_expert, 0)

        def per_tile(j, carry):
            cp = _copy(zb, xs_hbm.at[pl.ds(pl.multiple_of(j * TM_E, TM_E), TM_E)], zsem)
            if wait:
                cp.wait()
            else:
                cp.start()
            return carry

        lax.fori_loop(nv_ref[0], N_ETILES, per_tile, 0)

    @pl.when(b == 0)
    def _():
        zb[...] = jnp.zeros_like(zb)
        zero_copies(False)

    ch = 256
    for c in range(RB // ch):
        rows = (lax.broadcasted_iota(I32, (ch, TB), 0) + c * ch).astype(F32)
        hit = rows == metat_ref[2 * TOP_K:2 * TOP_K + 1, :]
        for k in range(1, TOP_K):
            hit = hit | (rows == metat_ref[2 * TOP_K + k:2 * TOP_K + k + 1, :])
        buf[slot, c * ch:(c + 1) * ch, :] = jnp.dot(hit.astype(BF16), h_ref[...], preferred_element_type=F32)

    def push(t_ref, s, wait):
        def per_expert(e, carry):
            _segment_copies(
                t_ref[0, e], t_ref[1, e], t_ref[2, e], SEG_SIZES,
                lambda so, do, size: _copy(buf.at[s, pl.ds(so, size)], xs_hbm.at[pl.ds(do, size)], sem.at[s]), wait)
            return carry

        lax.fori_loop(0, N_EXP, per_expert, 0)

    @pl.when(b > 0)
    def _():
        push(tabp_ref, 1 - slot, True)

    push(tab_ref, slot, False)

    @pl.when(b == NB - 1)
    def _():
        push(tab_ref, slot, True)

    @pl.when(b == 0)
    def _():
        zero_copies(True)


def _sort_push(tab, ztab, n_valid, metat, h):
    return pl.pallas_call(
        _sort_push_kernel,
        grid=(NB,),
        in_specs=[pl.BlockSpec((None, 3, N_EXP), lambda b: (b, 0, 0), memory_space=pltpu.SMEM),
                  pl.BlockSpec((None, 3, N_EXP), lambda b: (jnp.maximum(b - 1, 0), 0, 0),
                               memory_space=pltpu.SMEM),
                  pl.BlockSpec(memory_space=pltpu.SMEM),
                  pl.BlockSpec(memory_space=pltpu.SMEM),
                  pl.BlockSpec((LANES, TB), lambda b: (0, b)),
                  pl.BlockSpec((TB, D), lambda b: (b, 0))],
        out_specs=pl.BlockSpec(memory_space=pl.ANY),
        out_shape=jax.ShapeDtypeStruct((P_ROWS, D), F32),
        scratch_shapes=[pltpu.VMEM((2, RB, D), F32), pltpu.VMEM((TM_E, D), F32),
                        pltpu.SemaphoreType.DMA((2,)), pltpu.SemaphoreType.DMA(())],
        compiler_params=_cparams(("arbitrary",)),
        name="moe_sort_push",
    )(tab, tab, ztab, n_valid, metat, h)


def _expert_kernel(te_ref, tr_ref, nv_ref, x_ref, w1_ref, b1_ref, w2_ref, b2_ref, y_ref, w1_s, w2_s):
    j = pl.program_id(0)
    valid = j < nv_ref[0]
    fresh = jnp.logical_or(j == 0, te_ref[j] != te_ref[jnp.maximum(j - 1, 0)])

    @pl.when(jnp.logical_and(valid, fresh))
    def _():
        w1_s[...] = w1_ref[...].astype(BF16)
        w2_s[...] = w2_ref[...].astype(BF16)

    @pl.when(valid)
    def _():
        rows = lax.broadcasted_iota(I32, (TM_E, 1), 0)
        x = jnp.where(rows < tr_ref[j], x_ref[...], 0.0).astype(BF16)
        gu = jnp.dot(x, w1_s[...], preferred_element_type=F32) + b1_ref[...]
        gate = jnp.minimum(gu[:, :FF], SWIGLU_LIMIT)
        up = jnp.clip(gu[:, FF:], -SWIGLU_LIMIT, SWIGLU_LIMIT)
        act = (up + 1.0) * gate * jax.nn.sigmoid(SWIGLU_ALPHA * gate)
        y_ref[...] = jnp.dot(act.astype(BF16), w2_s[...], preferred_element_type=F32) + b2_ref[...]

    @pl.when(jnp.logical_not(valid))
    def _():
        y_ref[...] = jnp.zeros_like(y_ref)


def _experts(layer, tile_exp, tile_rows, n_valid, xs, w1, b1, w2, b2):
    def wmap(j, te, tr, nv):
        return (layer, te[j], 0, 0)

    return pl.pallas_call(
        _expert_kernel,
        grid_spec=pltpu.PrefetchScalarGridSpec(
            num_scalar_prefetch=3,
            grid=(N_ETILES,),
            in_specs=[pl.BlockSpec((TM_E, D), lambda j, te, tr, nv: (j, 0)),
                      pl.BlockSpec((None, None, D, 2 * FF), wmap),
                      pl.BlockSpec((None, None, 1, 2 * FF), wmap),
                      pl.BlockSpec((None, None, FF, D), wmap),
                      pl.BlockSpec((None, None, 1, D), wmap)],
            out_specs=pl.BlockSpec((TM_E, D), lambda j, te, tr, nv: (j, 0)),
            scratch_shapes=[pltpu.VMEM((D, 2 * FF), BF16), pltpu.VMEM((FF, D), BF16)]),
        out_shape=jax.ShapeDtypeStruct((P_ROWS, D), F32),
        compiler_params=_cparams(("arbitrary",)),
        name="moe_experts",
    )(tile_exp, tile_rows, n_valid, xs, w1, b1.reshape(DEPTH, N_EXP, 1, 2 * FF), w2,
      b2.reshape(DEPTH, N_EXP, 1, D))


def _combine_kernel(tab_ref, tabn_ref, y_hbm, x_ref, meta_ref, mod_ref, out_ref, buf, sem):
    b = pl.program_id(0)
    slot = b & 1

    def pull(t_ref, s, wait):
        def per_expert(e, carry):
            _segment_copies(
                t_ref[0, e], t_ref[2, e], t_ref[1, e], SEG_SIZES,
                lambda so, do, size: _copy(y_hbm.at[pl.ds(so, size)], buf.at[s, pl.ds(do, size)], sem.at[s]), wait)
            return carry

        lax.fori_loop(0, N_EXP, per_expert, 0)

    @pl.when(b == 0)
    def _():
        pull(tab_ref, slot, False)

    @pl.when(b + 1 < NB)
    def _():
        pull(tabn_ref, 1 - slot, False)

    pull(tab_ref, slot, True)

    used = tab_ref[1, N_EXP - 1] + tab_ref[0, N_EXP - 1]
    meta = meta_ref[...]
    ch = 256
    acc = jnp.zeros((TB, D), F32)
    for c in range(RB // ch):
        cols = (lax.broadcasted_iota(I32, (TB, ch), 1) + c * ch).astype(F32)
        g = jnp.zeros((TB, ch), F32)
        for k in range(TOP_K):
            g = g + jnp.where(cols == meta[:, 2 * TOP_K + k:2 * TOP_K + k + 1],
                              meta[:, TOP_K + k:TOP_K + k + 1], 0.0)
        rows = lax.broadcasted_iota(I32, (ch, 1), 0) + c * ch
        y = jnp.where(rows < used, buf[slot, c * ch:(c + 1) * ch, :], 0.0)
        acc = acc + _bdot(g, y)
    out_ref[...] = x_ref[...] + _mod_vec(mod_ref, _mod_row(b, TB), 5) * acc


def _combine(tab, y, x, meta, mod_l):
    return pl.pallas_call(
        _combine_kernel,
        grid=(NB,),
        in_specs=[pl.BlockSpec((None, 3, N_EXP), lambda b: (b, 0, 0), memory_space=pltpu.SMEM),
                  pl.BlockSpec((None, 3, N_EXP), lambda b: (jnp.minimum(b + 1, NB - 1), 0, 0),
                               memory_space=pltpu.SMEM),
                  pl.BlockSpec(memory_space=pl.ANY),
                  pl.BlockSpec((TB, D), lambda b: (b, 0)),
                  pl.BlockSpec((TB, LANES), lambda b: (b, 0)),
                  pl.BlockSpec((16, N_MOD * D), lambda b: (0, 0))],
        out_specs=pl.BlockSpec((TB, D), lambda b: (b, 0)),
        out_shape=jax.ShapeDtypeStruct((T, D), F32),
        scratch_shapes=[pltpu.VMEM((2, RB, D), F32), pltpu.SemaphoreType.DMA((2,))],
        compiler_params=_cparams(("arbitrary",)),
        name="moe_combine",
    )(tab, tab, y, x, meta, mod_l)


def _moe_layer(layer, x, mod_l, nw, rw, rb, w1, b1, w2, b2):
    rw_p = jnp.zeros((D, LANES), BF16).at[:, :N_EXP].set(rw.astype(BF16))
    rb_p = jnp.full((1, LANES), -jnp.inf, F32).at[0, :N_EXP].set(rb)
    h, meta, metat, cnt = _route(x, mod_l, nw, rw_p, rb_p)

    counts = cnt[:, 0, :N_EXP].astype(I32)
    n_seg = (counts + SEG - 1) // SEG * SEG
    seg_start = jnp.cumsum(n_seg, axis=1) - n_seg
    tot = jnp.sum(n_seg, axis=0)
    padded = (tot + TM_E - 1) // TM_E * TM_E
    ends = jnp.cumsum(padded)
    offs = ends - padded
    glob = offs[None, :] + jnp.cumsum(n_seg, axis=0) - n_seg
    tab = jnp.stack([n_seg, seg_start, glob], axis=1)
    ztab = jnp.stack([offs + tot, padded - tot], axis=0)
    n_valid = ends[-1] // TM_E
    starts = jnp.arange(N_ETILES, dtype=I32) * TM_E
    te = jnp.minimum(jnp.sum((ends[None, :] <= starts[:, None]).astype(I32), axis=1), N_EXP - 1)
    last = jnp.sum((ends[:-1] < ends[-1]).astype(I32))
    te = jnp.where(starts < ends[-1], te, last)
    tr = jnp.clip(offs[te] + tot[te] - starts, 0, TM_E)
    nv = n_valid.reshape(1)

    xs = _sort_push(tab, ztab, nv, metat, h)
    y = _experts(layer, te, tr, nv, xs, w1, b1, w2, b2)
    return _combine(tab, y, x, meta, mod_l)


def _head_sel():
    g = np.zeros((D, LANES), np.float32)
    for c in range(D):
        g[c, c // HEAD_DIM] = 1.0
    return jnp.asarray(g, BF16)


def _head_rms(x, sel, sel_t, w):
    n = x.shape[1]
    hi, lo = _split_hi_lo(x * x)
    ss = (jnp.dot(hi, sel[:n], preferred_element_type=F32)
          + jnp.dot(lo, sel[:n], preferred_element_type=F32))
    inv = lax.rsqrt(ss * (1.0 / HEAD_DIM) + EPS)
    hi, lo = _split_hi_lo(inv)
    inv_e = (jnp.dot(hi, sel_t[:, :n], preferred_element_type=F32)
             + jnp.dot(lo, sel_t[:, :n], preferred_element_type=F32))
    return x * inv_e * w


def _rope(x, cos, sin):
    n = x.shape[1]
    lane = lax.broadcasted_iota(I32, x.shape, 1)
    first = (lane % (HEAD_DIM // 2)) < (HEAD_DIM // 4)
    rot = jnp.where(first, -pltpu.roll(x, n - HEAD_DIM // 4, axis=1), pltpu.roll(x, HEAD_DIM // 4, axis=1))
    return x * cos + rot * sin


def _qkv_kernel(x_ref, mod_ref, nw_ref, w_ref, sel_ref, selt_ref, qw_ref, kw_ref, cos_ref, sin_ref,
                q_ref, k_ref, v_ref, kf_ref, vf_ref, *, tm):
    i = pl.program_id(0)
    r = _mod_row(i, tm)
    h = _rms(x_ref[...], nw_ref[...]) * (1.0 + _mod_vec(mod_ref, r, 1)) + _mod_vec(mod_ref, r, 0)
    qkv = _bdot(h, w_ref[...])
    nq, nk = ATT_HEADS * HEAD_DIM, KV_HEADS * HEAD_DIM
    q = _head_rms(qkv[:, :nq], sel_ref[...], selt_ref[...], qw_ref[...])
    k = _head_rms(qkv[:, nq:nq + nk], sel_ref[...], selt_ref[...], kw_ref[...])
    v = qkv[:, nq + nk:]
    kf_ref[...] = k
    vf_ref[...] = v
    latent = i * tm >= NP_TOK
    cos = jnp.where(latent, cos_ref[...], 1.0)
    sin = jnp.where(latent, sin_ref[...], 0.0)
    q = _rope(q, cos, sin) * (HEAD_DIM ** -0.5)
    k = _rope(k, cos[:, :nk], sin[:, :nk])
    for hd in range(ATT_HEADS):
        q_ref[hd] = q[:, hd * HEAD_DIM:(hd + 1) * HEAD_DIM].astype(BF16)
    for hd in range(KV_HEADS):
        k_ref[hd] = k[:, hd * HEAD_DIM:(hd + 1) * HEAD_DIM].astype(BF16)
        v_ref[hd] = v[:, hd * HEAD_DIM:(hd + 1) * HEAD_DIM].astype(BF16)


def _qkv(x, mod_l, nw, w, qw, kw, cos, sin, *, tm):
    nk = KV_HEADS * HEAD_DIM
    sel = _head_sel()
    nblk = DEC_SEQ // tm

    def cmap(i):
        return (jnp.maximum(i - NP_TOK // tm, 0) % nblk, 0)

    return pl.pallas_call(
        functools.partial(_qkv_kernel, tm=tm),
        grid=(T // tm,),
        in_specs=[pl.BlockSpec((tm, D), lambda i: (i, 0)),
                  pl.BlockSpec((16, N_MOD * D), lambda i: (0, 0)),
                  pl.BlockSpec((1, D), lambda i: (0, 0)),
                  pl.BlockSpec((D, D + 2 * nk), lambda i: (0, 0)),
                  pl.BlockSpec((D, LANES), lambda i: (0, 0)),
                  pl.BlockSpec((LANES, D), lambda i: (0, 0)),
                  pl.BlockSpec((1, D), lambda i: (0, 0)),
                  pl.BlockSpec((1, nk), lambda i: (0, 0)),
                  pl.BlockSpec((tm, D), cmap),
                  pl.BlockSpec((tm, D), cmap)],
        out_specs=[pl.BlockSpec((ATT_HEADS, tm, HEAD_DIM), lambda i: (0, i, 0)),
                   pl.BlockSpec((KV_HEADS, tm, HEAD_DIM), lambda i: (0, i, 0)),
                   pl.BlockSpec((KV_HEADS, tm, HEAD_DIM), lambda i: (0, i, 0)),
                   pl.BlockSpec((tm, nk), lambda i: (i, 0)),
                   pl.BlockSpec((tm, nk), lambda i: (i, 0))],
        out_shape=[jax.ShapeDtypeStruct((ATT_HEADS, T, HEAD_DIM), BF16),
                   jax.ShapeDtypeStruct((KV_HEADS, T, HEAD_DIM), BF16),
                   jax.ShapeDtypeStruct((KV_HEADS, T, HEAD_DIM), BF16),
                   jax.ShapeDtypeStruct((T, nk), F32),
                   jax.ShapeDtypeStruct((T, nk), F32)],
        compiler_params=_cparams(("arbitrary",)),
        name="attn_qkv",
    )(x, mod_l, nw, w, sel, sel.T, qw, kw, cos, sin)


def _attn_kernel(*refs, has_cache):
    if has_cache:
        q_ref, k_ref, v_ref, kc_ref, vc_ref, o_ref = refs
    else:
        q_ref, k_ref, v_ref, o_ref = refs
    k = k_ref[0]
    v = v_ref[0]
    outs = []
    for j in range(Q_PER_KV):
        q = q_ref[j]
        s = lax.dot_general(q, k, (((1,), (1,)), ((), ())), preferred_element_type=F32)
        m = jnp.max(s, axis=-1, keepdims=True)
        if has_cache:
            sc = lax.dot_general(q, kc_ref[0, 0], (((1,), (1,)), ((), ())), preferred_element_type=F32)
            m = jnp.maximum(m, jnp.max(sc, axis=-1, keepdims=True))
            pc = jnp.exp(sc - m)
        p = jnp.exp(s - m)
        den = jnp.sum(p, axis=-1, keepdims=True)
        o = jnp.dot(p.astype(BF16), v, preferred_element_type=F32)
        if has_cache:
            den = den + jnp.sum(pc, axis=-1, keepdims=True)
            o = o + jnp.dot(pc.astype(BF16), vc_ref[0, 0], preferred_element_type=F32)
        outs.append(o / den)
    o_ref[...] = jnp.concatenate(outs, axis=1).astype(BF16)


def _attn_call(q, k, v, *, L, n_seq, row0, tq, cache):
    rb_q = row0 // tq
    rb_k = row0 // L
    nq = L // tq
    has_cache = cache is not None
    in_specs = [pl.BlockSpec((Q_PER_KV, tq, HEAD_DIM), lambda b, g, t: (g, rb_q + b * nq + t, 0)),
                pl.BlockSpec((1, L, HEAD_DIM), lambda b, g, t: (g, rb_k + b, 0)),
                pl.BlockSpec((1, L, HEAD_DIM), lambda b, g, t: (g, rb_k + b, 0))]
    args = [q, k, v]
    if has_cache:
        in_specs += [pl.BlockSpec((1, 1, PAST, HEAD_DIM), lambda b, g, t: (b, g, 0, 0))] * 2
        args += list(cache)
    gw = Q_PER_KV * HEAD_DIM
    return pl.pallas_call(
        functools.partial(_attn_kernel, has_cache=has_cache),
        grid=(n_seq, KV_HEADS, nq),
        in_specs=in_specs,
        out_specs=pl.BlockSpec((tq, gw), lambda b, g, t: (b * nq + t, g)),
        out_shape=jax.ShapeDtypeStruct((n_seq * L, D), BF16),
        compiler_params=_cparams(("arbitrary", "arbitrary", "arbitrary")),
        name=f"attn_L{L}",
    )(*args)


def _rope_tables():
    rows = DEC_SEQ // GRID_W
    row = jnp.repeat(jnp.arange(rows), GRID_W)
    colp = jnp.tile(jnp.arange(GRID_W), rows)
    pos = jnp.stack([row, colp], axis=-1).astype(F32)
    half = HEAD_DIM // 2
    inv_freq = ROPE_THETA ** (-jnp.arange(0, half, 2, dtype=F32) / half)
    ang = pos[:, :, None] * inv_freq
    ang = jnp.concatenate([ang, ang], axis=-1).reshape(DEC_SEQ, HEAD_DIM)
    ang = jnp.tile(ang, (1, ATT_HEADS))
    return jnp.cos(ang), jnp.sin(ang)


def _final_kernel(x_ref, w_ref, o_ref):
    o_ref[...] = _rms(x_ref[...], w_ref[...])


def _final_norm(x, w, *, row0, n_rows, tm):
    rb = row0 // tm
    return pl.pallas_call(
        _final_kernel,
        grid=(n_rows // tm,),
        in_specs=[pl.BlockSpec((tm, D), lambda i: (rb + i, 0)),
                  pl.BlockSpec((1, D), lambda i: (0, 0))],
        out_specs=pl.BlockSpec((tm, D), lambda i: (i, 0)),
        out_shape=jax.ShapeDtypeStruct((n_rows, D), F32),
        compiler_params=_cparams(("arbitrary",)),
        name="final_norm",
    )(x, w)


def _ssd_params(conv_w, conv_b, dt_bias, a_log, d_skip):
    def dt_layout(p):
        g = p.reshape(2, SSD_G, HPG).transpose(1, 0, 2).reshape(SSD_G, 2 * HPG)
        return jnp.zeros((SSD_G, LANES), F32).at[:, :2 * HPG].set(g).reshape(1, SSD_G * LANES)

    ni = SSD_INNER
    nb = SSD_G * SSD_N
    return dict(
        cwx=conv_w[:, :ni], cwb=conv_w[:, ni:ni + nb], cwc=conv_w[:, ni + nb:],
        cbx=conv_b[None, :ni], cbb=conv_b[None, ni:ni + nb], cbc=conv_b[None, ni + nb:],
        dtb=dt_layout(dt_bias), alog=dt_layout(a_log),
        dcol=jnp.repeat(d_skip, SSD_P)[None, :], rsel=_ssd_rsel())


def _ssm_in_weights(w):
    ni, cd = SSD_INNER, SSD_INNER + 2 * SSD_G * SSD_N
    o_xbc, o_dt = ni, ni + cd
    o_hq = o_dt + 2 * SSD_HEADS
    o_hf, o_hi = o_hq + HG_QK, o_hq + 3 * HG_QK
    o_hg = o_hi + HG_IV
    main = jnp.concatenate([w[:, :ni], w[:, o_hq:o_hf], w[:, o_hi:o_hg], w[:, o_hg:o_hg + HG_IV],
                            w[:, o_hf:o_hi], w[:, o_xbc:o_dt]], axis=1).astype(BF16)
    wdt = w[:, o_dt:o_hq].reshape(D, 2, SSD_G, HPG).transpose(0, 2, 1, 3).reshape(D, SSD_G, 2 * HPG)
    wdt = jnp.zeros((D, SSD_G, LANES), F32).at[:, :, :2 * HPG].set(wdt).reshape(D, SSD_G * LANES)
    return main, wdt.astype(BF16)


def kernel(x_prompt, x_sample, c, c_ctx, state_ssd, state_hgrn, cache_k, cache_v, mod_w, mod_b, norm1_w,
           norm2_w, ssm_in_w, ssd_conv_w, ssd_conv_b, ssd_dt_bias, ssd_a_log, ssd_d, ssd_norm_w, hg_lb,
           hg_norm_w, ssm_out_w, attn_in_w, q_norm_w, k_norm_w, attn_out_w, router_w, router_b, exp_w1,
           exp_b1, exp_w2, exp_b2, final_norm_w):
    xp = x_prompt.reshape(NP_TOK, D)
    xs = x_sample.reshape(NS_TOK, D)
    cvec = jnp.zeros((16, D), F32).at[0].set(c_ctx).at[1:1 + DEC_BATCH].set(c)
    mod = _mod_table(cvec, mod_w, mod_b)

    w_main, w_dt = _ssm_in_weights(ssm_in_w[0])
    proj, dtp = _inproj(xp, xs, mod[0], norm1_w[0][None], w_main, w_dt, tm=1024, tn=1280)
    prm = _ssd_params(ssd_conv_w[0], ssd_conv_b[0], ssd_dt_bias[0], ssd_a_log[0], ssd_d[0])
    yp, st_ssd = _ssd_call(proj, dtp, prm, L=SEQ, n_seq=BATCH, row0=0, s0=None, want_state=True)
    s0 = state_ssd[:, 0].reshape(DEC_BATCH, 2, SSD_G, GW, SSD_N)
    (ys,) = _ssd_call(proj, dtp, prm, L=DEC_SEQ, n_seq=DEC_BATCH, row0=NP_TOK, s0=s0, want_state=False)
    hg_nw = hg_norm_w[0][None]
    op, st_hg = _gla_call(proj, hg_lb, hg_nw, layer=0, L=SEQ, n_seq=BATCH, row0=0, s0=None, want_state=True,
                          hpb=4)
    (os_,) = _gla_call(proj, hg_lb, hg_nw, layer=0, L=DEC_SEQ, n_seq=DEC_BATCH, row0=NP_TOK,
                       s0=state_hgrn[:, 0], want_state=False, hpb=2)
    x = _ssm_out(xp, xs, yp, ys, op, os_, proj, mod[0], ssd_norm_w[0][None], ssm_out_w[0].astype(BF16), tm=512)
    x = _moe_layer(0, x, mod[0], norm2_w[0][None], router_w[0], router_b[0], exp_w1, exp_b1, exp_w2, exp_b2)

    cos, sin = _rope_tables()
    q, k, v, kf, vf = _qkv(x, mod[1], norm1_w[1][None], attn_in_w[0].astype(BF16),
                           jnp.tile(q_norm_w[0], ATT_HEADS)[None], jnp.tile(k_norm_w[0], KV_HEADS)[None],
                           cos, sin, tm=256)
    ap = _attn_call(q, k, v, L=SEQ, n_seq=BATCH, row0=0, tq=SEQ, cache=None)
    ck = cache_k[:, 0].transpose(0, 2, 1, 3).astype(BF16)
    cv = cache_v[:, 0].transpose(0, 2, 1, 3).astype(BF16)
    as_ = _attn_call(q, k, v, L=DEC_SEQ, n_seq=DEC_BATCH, row0=NP_TOK, tq=256, cache=(ck, cv))
    x = _attn_out(x, ap, as_, mod[1], attn_out_w[0].astype(BF16), tm=512)
    x = _moe_layer(1, x, mod[1], norm2_w[1][None], router_w[1], router_b[1], exp_w1, exp_b1, exp_w2, exp_b2)

    y_prompt = _final_norm(x, final_norm_w[None], row0=0, n_rows=NP_TOK, tm=512).reshape(BATCH, SEQ, D)
    y_sample = _final_norm(x, final_norm_w[None], row0=NP_TOK, n_rows=NS_TOK, tm=512).reshape(
        DEC_BATCH, DEC_SEQ, D)
    new_ssd = st_ssd.reshape(BATCH, 1, 2, SSD_HEADS, SSD_P, SSD_N)
    new_hg = st_hg.reshape(BATCH, 1, 2, HG_HEADS, HG_K, HG_V)
    new_k = kf[:NP_TOK].reshape(BATCH, 1, SEQ, KV_HEADS, HEAD_DIM)
    new_v = vf[:NP_TOK].reshape(BATCH, 1, SEQ, KV_HEADS, HEAD_DIM)
    return (y_prompt, y_sample, new_ssd, new_hg, new_k, new_v)
```

```python
import functools

import jax
import jax.numpy as jnp
import numpy as np
from jax import lax
from jax.experimental import pallas as pl
from jax.experimental.pallas import tpu as pltpu

F32 = jnp.float32
BF16 = jnp.bfloat16
I32 = jnp.int32

D = 1024
BATCH, SEQ = 32, 256
DEC_BATCH, DEC_SEQ = 8, 1024
PAST = 256
NP_TOK = BATCH * SEQ
NS_TOK = DEC_BATCH * DEC_SEQ
T = NP_TOK + NS_TOK
DEPTH = 2
N_MOD = 6
EPS = 1e-6
GRID_W = 64
ROPE_THETA = 10000.0

SSD_HEADS, SSD_P, SSD_N, SSD_G = 16, 64, 128, 2
SSD_INNER = SSD_HEADS * SSD_P
SSD_CONV = 5
HPG = SSD_HEADS // SSD_G
GW = HPG * SSD_P
HG_HEADS, HG_K, HG_V = 8, 128, 128
HG_QK = HG_HEADS * HG_K
HG_IV = HG_HEADS * HG_V
ATT_HEADS, KV_HEADS, HEAD_DIM = 16, 4, 64
Q_PER_KV = ATT_HEADS // KV_HEADS
N_EXP, TOP_K, FF = 32, 4, 1024
SWIGLU_ALPHA, SWIGLU_LIMIT = 1.702, 7.0

LANES = 128
SUBLANES = 8
VMEM_LIMIT = 56 * 1024 * 1024

COL_Z, COL_HQ, COL_HI, COL_HG, COL_HF, COL_XBC = 0, 1024, 2048, 3072, 4096, 6144
PROJ_W = 7680

SSD_TC = 256
GLA_TC = 64
GLA_GROUP = 256

TB = 512
NB = T // TB
SEG = SUBLANES
RB = TB * TOP_K + N_EXP * SEG
TM_E = 512
SEG_SIZES = tuple(SEG << i for i in range((TB // SEG).bit_length() - 1, -1, -1))
PAD_SIZES = tuple(s for s in SEG_SIZES if s < TM_E)
BUF_SIZES = tuple(SEG << i for i in range((RB // SEG).bit_length() - 1, -1, -1))
P_ROWS_MAX = T * TOP_K + NB * N_EXP * (SEG - 1) + N_EXP * (TM_E - SEG)
N_ETILES = -(-P_ROWS_MAX // TM_E)
P_ROWS = N_ETILES * TM_E


def _cparams(sem):
    return pltpu.CompilerParams(dimension_semantics=sem, vmem_limit_bytes=VMEM_LIMIT)


def _silu(x):
    return x * jax.nn.sigmoid(x)


def _bdot(a, b):
    return jnp.dot(a.astype(BF16), b.astype(BF16), preferred_element_type=F32)


def _bdot_nt(a, b):
    return lax.dot_general(a.astype(BF16), b.astype(BF16), (((1,), (1,)), ((), ())),
                           preferred_element_type=F32)


def _mod_row(i, tm):
    start = i * tm
    return jnp.where(start < NP_TOK, 0, 1 + (start - NP_TOK) // DEC_SEQ)


def _mod_vec(mod_ref, r, k):
    return mod_ref[pl.ds(r, 1), k * D:(k + 1) * D]


def _rms(x, w):
    return x * lax.rsqrt(jnp.mean(x * x, axis=-1, keepdims=True) + EPS) * w


def _two_src_specs(tm, width=D):
    npt = NP_TOK // tm
    return [pl.BlockSpec((tm, width), lambda i, *_: (jnp.minimum(i, npt - 1), 0)),
            pl.BlockSpec((tm, width), lambda i, *_: (jnp.maximum(i - npt, 0), 0))]


def _pick(i, tm, p_ref, s_ref):
    return jnp.where(i * tm < NP_TOK, p_ref[...], s_ref[...])


def _mod_kernel(c_ref, w_ref, b_ref, o_ref):
    o_ref[0] = _bdot(_silu(c_ref[...]), w_ref[0]) + b_ref[0]


def _mod_table(cvec, mod_w, mod_b):
    tn = 1536
    return pl.pallas_call(
        _mod_kernel,
        grid=(DEPTH, N_MOD * D // tn),
        in_specs=[pl.BlockSpec((16, D), lambda l, n: (0, 0)),
                  pl.BlockSpec((1, D, tn), lambda l, n: (l, 0, n)),
                  pl.BlockSpec((1, 1, tn), lambda l, n: (l, 0, n))],
        out_specs=pl.BlockSpec((1, 16, tn), lambda l, n: (l, 0, n)),
        out_shape=jax.ShapeDtypeStruct((DEPTH, 16, N_MOD * D), F32),
        compiler_params=_cparams(("arbitrary", "arbitrary")),
        name="mod_table",
    )(cvec, mod_w, mod_b.reshape(DEPTH, 1, N_MOD * D))


def _inproj_kernel(xp_ref, xs_ref, mod_ref, nw_ref, w_ref, wx_ref, o_ref, ox_ref, h_ref, *, tm):
    i = pl.program_id(0)

    @pl.when(pl.program_id(1) == 0)
    def _():
        x = _pick(i, tm, xp_ref, xs_ref)
        r = _mod_row(i, tm)
        h = _rms(x, nw_ref[...]) * (1.0 + _mod_vec(mod_ref, r, 1)) + _mod_vec(mod_ref, r, 0)
        hb = h.astype(BF16)
        h_ref[...] = hb
        ox_ref[...] = jnp.dot(hb, wx_ref[...], preferred_element_type=F32)

    o_ref[...] = jnp.dot(h_ref[...], w_ref[...], preferred_element_type=F32)


def _inproj(xp, xs, mod_l, nw, w, wx, *, tm, tn):
    n_out, nx = w.shape[1], wx.shape[1]
    return pl.pallas_call(
        functools.partial(_inproj_kernel, tm=tm),
        grid=(T // tm, n_out // tn),
        in_specs=_two_src_specs(tm) + [
            pl.BlockSpec((16, N_MOD * D), lambda i, n: (0, 0)),
            pl.BlockSpec((1, D), lambda i, n: (0, 0)),
            pl.BlockSpec((D, tn), lambda i, n: (0, n)),
            pl.BlockSpec((D, nx), lambda i, n: (0, 0))],
        out_specs=[pl.BlockSpec((tm, tn), lambda i, n: (i, n)),
                   pl.BlockSpec((tm, nx), lambda i, n: (i, 0))],
        out_shape=[jax.ShapeDtypeStruct((T, n_out), F32), jax.ShapeDtypeStruct((T, nx), F32)],
        scratch_shapes=[pltpu.VMEM((tm, D), BF16)],
        compiler_params=_cparams(("arbitrary", "arbitrary")),
        name="inproj",
    )(xp, xs, mod_l, nw, w, wx)


def _tri(n, upper):
    r = lax.broadcasted_iota(I32, (n, n), 0)
    c = lax.broadcasted_iota(I32, (n, n), 1)
    return (c >= r) if upper else (c <= r)


def _conv_silu(x, w, b, L):
    row = lax.broadcasted_iota(I32, (L, 1), 0)
    acc = x * w[2:3] + b
    for k in (0, 1, 3, 4):
        off = k - 2
        shifted = pltpu.roll(x, (-off) % L, axis=0)
        ok = (row + off >= 0) & (row + off < L)
        acc = acc + jnp.where(ok, shifted, 0.0) * w[k:k + 1]
    return _silu(acc)


def _split_hi_lo(x):
    hi = x.astype(BF16)
    return hi, (x - hi.astype(F32)).astype(BF16)


def _ssd_kernel(*refs, L, has_init, want_state):
    (x_ref, b_ref, c_ref, dt_ref, cwx_ref, cwb_ref, cwc_ref, cbx_ref, cbb_ref, cbc_ref,
     dtb_ref, alog_ref, dcol_ref, rsel_ref) = refs[:14]
    pos = 14
    s0_ref = None
    if has_init:
        s0_ref = refs[pos]
        pos += 1
    y_ref = refs[pos]
    pos += 1
    st_ref = None
    if want_state:
        st_ref = refs[pos]
        pos += 1
    xs_s, bs_s, cs_s, y_s = refs[pos:pos + 4]

    tc = SSD_TC
    nc = L // tc
    xs_s[...] = _conv_silu(x_ref[...], cwx_ref[...], cbx_ref[...], L)
    bs_s[...] = _conv_silu(b_ref[...], cwb_ref[...], cbb_ref[...], L)
    cs_s[...] = _conv_silu(c_ref[...], cwc_ref[...], cbc_ref[...], L)
    y_s[...] = xs_s[...] * dcol_ref[...]

    z = dt_ref[...] + dtb_ref[...]
    dt_all = jnp.maximum(z, 0.0) + jnp.log(1.0 + jnp.exp(-jnp.abs(z)))
    a_row = -jnp.exp(alog_ref[...])

    dirs = (0, 1)
    tris = [_tri(tc, False), _tri(tc, True)]
    tri_b = [t.astype(BF16) for t in tris]
    sts = [s0_ref[d].T if has_init else None for d in dirs]
    for ci in range(nc):
        cidx = (ci, nc - 1 - ci)
        sls = [slice(c * tc, (c + 1) * tc) for c in cidx]
        x = [xs_s[s] for s in sls]
        bm = [bs_s[s] for s in sls]
        cm = [cs_s[s] for s in sls]
        dt = [dt_all[s] for s in sls]
        parts = [_split_hi_lo(dt[d] * a_row) for d in dirs]
        cum = [jnp.dot(tri_b[d], parts[d][0], preferred_element_type=F32)
               + jnp.dot(tri_b[d], parts[d][1], preferred_element_type=F32) for d in dirs]
        cum_t = [cum[d].T for d in dirs]
        end = [cum[0][tc - 1:tc], cum[1][0:1]]
        ex = [_bdot(jnp.concatenate(
            [dt[d], jnp.exp(cum[d]), jnp.exp(end[d] - cum[d]),
             jnp.broadcast_to(jnp.exp(end[d]), (SUBLANES, LANES))], axis=0), rsel_ref[d])
            for d in dirs]
        xdt = [x[d] * ex[d][0:tc] for d in dirs]
        cb = [_bdot_nt(cm[0], bm[0])]
        cb.append(cb[0] if cidx[1] == cidx[0] else _bdot_nt(cm[1], bm[1]))
        lms = [[jnp.exp(jnp.where(tris[d], cum[d][:, d * HPG + h:d * HPG + h + 1]
                                  - cum_t[d][d * HPG + h:d * HPG + h + 1, :], -jnp.inf))
                for h in range(HPG)] for d in dirs]
        ys = [[_bdot(cb[d] * lms[d][h], xdt[d][:, h * SSD_P:(h + 1) * SSD_P]) for h in range(HPG)]
              for d in dirs]
        y = [jnp.concatenate(ys[d], axis=1) for d in dirs]
        for d in dirs:
            if sts[d] is not None:
                y[d] = y[d] + _bdot(cm[d], sts[d]) * ex[d][tc:2 * tc]
        if cidx[0] == cidx[1]:
            y_s[sls[0]] = y_s[sls[0]] + (y[0] + y[1])
        else:
            for d in dirs:
                y_s[sls[d]] = y_s[sls[d]] + y[d]
        sc = [_bdot(bm[d].T, xdt[d] * ex[d][2 * tc:3 * tc]) for d in dirs]
        sts = [sc[d] if sts[d] is None else sts[d] * ex[d][3 * tc:3 * tc + 1] + sc[d] for d in dirs]
    if want_state:
        for d in dirs:
            st_ref[d] = sts[d].T

    y_ref[...] = y_s[...]


def _ssd_rsel():
    r = np.zeros((2, LANES, GW), np.float32)
    for d in range(2):
        for h in range(HPG):
            r[d, d * HPG + h, h * SSD_P:(h + 1) * SSD_P] = 1.0
    return jnp.asarray(r, BF16)


def _ssd_call(proj, dtp, prm, *, L, n_seq, row0, s0, want_state):
    rb = row0 // L
    has_init = s0 is not None
    xbc = COL_XBC
    in_specs = [
        pl.BlockSpec((L, GW), lambda b, g: (rb + b, xbc // GW + g)),
        pl.BlockSpec((L, SSD_N), lambda b, g: (rb + b, (xbc + SSD_INNER) // SSD_N + g)),
        pl.BlockSpec((L, SSD_N), lambda b, g: (rb + b, (xbc + SSD_INNER + SSD_G * SSD_N) // SSD_N + g)),
        pl.BlockSpec((L, LANES), lambda b, g: (rb + b, g)),
        pl.BlockSpec((SSD_CONV, GW), lambda b, g: (0, g)),
        pl.BlockSpec((SSD_CONV, SSD_N), lambda b, g: (0, g)),
        pl.BlockSpec((SSD_CONV, SSD_N), lambda b, g: (0, g)),
        pl.BlockSpec((1, GW), lambda b, g: (0, g)),
        pl.BlockSpec((1, SSD_N), lambda b, g: (0, g)),
        pl.BlockSpec((1, SSD_N), lambda b, g: (0, g)),
        pl.BlockSpec((1, LANES), lambda b, g: (0, g)),
        pl.BlockSpec((1, LANES), lambda b, g: (0, g)),
        pl.BlockSpec((1, GW), lambda b, g: (0, g)),
        pl.BlockSpec((2, LANES, GW), lambda b, g: (0, 0, 0)),
    ]
    args = [proj, proj, proj, dtp, prm["cwx"], prm["cwb"], prm["cwc"], prm["cbx"], prm["cbb"], prm["cbc"],
            prm["dtb"], prm["alog"], prm["dcol"], prm["rsel"]]
    if has_init:
        in_specs.append(pl.BlockSpec((None, 2, None, GW, SSD_N), lambda b, g: (b, 0, g, 0, 0)))
        args.append(s0)
    out_specs = [pl.BlockSpec((L, GW), lambda b, g: (b, g))]
    out_shape = [jax.ShapeDtypeStruct((n_seq * L, SSD_INNER), F32)]
    if want_state:
        out_specs.append(pl.BlockSpec((None, 2, None, GW, SSD_N), lambda b, g: (b, 0, g, 0, 0)))
        out_shape.append(jax.ShapeDtypeStruct((n_seq, 2, SSD_G, GW, SSD_N), F32))
    return pl.pallas_call(
        functools.partial(_ssd_kernel, L=L, has_init=has_init, want_state=want_state),
        grid=(n_seq, SSD_G),
        in_specs=in_specs, out_specs=out_specs, out_shape=out_shape,
        scratch_shapes=[pltpu.VMEM((L, GW), F32), pltpu.VMEM((L, SSD_N), F32),
                        pltpu.VMEM((L, SSD_N), F32), pltpu.VMEM((L, GW), F32)],
        compiler_params=_cparams(("arbitrary", "arbitrary")),
        name=f"ssd_L{L}",
    )(*args)


def _gla_kernel(*refs, L, layer, has_init, want_state, hpb):
    q_ref, ff_ref, fb_ref, v_ref, g_ref, lb_ref, nw_ref = refs[:7]
    pos = 7
    s0_ref = None
    if has_init:
        s0_ref = refs[pos]
        pos += 1
    o_ref = refs[pos]
    pos += 1
    st_ref = None
    if want_state:
        st_ref = refs[pos]
        pos += 1
    qs, ks, ls, os_ = refs[pos:pos + 4]

    tc, gr = GLA_TC, GLA_GROUP
    ng, cpg = L // gr, gr // tc
    lbr = lb_ref[...]
    e = jnp.exp(lbr - jnp.max(lbr, axis=0, keepdims=True))
    sm = e / jnp.sum(e, axis=0, keepdims=True)
    lb = sm[0]
    for j in range(1, layer + 1):
        lb = lb + sm[j]

    qs[...] = _silu(q_ref[...])
    for d, fr in ((0, ff_ref), (1, fb_ref)):
        logit = fr[...]
        lbd = lb[d:d + 1]
        f = lbd + (1.0 - lbd) * jax.nn.sigmoid(logit)
        ks[d] = (1.0 - lbd) * jax.nn.sigmoid(-logit)
        ls[d] = jnp.log(f)

    r_i = lax.broadcasted_iota(I32, (gr, gr), 0)
    c_i = lax.broadcasted_iota(I32, (gr, gr), 1)
    sh = tc.bit_length() - 1
    same = (r_i >> sh) == (c_i >> sh)
    causal = [same & (c_i <= r_i), same & (c_i >= r_i)]
    bd = [m.astype(BF16) for m in causal]
    row_chunk = lax.broadcasted_iota(I32, (gr, 1), 0) >> sh
    col_chunk = lax.broadcasted_iota(I32, (1, gr), 1) >> sh

    def rows_of(x, r):
        return jnp.concatenate(
            [jnp.broadcast_to(x[c * tc + r:c * tc + r + 1], (tc, x.shape[1])) for c in range(cpg)], axis=0)

    def groups(units):
        n = range(len(units))
        ds_ = [u[0] for u in units]
        sl = [(pl.ds(u[2], gr), slice(u[1] * HG_K, (u[1] + 1) * HG_K)) for u in units]
        sts = [u[3] for u in units]
        parts = [_split_hi_lo(ls[ds_[i], sl[i][0], sl[i][1]]) for i in n]
        cums = [jnp.dot(bd[ds_[i]], parts[i][0], preferred_element_type=F32)
                + jnp.dot(bd[ds_[i]], parts[i][1], preferred_element_type=F32) for i in n]
        q = [qs[sl[i][0], sl[i][1]] for i in n]
        k = [ks[ds_[i], sl[i][0], sl[i][1]] for i in n]
        v = [v_ref[sl[i][0], sl[i][1]] for i in n]
        ref_b = [rows_of(cums[i], tc // 2 if ds_[i] else tc // 2 - 1) for i in n]
        end_b = [rows_of(cums[i], 0 if ds_[i] else tc - 1) for i in n]
        att = [_bdot_nt(q[i] * jnp.exp(cums[i] - ref_b[i]), k[i] * jnp.exp(ref_b[i] - cums[i])) for i in n]
        o = [_bdot(jnp.where(causal[ds_[i]], att[i], 0.0), v[i]) for i in n]
        kd_t = [(k[i] * jnp.exp(end_b[i] - cums[i])).T for i in n]
        inc = [_bdot(jnp.concatenate([jnp.where(col_chunk == c, kd_t[i], 0.0) for c in range(cpg)], axis=0), v[i])
               for i in n]
        dec_t = [jnp.exp(end_b[i]).T for i in n]
        stacks = []
        for i in n:
            st, s_in = sts[i], [None] * cpg
            for c in (range(cpg - 1, -1, -1) if ds_[i] else range(cpg)):
                s_in[c] = st
                st = st * dec_t[i][:, c * tc:c * tc + 1] + inc[i][c * HG_K:(c + 1) * HG_K]
            sts[i] = st
            stacks.append(jnp.concatenate(s_in, axis=0))
        qe = [q[i] * jnp.exp(cums[i]) for i in n]
        o = [o[i] + _bdot(jnp.concatenate([jnp.where(row_chunk == c, qe[i], 0.0) for c in range(cpg)], axis=1),
                          stacks[i]) for i in n]
        return o, sts

    def init(d, j):
        return s0_ref[d, j] if has_init else jnp.zeros((HG_K, HG_V), F32)

    heads = range(hpb)
    if ng == 1:
        o, sts = groups([(d, j, 0, init(d, j)) for j in heads for d in (0, 1)])
        outs = [o[2 * j] + o[2 * j + 1] for j in heads]
        finals = [(sts[2 * j], sts[2 * j + 1]) for j in heads]
    else:
        os_[...] = jnp.zeros_like(os_)

        def body(i, carry):
            g0 = (pl.multiple_of(i * gr, gr), pl.multiple_of((ng - 1 - i) * gr, gr))
            o, sts = groups([(d, j, g0[d], carry[j][d]) for j in heads for d in (0, 1)])
            for j in heads:
                cs = slice(j * HG_V, (j + 1) * HG_V)
                for d in (0, 1):
                    os_[pl.ds(g0[d], gr), cs] = os_[pl.ds(g0[d], gr), cs] + o[2 * j + d]
            return tuple((sts[2 * j], sts[2 * j + 1]) for j in heads)

        finals = lax.fori_loop(0, ng, body, tuple((init(0, j), init(1, j)) for j in heads))
        outs = [os_[:, j * HG_V:(j + 1) * HG_V] for j in heads]
    for j in heads:
        cs = slice(j * HG_V, (j + 1) * HG_V)
        if want_state:
            st_ref[0, j] = finals[j][0]
            st_ref[1, j] = finals[j][1]
        o_ref[:, cs] = _rms(outs[j], nw_ref[:, cs]) * _silu(g_ref[:, cs])


def _gla_call(proj, hg_lb, hg_nw, *, layer, L, n_seq, row0, s0, want_state, hpb):
    rb = row0 // L
    has_init = s0 is not None
    w = hpb * HG_K

    def col(c0):
        return lambda b, h: (rb + b, c0 // w + h)

    st_spec = pl.BlockSpec((None, 2, hpb, HG_K, HG_V), lambda b, h: (b, 0, h, 0, 0))
    in_specs = [
        pl.BlockSpec((L, w), col(COL_HQ)),
        pl.BlockSpec((L, w), col(COL_HF)),
        pl.BlockSpec((L, w), col(COL_HF + HG_QK)),
        pl.BlockSpec((L, w), col(COL_HI)),
        pl.BlockSpec((L, w), col(COL_HG)),
        pl.BlockSpec((DEPTH + 1, 2, w), lambda b, h: (0, 0, h)),
        pl.BlockSpec((1, w), lambda b, h: (0, h)),
    ]
    args = [proj, proj, proj, proj, proj, hg_lb, hg_nw]
    if has_init:
        in_specs.append(st_spec)
        args.append(s0)
    out_specs = [pl.BlockSpec((L, w), lambda b, h: (b, h))]
    out_shape = [jax.ShapeDtypeStruct((n_seq * L, HG_IV), F32)]
    if want_state:
        out_specs.append(st_spec)
        out_shape.append(jax.ShapeDtypeStruct((n_seq, 2, HG_HEADS, HG_K, HG_V), F32))
    return pl.pallas_call(
        functools.partial(_gla_kernel, L=L, layer=layer, has_init=has_init, want_state=want_state, hpb=hpb),
        grid=(n_seq, HG_HEADS // hpb),
        in_specs=in_specs, out_specs=out_specs, out_shape=out_shape,
        scratch_shapes=[pltpu.VMEM((L, w), F32), pltpu.VMEM((2, L, w), F32),
                        pltpu.VMEM((2, L, w), F32), pltpu.VMEM((L, w), F32)],
        compiler_params=_cparams(("arbitrary", "arbitrary")),
        name=f"gla_L{L}",
    )(*args)


def _ssm_out_kernel(xp_ref, xs_ref, yp_ref, ys_ref, op_ref, os_ref, z_ref, mod_ref, nw_ref, w_ref,
                    out_ref, *, tm):
    i = pl.program_id(0)
    g = _pick(i, tm, yp_ref, ys_ref) * _silu(z_ref[...])
    mix = (_bdot(_rms(g, nw_ref[...]), w_ref[0:SSD_INNER, :])
           + _bdot(_pick(i, tm, op_ref, os_ref), w_ref[SSD_INNER:, :]))
    out_ref[...] = _pick(i, tm, xp_ref, xs_ref) + _mod_vec(mod_ref, _mod_row(i, tm), 2) * mix


def _ssm_out(xp, xs, yp, ys, op, os_, proj, mod_l, nw, w, *, tm):
    return pl.pallas_call(
        functools.partial(_ssm_out_kernel, tm=tm),
        grid=(T // tm,),
        in_specs=_two_src_specs(tm) + _two_src_specs(tm, SSD_INNER) + _two_src_specs(tm, HG_IV) + [
            pl.BlockSpec((tm, SSD_INNER), lambda i: (i, COL_Z // SSD_INNER)),
            pl.BlockSpec((16, N_MOD * D), lambda i: (0, 0)),
            pl.BlockSpec((1, SSD_INNER), lambda i: (0, 0)),
            pl.BlockSpec((SSD_INNER + HG_IV, D), lambda i: (0, 0))],
        out_specs=pl.BlockSpec((tm, D), lambda i: (i, 0)),
        out_shape=jax.ShapeDtypeStruct((T, D), F32),
        compiler_params=_cparams(("arbitrary",)),
        name="ssm_out",
    )(xp, xs, yp, ys, op, os_, proj, mod_l, nw, w)


def _attn_out_kernel(x_ref, ap_ref, as_ref, mod_ref, w_ref, out_ref, *, tm):
    i = pl.program_id(0)
    mix = jnp.dot(_pick(i, tm, ap_ref, as_ref), w_ref[...], preferred_element_type=F32)
    out_ref[...] = x_ref[...] + _mod_vec(mod_ref, _mod_row(i, tm), 2) * mix


def _attn_out(x, ap, as_, mod_l, w, *, tm):
    return pl.pallas_call(
        functools.partial(_attn_out_kernel, tm=tm),
        grid=(T // tm,),
        in_specs=[pl.BlockSpec((tm, D), lambda i: (i, 0))] + _two_src_specs(tm) + [
            pl.BlockSpec((16, N_MOD * D), lambda i: (0, 0)),
            pl.BlockSpec((D, D), lambda i: (0, 0))],
        out_specs=pl.BlockSpec((tm, D), lambda i: (i, 0)),
        out_shape=jax.ShapeDtypeStruct((T, D), F32),
        compiler_params=_cparams(("arbitrary",)),
        name="attn_out",
    )(x, ap, as_, mod_l, w)


def _route_kernel(x_ref, mod_ref, nw_ref, rw_ref, rb_ref, h_ref, meta_ref, metat_ref, cnt_ref):
    i = pl.program_id(0)
    r = _mod_row(i, TB)
    h = _rms(x_ref[...], nw_ref[...]) * (1.0 + _mod_vec(mod_ref, r, 4)) + _mod_vec(mod_ref, r, 3)
    hb = h.astype(BF16)
    h_ref[...] = hb
    logits = jnp.dot(hb, rw_ref[...], preferred_element_type=F32) + rb_ref[...]
    lane = lax.broadcasted_iota(I32, (TB, LANES), 1)
    lane_f = lane.astype(F32)
    vals, hots, idxs = [], [], []
    for _ in range(TOP_K):
        m = jnp.max(logits, axis=-1, keepdims=True)
        idx = jnp.min(jnp.where(logits == m, lane_f, float(LANES)), axis=-1, keepdims=True)
        hot = lane_f == idx
        vals.append(m)
        idxs.append(idx)
        hots.append(hot)
        logits = jnp.where(hot, -jnp.inf, logits)
    es = [jnp.exp(v - vals[0]) for v in vals]
    den = es[0] + es[1] + es[2] + es[3]

    r_i = lax.broadcasted_iota(I32, (TB, TB), 0)
    c_i = lax.broadcasted_iota(I32, (TB, TB), 1)
    below = (c_i < r_i).astype(BF16)
    base = jnp.zeros((1, LANES), F32)
    ranks = []
    for k in range(TOP_K):
        hot_f = hots[k].astype(F32)
        before = jnp.dot(below, hots[k].astype(BF16), preferred_element_type=F32)
        ranks.append(base + before)
        base = base + jnp.sum(hot_f, axis=0, keepdims=True)
    units = jnp.floor((base + (SEG - 1)) * (1.0 / SEG))
    lr = lax.broadcasted_iota(I32, (LANES, LANES), 0)
    lc = lax.broadcasted_iota(I32, (LANES, LANES), 1)
    seg_start = SEG * jnp.dot(jnp.broadcast_to(units, (SUBLANES, LANES)).astype(BF16),
                              (lr < lc).astype(BF16), preferred_element_type=F32)[0:1]
    meta = jnp.zeros((TB, LANES), F32)
    for k in range(TOP_K):
        row = jnp.sum(hots[k].astype(F32) * (seg_start + ranks[k]), axis=-1, keepdims=True)
        meta = jnp.where(lane == k, idxs[k], meta)
        meta = jnp.where(lane == TOP_K + k, es[k] / den, meta)
        meta = jnp.where(lane == 2 * TOP_K + k, row, meta)
    meta_ref[...] = meta
    metat_ref[...] = meta.T
    cnt_ref[...] = base


def _route(x, mod_l, nw, rw, rb):
    return pl.pallas_call(
        _route_kernel,
        grid=(NB,),
        in_specs=[pl.BlockSpec((TB, D), lambda i: (i, 0)),
                  pl.BlockSpec((16, N_MOD * D), lambda i: (0, 0)),
                  pl.BlockSpec((1, D), lambda i: (0, 0)),
                  pl.BlockSpec((D, LANES), lambda i: (0, 0)),
                  pl.BlockSpec((1, LANES), lambda i: (0, 0))],
        out_specs=[pl.BlockSpec((TB, D), lambda i: (i, 0)),
                   pl.BlockSpec((TB, LANES), lambda i: (i, 0)),
                   pl.BlockSpec((LANES, TB), lambda i: (0, i)),
                   pl.BlockSpec((None, 1, LANES), lambda i: (i, 0, 0))],
        out_shape=[jax.ShapeDtypeStruct((T, D), BF16),
                   jax.ShapeDtypeStruct((T, LANES), F32),
                   jax.ShapeDtypeStruct((LANES, T), F32),
                   jax.ShapeDtypeStruct((NB, 1, LANES), F32)],
        compiler_params=_cparams(("arbitrary",)),
        name="moe_route",
    )(x, mod_l, nw, rw, rb)


def _copy(src, dst, sem):
    return pltpu.make_async_copy(src, dst, sem)


def _segment_copies(n, src, dst, sizes, make, wait):
    src = jnp.asarray(src, I32)
    dst = jnp.asarray(dst, I32)
    for size in sizes:
        take = n & size

        @pl.when(take != 0)
        def _(src=src, dst=dst, size=size):
            cp = make(pl.multiple_of(src, SEG), pl.multiple_of(dst, SEG), size)
            if wait:
                cp.wait()
            else:
                cp.start()

        src = src + take
        dst = dst + take


def _sort_push_kernel(tab_ref, tabp_ref, ztab_ref, nv_ref, metat_ref, h_ref, xs_hbm, buf, zb, sem, zsem):
    b = pl.program_id(0)
    slot = b & 1

    def zero_copies(wait):
        def per_expert(e, carry):
            _segment_copies(ztab_ref[1, e], 0, ztab_ref[0, e], PAD_SIZES,
                            lambda s, d, size: _copy(zb.at[pl.ds(0, size)], xs_hbm.at[pl.ds(d, size)], zsem), wait)
            return carry

        lax.fori_loop(0, N_EXP, per_expert, 0)

        def per_tile(j, carry):
            cp = _copy(zb, xs_hbm.at[pl.ds(pl.multiple_of(j * TM_E, TM_E), TM_E)], zsem)
            if wait:
                cp.wait()
            else:
                cp.start()
            return carry

        lax.fori_loop(nv_ref[0], N_ETILES, per_tile, 0)

    @pl.when(b == 0)
    def _():
        zb[...] = jnp.zeros_like(zb)
        zero_copies(False)

    ch = 256
    for c in range(RB // ch):
        rows = (lax.broadcasted_iota(I32, (ch, TB), 0) + c * ch).astype(F32)
        hit = rows == metat_ref[2 * TOP_K:2 * TOP_K + 1, :]
        for k in range(1, TOP_K):
            hit = hit | (rows == metat_ref[2 * TOP_K + k:2 * TOP_K + k + 1, :])
        buf[slot, c * ch:(c + 1) * ch, :] = jnp.dot(hit.astype(BF16), h_ref[...], preferred_element_type=F32)

    def push(t_ref, s, wait):
        def per_expert(e, carry):
            _segment_copies(
                t_ref[0, e], t_ref[1, e], t_ref[2, e], SEG_SIZES,
                lambda so, do, size: _copy(buf.at[s, pl.ds(so, size)], xs_hbm.at[pl.ds(do, size)], sem.at[s]), wait)
            return carry

        lax.fori_loop(0, N_EXP, per_expert, 0)

    def drain(t_ref, s):
        _segment_copies(t_ref[1, N_EXP - 1] + t_ref[0, N_EXP - 1], 0, 0, BUF_SIZES,
                        lambda so, do, size: _copy(buf.at[s, pl.ds(0, size)], xs_hbm.at[pl.ds(0, size)], sem.at[s]),
                        True)

    @pl.when(b > 0)
    def _():
        drain(tabp_ref, 1 - slot)

    push(tab_ref, slot, False)

    @pl.when(b == NB - 1)
    def _():
        drain(tab_ref, slot)

    @pl.when(b == 0)
    def _():
        zero_copies(True)


def _sort_push(tab, ztab, n_valid, metat, h):
    return pl.pallas_call(
        _sort_push_kernel,
        grid=(NB,),
        in_specs=[pl.BlockSpec((None, 3, N_EXP), lambda b: (b, 0, 0), memory_space=pltpu.SMEM),
                  pl.BlockSpec((None, 3, N_EXP), lambda b: (jnp.maximum(b - 1, 0), 0, 0),
                               memory_space=pltpu.SMEM),
                  pl.BlockSpec(memory_space=pltpu.SMEM),
                  pl.BlockSpec(memory_space=pltpu.SMEM),
                  pl.BlockSpec((LANES, TB), lambda b: (0, b)),
                  pl.BlockSpec((TB, D), lambda b: (b, 0))],
        out_specs=pl.BlockSpec(memory_space=pl.ANY),
        out_shape=jax.ShapeDtypeStruct((P_ROWS, D), F32),
        scratch_shapes=[pltpu.VMEM((2, RB, D), F32), pltpu.VMEM((TM_E, D), F32),
                        pltpu.SemaphoreType.DMA((2,)), pltpu.SemaphoreType.DMA(())],
        compiler_params=_cparams(("arbitrary",)),
        name="moe_sort_push",
    )(tab, tab, ztab, n_valid, metat, h)


def _expert_kernel(te_ref, tr_ref, nv_ref, x_ref, w1_ref, b1_ref, w2_ref, b2_ref, y_ref, w1_s, w2_s):
    j = pl.program_id(0)
    valid = j < nv_ref[0]
    fresh = jnp.logical_or(j == 0, te_ref[j] != te_ref[jnp.maximum(j - 1, 0)])

    @pl.when(jnp.logical_and(valid, fresh))
    def _():
        w1_s[...] = w1_ref[...].astype(BF16)
        w2_s[...] = w2_ref[...].astype(BF16)

    @pl.when(valid)
    def _():
        rows = lax.broadcasted_iota(I32, (TM_E, 1), 0)
        x = jnp.where(rows < tr_ref[j], x_ref[...], 0.0).astype(BF16)
        gu = jnp.dot(x, w1_s[...], preferred_element_type=F32) + b1_ref[...]
        gate = jnp.minimum(gu[:, :FF], SWIGLU_LIMIT)
        up = jnp.clip(gu[:, FF:], -SWIGLU_LIMIT, SWIGLU_LIMIT)
        act = (up + 1.0) * gate * jax.nn.sigmoid(SWIGLU_ALPHA * gate)
        y_ref[...] = jnp.dot(act.astype(BF16), w2_s[...], preferred_element_type=F32) + b2_ref[...]

    @pl.when(jnp.logical_not(valid))
    def _():
        y_ref[...] = jnp.zeros_like(y_ref)


def _experts(layer, tile_exp, tile_rows, n_valid, xs, w1, b1, w2, b2):
    def wmap(j, te, tr, nv):
        return (layer, te[j], 0, 0)

    return pl.pallas_call(
        _expert_kernel,
        grid_spec=pltpu.PrefetchScalarGridSpec(
            num_scalar_prefetch=3,
            grid=(N_ETILES,),
            in_specs=[pl.BlockSpec((TM_E, D), lambda j, te, tr, nv: (j, 0)),
                      pl.BlockSpec((None, None, D, 2 * FF), wmap),
                      pl.BlockSpec((None, None, 1, 2 * FF), wmap),
                      pl.BlockSpec((None, None, FF, D), wmap),
                      pl.BlockSpec((None, None, 1, D), wmap)],
            out_specs=pl.BlockSpec((TM_E, D), lambda j, te, tr, nv: (j, 0)),
            scratch_shapes=[pltpu.VMEM((D, 2 * FF), BF16), pltpu.VMEM((FF, D), BF16)]),
        out_shape=jax.ShapeDtypeStruct((P_ROWS, D), F32),
        compiler_params=_cparams(("arbitrary",)),
        name="moe_experts",
    )(tile_exp, tile_rows, n_valid, xs, w1, b1.reshape(DEPTH, N_EXP, 1, 2 * FF), w2,
      b2.reshape(DEPTH, N_EXP, 1, D))


def _combine_kernel(tab_ref, tabn_ref, y_hbm, x_ref, meta_ref, mod_ref, out_ref, buf, sem):
    b = pl.program_id(0)
    slot = b & 1

    def pull(t_ref, s, wait):
        def per_expert(e, carry):
            _segment_copies(
                t_ref[0, e], t_ref[2, e], t_ref[1, e], SEG_SIZES,
                lambda so, do, size: _copy(y_hbm.at[pl.ds(so, size)], buf.at[s, pl.ds(do, size)], sem.at[s]), wait)
            return carry

        lax.fori_loop(0, N_EXP, per_expert, 0)

    @pl.when(b == 0)
    def _():
        pull(tab_ref, slot, False)

    @pl.when(b + 1 < NB)
    def _():
        pull(tabn_ref, 1 - slot, False)

    used = tab_ref[1, N_EXP - 1] + tab_ref[0, N_EXP - 1]
    _segment_copies(used, 0, 0, BUF_SIZES,
                    lambda so, do, size: _copy(y_hbm.at[pl.ds(0, size)], buf.at[slot, pl.ds(0, size)], sem.at[slot]),
                    True)

    meta = meta_ref[...]
    ch = 256
    acc = jnp.zeros((TB, D), F32)
    for c in range(RB // ch):
        cols = (lax.broadcasted_iota(I32, (TB, ch), 1) + c * ch).astype(F32)
        g = jnp.zeros((TB, ch), F32)
        for k in range(TOP_K):
            g = g + jnp.where(cols == meta[:, 2 * TOP_K + k:2 * TOP_K + k + 1],
                              meta[:, TOP_K + k:TOP_K + k + 1], 0.0)
        rows = lax.broadcasted_iota(I32, (ch, 1), 0) + c * ch
        y = jnp.where(rows < used, buf[slot, c * ch:(c + 1) * ch, :], 0.0)
        acc = acc + _bdot(g, y)
    out_ref[...] = x_ref[...] + _mod_vec(mod_ref, _mod_row(b, TB), 5) * acc


def _combine(tab, y, x, meta, mod_l):
    return pl.pallas_call(
        _combine_kernel,
        grid=(NB,),
        in_specs=[pl.BlockSpec((None, 3, N_EXP), lambda b: (b, 0, 0), memory_space=pltpu.SMEM),
                  pl.BlockSpec((None, 3, N_EXP), lambda b: (jnp.minimum(b + 1, NB - 1), 0, 0),
                               memory_space=pltpu.SMEM),
                  pl.BlockSpec(memory_space=pl.ANY),
                  pl.BlockSpec((TB, D), lambda b: (b, 0)),
                  pl.BlockSpec((TB, LANES), lambda b: (b, 0)),
                  pl.BlockSpec((16, N_MOD * D), lambda b: (0, 0))],
        out_specs=pl.BlockSpec((TB, D), lambda b: (b, 0)),
        out_shape=jax.ShapeDtypeStruct((T, D), F32),
        scratch_shapes=[pltpu.VMEM((2, RB, D), F32), pltpu.SemaphoreType.DMA((2,))],
        compiler_params=_cparams(("arbitrary",)),
        name="moe_combine",
    )(tab, tab, y, x, meta, mod_l)


def _moe_layer(layer, x, mod_l, nw, rw, rb, w1, b1, w2, b2):
    rw_p = jnp.zeros((D, LANES), BF16).at[:, :N_EXP].set(rw.astype(BF16))
    rb_p = jnp.full((1, LANES), -jnp.inf, F32).at[0, :N_EXP].set(rb)
    h, meta, metat, cnt = _route(x, mod_l, nw, rw_p, rb_p)

    counts = cnt[:, 0, :N_EXP].astype(I32)
    n_seg = (counts + SEG - 1) // SEG * SEG
    seg_start = jnp.cumsum(n_seg, axis=1) - n_seg
    tot = jnp.sum(n_seg, axis=0)
    padded = (tot + TM_E - 1) // TM_E * TM_E
    ends = jnp.cumsum(padded)
    offs = ends - padded
    glob = offs[None, :] + jnp.cumsum(n_seg, axis=0) - n_seg
    tab = jnp.stack([n_seg, seg_start, glob], axis=1)
    ztab = jnp.stack([offs + tot, padded - tot], axis=0)
    n_valid = ends[-1] // TM_E
    starts = jnp.arange(N_ETILES, dtype=I32) * TM_E
    te = jnp.minimum(jnp.sum((ends[None, :] <= starts[:, None]).astype(I32), axis=1), N_EXP - 1)
    last = jnp.sum((ends[:-1] < ends[-1]).astype(I32))
    te = jnp.where(starts < ends[-1], te, last)
    tr = jnp.clip(offs[te] + tot[te] - starts, 0, TM_E)
    nv = n_valid.reshape(1)

    xs = _sort_push(tab, ztab, nv, metat, h)
    y = _experts(layer, te, tr, nv, xs, w1, b1, w2, b2)
    return _combine(tab, y, x, meta, mod_l)


def _head_sel():
    g = np.zeros((D, LANES), np.float32)
    for c in range(D):
        g[c, c // HEAD_DIM] = 1.0
    return jnp.asarray(g, BF16)


def _head_rms(x, sel, sel_t, w):
    n = x.shape[1]
    hi, lo = _split_hi_lo(x * x)
    ss = (jnp.dot(hi, sel[:n], preferred_element_type=F32)
          + jnp.dot(lo, sel[:n], preferred_element_type=F32))
    inv = lax.rsqrt(ss * (1.0 / HEAD_DIM) + EPS)
    hi, lo = _split_hi_lo(inv)
    inv_e = (jnp.dot(hi, sel_t[:, :n], preferred_element_type=F32)
             + jnp.dot(lo, sel_t[:, :n], preferred_element_type=F32))
    return x * inv_e * w


def _rope(x, cos, sin):
    n = x.shape[1]
    lane = lax.broadcasted_iota(I32, x.shape, 1)
    first = (lane % (HEAD_DIM // 2)) < (HEAD_DIM // 4)
    rot = jnp.where(first, -pltpu.roll(x, n - HEAD_DIM // 4, axis=1), pltpu.roll(x, HEAD_DIM // 4, axis=1))
    return x * cos + rot * sin


def _qkv_kernel(x_ref, mod_ref, nw_ref, w_ref, sel_ref, selt_ref, qw_ref, kw_ref, cos_ref, sin_ref,
                q_ref, k_ref, v_ref, kf_ref, vf_ref, *, tm):
    i = pl.program_id(0)
    r = _mod_row(i, tm)
    h = _rms(x_ref[...], nw_ref[...]) * (1.0 + _mod_vec(mod_ref, r, 1)) + _mod_vec(mod_ref, r, 0)
    qkv = _bdot(h, w_ref[...])
    nq, nk = ATT_HEADS * HEAD_DIM, KV_HEADS * HEAD_DIM
    q = _head_rms(qkv[:, :nq], sel_ref[...], selt_ref[...], qw_ref[...])
    k = _head_rms(qkv[:, nq:nq + nk], sel_ref[...], selt_ref[...], kw_ref[...])
    v = qkv[:, nq + nk:]
    kf_ref[...] = k
    vf_ref[...] = v
    latent = i * tm >= NP_TOK
    cos = jnp.where(latent, cos_ref[...], 1.0)
    sin = jnp.where(latent, sin_ref[...], 0.0)
    q = _rope(q, cos, sin) * (HEAD_DIM ** -0.5)
    k = _rope(k, cos[:, :nk], sin[:, :nk])
    q_ref[...] = q.T.astype(BF16)
    v_ref[...] = v.T.astype(BF16)
    for hd in range(KV_HEADS):
        k_ref[hd] = k[:, hd * HEAD_DIM:(hd + 1) * HEAD_DIM].astype(BF16)


def _qkv(x, mod_l, nw, w, qw, kw, cos, sin, *, tm):
    nk = KV_HEADS * HEAD_DIM
    sel = _head_sel()
    nblk = DEC_SEQ // tm

    def cmap(i):
        return (jnp.maximum(i - NP_TOK // tm, 0) % nblk, 0)

    return pl.pallas_call(
        functools.partial(_qkv_kernel, tm=tm),
        grid=(T // tm,),
        in_specs=[pl.BlockSpec((tm, D), lambda i: (i, 0)),
                  pl.BlockSpec((16, N_MOD * D), lambda i: (0, 0)),
                  pl.BlockSpec((1, D), lambda i: (0, 0)),
                  pl.BlockSpec((D, D + 2 * nk), lambda i: (0, 0)),
                  pl.BlockSpec((D, LANES), lambda i: (0, 0)),
                  pl.BlockSpec((LANES, D), lambda i: (0, 0)),
                  pl.BlockSpec((1, D), lambda i: (0, 0)),
                  pl.BlockSpec((1, nk), lambda i: (0, 0)),
                  pl.BlockSpec((tm, D), cmap),
                  pl.BlockSpec((tm, D), cmap)],
        out_specs=[pl.BlockSpec((D, tm), lambda i: (0, i)),
                   pl.BlockSpec((KV_HEADS, tm, HEAD_DIM), lambda i: (0, i, 0)),
                   pl.BlockSpec((nk, tm), lambda i: (0, i)),
                   pl.BlockSpec((tm, nk), lambda i: (i, 0)),
                   pl.BlockSpec((tm, nk), lambda i: (i, 0))],
        out_shape=[jax.ShapeDtypeStruct((D, T), BF16),
                   jax.ShapeDtypeStruct((KV_HEADS, T, HEAD_DIM), BF16),
                   jax.ShapeDtypeStruct((nk, T), BF16),
                   jax.ShapeDtypeStruct((T, nk), F32),
                   jax.ShapeDtypeStruct((T, nk), F32)],
        compiler_params=_cparams(("arbitrary",)),
        name="attn_qkv",
    )(x, mod_l, nw, w, sel, sel.T, qw, kw, cos, sin)


def _attn_kernel(*refs, has_cache):
    if has_cache:
        q_ref, k_ref, v_ref, kc_ref, vc_ref, o_ref = refs
    else:
        q_ref, k_ref, v_ref, o_ref = refs
    k = k_ref[0]
    vt = v_ref[...]
    heads = range(Q_PER_KV)
    qts = [q_ref[j * HEAD_DIM:(j + 1) * HEAD_DIM, :] for j in heads]
    ss = [jnp.dot(k, qt, preferred_element_type=F32) for qt in qts]
    ms = [jnp.max(s, axis=0, keepdims=True) for s in ss]
    if has_cache:
        scs = [jnp.dot(kc_ref[0, 0], qt, preferred_element_type=F32) for qt in qts]
        ms = [jnp.maximum(m, jnp.max(sc, axis=0, keepdims=True)) for m, sc in zip(ms, scs)]
    ps = [jnp.exp(s - m) for s, m in zip(ss, ms)]
    dens = [jnp.sum(p, axis=0, keepdims=True) for p in ps]
    os_ = [jnp.dot(vt, p.astype(BF16), preferred_element_type=F32) for p in ps]
    if has_cache:
        pcs = [jnp.exp(sc - m) for sc, m in zip(scs, ms)]
        dens = [d + jnp.sum(pc, axis=0, keepdims=True) for d, pc in zip(dens, pcs)]
        os_ = [o + jnp.dot(vc_ref[0, 0], pc.astype(BF16), preferred_element_type=F32) for o, pc in zip(os_, pcs)]
    o_ref[...] = jnp.concatenate([o / d for o, d in zip(os_, dens)], axis=0).T.astype(BF16)


def _attn_call(q, k, v, *, L, n_seq, row0, tq, cache):
    rb_q = row0 // tq
    rb_k = row0 // L
    nq = L // tq
    has_cache = cache is not None
    in_specs = [pl.BlockSpec((Q_PER_KV * HEAD_DIM, tq), lambda b, g, t: (g, rb_q + b * nq + t)),
                pl.BlockSpec((1, L, HEAD_DIM), lambda b, g, t: (g, rb_k + b, 0)),
                pl.BlockSpec((HEAD_DIM, L), lambda b, g, t: (g, rb_k + b))]
    args = [q, k, v]
    if has_cache:
        in_specs += [pl.BlockSpec((1, 1, PAST, HEAD_DIM), lambda b, g, t: (b, g, 0, 0)),
                     pl.BlockSpec((1, 1, HEAD_DIM, PAST), lambda b, g, t: (b, g, 0, 0))]
        args += list(cache)
    gw = Q_PER_KV * HEAD_DIM
    return pl.pallas_call(
        functools.partial(_attn_kernel, has_cache=has_cache),
        grid=(n_seq, KV_HEADS, nq),
        in_specs=in_specs,
        out_specs=pl.BlockSpec((tq, gw), lambda b, g, t: (b * nq + t, g)),
        out_shape=jax.ShapeDtypeStruct((n_seq * L, D), BF16),
        compiler_params=_cparams(("arbitrary", "arbitrary", "arbitrary")),
        name=f"attn_L{L}",
    )(*args)


def _rope_tables():
    rows = DEC_SEQ // GRID_W
    row = jnp.repeat(jnp.arange(rows), GRID_W)
    colp = jnp.tile(jnp.arange(GRID_W), rows)
    pos = jnp.stack([row, colp], axis=-1).astype(F32)
    half = HEAD_DIM // 2
    inv_freq = ROPE_THETA ** (-jnp.arange(0, half, 2, dtype=F32) / half)
    ang = pos[:, :, None] * inv_freq
    ang = jnp.concatenate([ang, ang], axis=-1).reshape(DEC_SEQ, HEAD_DIM)
    ang = jnp.tile(ang, (1, ATT_HEADS))
    return jnp.cos(ang), jnp.sin(ang)


def _final_kernel(x_ref, w_ref, o_ref):
    o_ref[...] = _rms(x_ref[...], w_ref[...])


def _final_norm(x, w, *, row0, n_rows, tm):
    rb = row0 // tm
    return pl.pallas_call(
        _final_kernel,
        grid=(n_rows // tm,),
        in_specs=[pl.BlockSpec((tm, D), lambda i: (rb + i, 0)),
                  pl.BlockSpec((1, D), lambda i: (0, 0))],
        out_specs=pl.BlockSpec((tm, D), lambda i: (i, 0)),
        out_shape=jax.ShapeDtypeStruct((n_rows, D), F32),
        compiler_params=_cparams(("arbitrary",)),
        name="final_norm",
    )(x, w)


def _ssd_params(conv_w, conv_b, dt_bias, a_log, d_skip):
    def dt_layout(p):
        g = p.reshape(2, SSD_G, HPG).transpose(1, 0, 2).reshape(SSD_G, 2 * HPG)
        return jnp.zeros((SSD_G, LANES), F32).at[:, :2 * HPG].set(g).reshape(1, SSD_G * LANES)

    ni = SSD_INNER
    nb = SSD_G * SSD_N
    return dict(
        cwx=conv_w[:, :ni], cwb=conv_w[:, ni:ni + nb], cwc=conv_w[:, ni + nb:],
        cbx=conv_b[None, :ni], cbb=conv_b[None, ni:ni + nb], cbc=conv_b[None, ni + nb:],
        dtb=dt_layout(dt_bias), alog=dt_layout(a_log),
        dcol=jnp.repeat(d_skip, SSD_P)[None, :], rsel=_ssd_rsel())


def _ssm_in_weights(w):
    ni, cd = SSD_INNER, SSD_INNER + 2 * SSD_G * SSD_N
    o_xbc, o_dt = ni, ni + cd
    o_hq = o_dt + 2 * SSD_HEADS
    o_hf, o_hi = o_hq + HG_QK, o_hq + 3 * HG_QK
    o_hg = o_hi + HG_IV
    main = jnp.concatenate([w[:, :ni], w[:, o_hq:o_hf], w[:, o_hi:o_hg], w[:, o_hg:o_hg + HG_IV],
                            w[:, o_hf:o_hi], w[:, o_xbc:o_dt]], axis=1).astype(BF16)
    wdt = w[:, o_dt:o_hq].reshape(D, 2, SSD_G, HPG).transpose(0, 2, 1, 3).reshape(D, SSD_G, 2 * HPG)
    wdt = jnp.zeros((D, SSD_G, LANES), F32).at[:, :, :2 * HPG].set(wdt).reshape(D, SSD_G * LANES)
    return main, wdt.astype(BF16)


def kernel(x_prompt, x_sample, c, c_ctx, state_ssd, state_hgrn, cache_k, cache_v, mod_w, mod_b, norm1_w,
           norm2_w, ssm_in_w, ssd_conv_w, ssd_conv_b, ssd_dt_bias, ssd_a_log, ssd_d, ssd_norm_w, hg_lb,
           hg_norm_w, ssm_out_w, attn_in_w, q_norm_w, k_norm_w, attn_out_w, router_w, router_b, exp_w1,
           exp_b1, exp_w2, exp_b2, final_norm_w):
    xp = x_prompt.reshape(NP_TOK, D)
    xs = x_sample.reshape(NS_TOK, D)
    cvec = jnp.zeros((16, D), F32).at[0].set(c_ctx).at[1:1 + DEC_BATCH].set(c)
    mod = _mod_table(cvec, mod_w, mod_b)

    w_main, w_dt = _ssm_in_weights(ssm_in_w[0])
    proj, dtp = _inproj(xp, xs, mod[0], norm1_w[0][None], w_main, w_dt, tm=1024, tn=1280)
    prm = _ssd_params(ssd_conv_w[0], ssd_conv_b[0], ssd_dt_bias[0], ssd_a_log[0], ssd_d[0])
    yp, st_ssd = _ssd_call(proj, dtp, prm, L=SEQ, n_seq=BATCH, row0=0, s0=None, want_state=True)
    s0 = state_ssd[:, 0].reshape(DEC_BATCH, 2, SSD_G, GW, SSD_N)
    (ys,) = _ssd_call(proj, dtp, prm, L=DEC_SEQ, n_seq=DEC_BATCH, row0=NP_TOK, s0=s0, want_state=False)
    hg_nw = hg_norm_w[0][None]
    op, st_hg = _gla_call(proj, hg_lb, hg_nw, layer=0, L=SEQ, n_seq=BATCH, row0=0, s0=None, want_state=True,
                          hpb=4)
    (os_,) = _gla_call(proj, hg_lb, hg_nw, layer=0, L=DEC_SEQ, n_seq=DEC_BATCH, row0=NP_TOK,
                       s0=state_hgrn[:, 0], want_state=False, hpb=2)
    x = _ssm_out(xp, xs, yp, ys, op, os_, proj, mod[0], ssd_norm_w[0][None], ssm_out_w[0].astype(BF16), tm=512)
    x = _moe_layer(0, x, mod[0], norm2_w[0][None], router_w[0], router_b[0], exp_w1, exp_b1, exp_w2, exp_b2)

    cos, sin = _rope_tables()
    q, k, v, kf, vf = _qkv(x, mod[1], norm1_w[1][None], attn_in_w[0].astype(BF16),
                           jnp.tile(q_norm_w[0], ATT_HEADS)[None], jnp.tile(k_norm_w[0], KV_HEADS)[None],
                           cos, sin, tm=256)
    ap = _attn_call(q, k, v, L=SEQ, n_seq=BATCH, row0=0, tq=SEQ, cache=None)
    ck = cache_k[:, 0].transpose(0, 2, 1, 3).astype(BF16)
    cv = cache_v[:, 0].transpose(0, 2, 3, 1).astype(BF16)
    as_ = _attn_call(q, k, v, L=DEC_SEQ, n_seq=DEC_BATCH, row0=NP_TOK, tq=256, cache=(ck, cv))
    x = _attn_out(x, ap, as_, mod[1], attn_out_w[0].astype(BF16), tm=512)
    x = _moe_layer(1, x, mod[1], norm2_w[1][None], router_w[1], router_b[1], exp_w1, exp_b1, exp_w2, exp_b2)

    y_prompt = _final_norm(x, final_norm_w[None], row0=0, n_rows=NP_TOK, tm=512).reshape(BATCH, SEQ, D)
    y_sample = _final_norm(x, final_norm_w[None], row0=NP_TOK, n_rows=NS_TOK, tm=512).reshape(
        DEC_BATCH, DEC_SEQ, D)
    new_ssd = st_ssd.reshape(BATCH, 1, 2, SSD_HEADS, SSD_P, SSD_N)
    new_hg = st_hg.reshape(BATCH, 1, 2, HG_HEADS, HG_K, HG_V)
    new_k = kf[:NP_TOK].reshape(BATCH, 1, SEQ, KV_HEADS, HEAD_DIM)
    new_v = vf[:NP_TOK].reshape(BATCH, 1, SEQ, KV_HEADS, HEAD_DIM)
    return (y_prompt, y_sample, new_ssd, new_hg, new_k, new_v)
```

```python
import functools

import jax
import jax.numpy as jnp
import numpy as np
from jax import lax
from jax.experimental import pallas as pl
from jax.experimental.pallas import tpu as pltpu

F32 = jnp.float32
BF16 = jnp.bfloat16
I32 = jnp.int32

D = 1024
BATCH, SEQ = 32, 256
DEC_BATCH, DEC_SEQ = 8, 1024
PAST = 256
NP_TOK = BATCH * SEQ
NS_TOK = DEC_BATCH * DEC_SEQ
T = NP_TOK + NS_TOK
DEPTH = 2
N_MOD = 6
EPS = 1e-6
GRID_W = 64
ROPE_THETA = 10000.0

SSD_HEADS, SSD_P, SSD_N, SSD_G = 16, 64, 128, 2
SSD_INNER = SSD_HEADS * SSD_P
SSD_CONV = 5
HPG = SSD_HEADS // SSD_G
GW = HPG * SSD_P
HG_HEADS, HG_K, HG_V = 8, 128, 128
HG_QK = HG_HEADS * HG_K
HG_IV = HG_HEADS * HG_V
ATT_HEADS, KV_HEADS, HEAD_DIM = 16, 4, 64
Q_PER_KV = ATT_HEADS // KV_HEADS
N_EXP, TOP_K, FF = 32, 4, 1024
SWIGLU_ALPHA, SWIGLU_LIMIT = 1.702, 7.0

LANES = 128
SUBLANES = 8
VMEM_LIMIT = 56 * 1024 * 1024

COL_Z, COL_HQ, COL_HI, COL_HG, COL_HF, COL_XBC = 0, 1024, 2048, 3072, 4096, 6144
PROJ_W = 7680

SSD_TC = 256
GLA_TC = 64
GLA_GROUP = 256

TB = 512
NB = T // TB
SEG = SUBLANES
RB = TB * TOP_K + N_EXP * SEG
TM_E = 512
SEG_SIZES = tuple(SEG << i for i in range((TB // SEG).bit_length() - 1, -1, -1))
PAD_SIZES = tuple(s for s in SEG_SIZES if s < TM_E)
BUF_SIZES = tuple(SEG << i for i in range((RB // SEG).bit_length() - 1, -1, -1))
P_ROWS_MAX = T * TOP_K + NB * N_EXP * (SEG - 1) + N_EXP * (TM_E - SEG)
N_ETILES = -(-P_ROWS_MAX // TM_E)
P_ROWS = N_ETILES * TM_E


def _cparams(sem):
    return pltpu.CompilerParams(dimension_semantics=sem, vmem_limit_bytes=VMEM_LIMIT)


def _silu(x):
    return x * jax.nn.sigmoid(x)


def _bdot(a, b):
    return jnp.dot(a.astype(BF16), b.astype(BF16), preferred_element_type=F32)


def _bdot_nt(a, b):
    return lax.dot_general(a.astype(BF16), b.astype(BF16), (((1,), (1,)), ((), ())),
                           preferred_element_type=F32)


def _mod_row(i, tm):
    start = i * tm
    return jnp.where(start < NP_TOK, 0, 1 + (start - NP_TOK) // DEC_SEQ)


def _mod_vec(mod_ref, r, k):
    return mod_ref[pl.ds(r, 1), k * D:(k + 1) * D]


def _rms(x, w):
    return x * lax.rsqrt(jnp.mean(x * x, axis=-1, keepdims=True) + EPS) * w


def _two_src_specs(tm, width=D):
    npt = NP_TOK // tm
    return [pl.BlockSpec((tm, width), lambda i, *_: (jnp.minimum(i, npt - 1), 0)),
            pl.BlockSpec((tm, width), lambda i, *_: (jnp.maximum(i - npt, 0), 0))]


def _pick(i, tm, p_ref, s_ref):
    return jnp.where(i * tm < NP_TOK, p_ref[...], s_ref[...])


def _mod_kernel(c_ref, w_ref, b_ref, o_ref):
    o_ref[0] = _bdot(_silu(c_ref[...]), w_ref[0]) + b_ref[0]


def _mod_table(cvec, mod_w, mod_b):
    tn = 1536
    return pl.pallas_call(
        _mod_kernel,
        grid=(DEPTH, N_MOD * D // tn),
        in_specs=[pl.BlockSpec((16, D), lambda l, n: (0, 0)),
                  pl.BlockSpec((1, D, tn), lambda l, n: (l, 0, n)),
                  pl.BlockSpec((1, 1, tn), lambda l, n: (l, 0, n))],
        out_specs=pl.BlockSpec((1, 16, tn), lambda l, n: (l, 0, n)),
        out_shape=jax.ShapeDtypeStruct((DEPTH, 16, N_MOD * D), F32),
        compiler_params=_cparams(("arbitrary", "arbitrary")),
        name="mod_table",
    )(cvec, mod_w, mod_b.reshape(DEPTH, 1, N_MOD * D))


def _inproj_kernel(xp_ref, xs_ref, mod_ref, nw_ref, w_ref, wx_ref, o_ref, ox_ref, h_ref, *, tm):
    i = pl.program_id(0)

    @pl.when(pl.program_id(1) == 0)
    def _():
        x = _pick(i, tm, xp_ref, xs_ref)
        r = _mod_row(i, tm)
        h = _rms(x, nw_ref[...]) * (1.0 + _mod_vec(mod_ref, r, 1)) + _mod_vec(mod_ref, r, 0)
        hb = h.astype(BF16)
        h_ref[...] = hb
        ox_ref[...] = jnp.dot(hb, wx_ref[...], preferred_element_type=F32)

    o_ref[...] = jnp.dot(h_ref[...], w_ref[...], preferred_element_type=F32)


def _inproj(xp, xs, mod_l, nw, w, wx, *, tm, tn):
    n_out, nx = w.shape[1], wx.shape[1]
    return pl.pallas_call(
        functools.partial(_inproj_kernel, tm=tm),
        grid=(T // tm, n_out // tn),
        in_specs=_two_src_specs(tm) + [
            pl.BlockSpec((16, N_MOD * D), lambda i, n: (0, 0)),
            pl.BlockSpec((1, D), lambda i, n: (0, 0)),
            pl.BlockSpec((D, tn), lambda i, n: (0, n)),
            pl.BlockSpec((D, nx), lambda i, n: (0, 0))],
        out_specs=[pl.BlockSpec((tm, tn), lambda i, n: (i, n)),
                   pl.BlockSpec((tm, nx), lambda i, n: (i, 0))],
        out_shape=[jax.ShapeDtypeStruct((T, n_out), F32), jax.ShapeDtypeStruct((T, nx), F32)],
        scratch_shapes=[pltpu.VMEM((tm, D), BF16)],
        compiler_params=_cparams(("arbitrary", "arbitrary")),
        name="inproj",
    )(xp, xs, mod_l, nw, w, wx)


def _tri(n, upper):
    r = lax.broadcasted_iota(I32, (n, n), 0)
    c = lax.broadcasted_iota(I32, (n, n), 1)
    return (c >= r) if upper else (c <= r)


def _conv_silu(x, w, b, L):
    row = lax.broadcasted_iota(I32, (L, 1), 0)
    acc = x * w[2:3] + b
    for k in (0, 1, 3, 4):
        off = k - 2
        shifted = pltpu.roll(x, (-off) % L, axis=0)
        ok = (row + off >= 0) & (row + off < L)
        acc = acc + jnp.where(ok, shifted, 0.0) * w[k:k + 1]
    return _silu(acc)


def _split_hi_lo(x):
    hi = x.astype(BF16)
    return hi, (x - hi.astype(F32)).astype(BF16)


def _ssd_kernel(*refs, L, has_init, want_state):
    (x_ref, b_ref, c_ref, dt_ref, cwx_ref, cwb_ref, cwc_ref, cbx_ref, cbb_ref, cbc_ref,
     dtb_ref, alog_ref, dcol_ref, rsel_ref) = refs[:14]
    pos = 14
    s0_ref = None
    if has_init:
        s0_ref = refs[pos]
        pos += 1
    y_ref = refs[pos]
    pos += 1
    st_ref = None
    if want_state:
        st_ref = refs[pos]
        pos += 1
    xs_s, bs_s, cs_s, y_s = refs[pos:pos + 4]

    tc = SSD_TC
    nc = L // tc
    xs_s[...] = _conv_silu(x_ref[...], cwx_ref[...], cbx_ref[...], L)
    bs_s[...] = _conv_silu(b_ref[...], cwb_ref[...], cbb_ref[...], L)
    cs_s[...] = _conv_silu(c_ref[...], cwc_ref[...], cbc_ref[...], L)
    y_s[...] = xs_s[...] * dcol_ref[...]

    z = dt_ref[...] + dtb_ref[...]
    dt_all = jnp.maximum(z, 0.0) + jnp.log(1.0 + jnp.exp(-jnp.abs(z)))
    a_row = -jnp.exp(alog_ref[...])

    dirs = (0, 1)
    tris = [_tri(tc, False), _tri(tc, True)]
    tri_b = [t.astype(BF16) for t in tris]
    sts = [s0_ref[d].reshape(GW, SSD_N).T if has_init else None for d in dirs]
    for ci in range(nc):
        cidx = (ci, nc - 1 - ci)
        sls = [slice(c * tc, (c + 1) * tc) for c in cidx]
        x = [xs_s[s] for s in sls]
        bm = [bs_s[s] for s in sls]
        cm = [cs_s[s] for s in sls]
        dt = [dt_all[s] for s in sls]
        parts = [_split_hi_lo(dt[d] * a_row) for d in dirs]
        cum = [jnp.dot(tri_b[d], parts[d][0], preferred_element_type=F32)
               + jnp.dot(tri_b[d], parts[d][1], preferred_element_type=F32) for d in dirs]
        cum_t = [cum[d].T for d in dirs]
        end = [cum[0][tc - 1:tc], cum[1][0:1]]
        ex = [_bdot(jnp.concatenate(
            [dt[d], jnp.exp(cum[d]), jnp.exp(end[d] - cum[d]),
             jnp.broadcast_to(jnp.exp(end[d]), (SUBLANES, LANES))], axis=0), rsel_ref[d])
            for d in dirs]
        xdt = [x[d] * ex[d][0:tc] for d in dirs]
        cb = [_bdot_nt(cm[0], bm[0])]
        cb.append(cb[0] if cidx[1] == cidx[0] else _bdot_nt(cm[1], bm[1]))
        lms = [[jnp.exp(jnp.where(tris[d], cum[d][:, d * HPG + h:d * HPG + h + 1]
                                  - cum_t[d][d * HPG + h:d * HPG + h + 1, :], -jnp.inf))
                for h in range(HPG)] for d in dirs]
        ys = [[_bdot(cb[d] * lms[d][h], xdt[d][:, h * SSD_P:(h + 1) * SSD_P]) for h in range(HPG)]
              for d in dirs]
        y = [jnp.concatenate(ys[d], axis=1) for d in dirs]
        for d in dirs:
            if sts[d] is not None:
                y[d] = y[d] + _bdot(cm[d], sts[d]) * ex[d][tc:2 * tc]
        if cidx[0] == cidx[1]:
            y_s[sls[0]] = y_s[sls[0]] + (y[0] + y[1])
        else:
            for d in dirs:
                y_s[sls[d]] = y_s[sls[d]] + y[d]
        sc = [_bdot(bm[d].T, xdt[d] * ex[d][2 * tc:3 * tc]) for d in dirs]
        sts = [sc[d] if sts[d] is None else sts[d] * ex[d][3 * tc:3 * tc + 1] + sc[d] for d in dirs]
    if want_state:
        for d in dirs:
            st_ref[d] = sts[d].T.reshape(HPG, SSD_P, SSD_N)

    y_ref[...] = y_s[...]


def _ssd_rsel():
    r = np.zeros((2, LANES, GW), np.float32)
    for d in range(2):
        for h in range(HPG):
            r[d, d * HPG + h, h * SSD_P:(h + 1) * SSD_P] = 1.0
    return jnp.asarray(r, BF16)


def _ssd_call(proj, dtp, prm, *, L, n_seq, row0, s0, want_state):
    rb = row0 // L
    has_init = s0 is not None
    xbc = COL_XBC
    in_specs = [
        pl.BlockSpec((L, GW), lambda b, g: (rb + b, xbc // GW + g)),
        pl.BlockSpec((L, SSD_N), lambda b, g: (rb + b, (xbc + SSD_INNER) // SSD_N + g)),
        pl.BlockSpec((L, SSD_N), lambda b, g: (rb + b, (xbc + SSD_INNER + SSD_G * SSD_N) // SSD_N + g)),
        pl.BlockSpec((L, LANES), lambda b, g: (rb + b, g)),
        pl.BlockSpec((SSD_CONV, GW), lambda b, g: (0, g)),
        pl.BlockSpec((SSD_CONV, SSD_N), lambda b, g: (0, g)),
        pl.BlockSpec((SSD_CONV, SSD_N), lambda b, g: (0, g)),
        pl.BlockSpec((1, GW), lambda b, g: (0, g)),
        pl.BlockSpec((1, SSD_N), lambda b, g: (0, g)),
        pl.BlockSpec((1, SSD_N), lambda b, g: (0, g)),
        pl.BlockSpec((1, LANES), lambda b, g: (0, g)),
        pl.BlockSpec((1, LANES), lambda b, g: (0, g)),
        pl.BlockSpec((1, GW), lambda b, g: (0, g)),
        pl.BlockSpec((2, LANES, GW), lambda b, g: (0, 0, 0)),
    ]
    args = [proj, proj, proj, dtp, prm["cwx"], prm["cwb"], prm["cwc"], prm["cbx"], prm["cbb"], prm["cbc"],
            prm["dtb"], prm["alog"], prm["dcol"], prm["rsel"]]
    if has_init:
        in_specs.append(pl.BlockSpec((None, 2, HPG, SSD_P, SSD_N), lambda b, g: (b, 0, g, 0, 0)))
        args.append(s0)
    out_specs = [pl.BlockSpec((L, GW), lambda b, g: (b, g))]
    out_shape = [jax.ShapeDtypeStruct((n_seq * L, SSD_INNER), F32)]
    if want_state:
        out_specs.append(pl.BlockSpec((None, 2, HPG, SSD_P, SSD_N), lambda b, g: (b, 0, g, 0, 0)))
        out_shape.append(jax.ShapeDtypeStruct((n_seq, 2, SSD_HEADS, SSD_P, SSD_N), F32))
    return pl.pallas_call(
        functools.partial(_ssd_kernel, L=L, has_init=has_init, want_state=want_state),
        grid=(n_seq, SSD_G),
        in_specs=in_specs, out_specs=out_specs, out_shape=out_shape,
        scratch_shapes=[pltpu.VMEM((L, GW), F32), pltpu.VMEM((L, SSD_N), F32),
                        pltpu.VMEM((L, SSD_N), F32), pltpu.VMEM((L, GW), F32)],
        compiler_params=_cparams(("arbitrary", "arbitrary")),
        name=f"ssd_L{L}",
    )(*args)


def _gla_kernel(*refs, L, layer, has_init, want_state, hpb):
    q_ref, ff_ref, fb_ref, v_ref, g_ref, lb_ref, nw_ref = refs[:7]
    pos = 7
    s0_ref = None
    if has_init:
        s0_ref = refs[pos]
        pos += 1
    o_ref = refs[pos]
    pos += 1
    st_ref = None
    if want_state:
        st_ref = refs[pos]
        pos += 1
    qs, ks, ls, os_ = refs[pos:pos + 4]

    tc, gr = GLA_TC, GLA_GROUP
    ng, cpg = L // gr, gr // tc
    lbr = lb_ref[...]
    e = jnp.exp(lbr - jnp.max(lbr, axis=0, keepdims=True))
    sm = e / jnp.sum(e, axis=0, keepdims=True)
    lb = sm[0]
    for j in range(1, layer + 1):
        lb = lb + sm[j]

    qs[...] = _silu(q_ref[...])
    for d, fr in ((0, ff_ref), (1, fb_ref)):
        logit = fr[...]
        lbd = lb[d:d + 1]
        f = lbd + (1.0 - lbd) * jax.nn.sigmoid(logit)
        ks[d] = (1.0 - lbd) * jax.nn.sigmoid(-logit)
        ls[d] = jnp.log(f)

    r_i = lax.broadcasted_iota(I32, (gr, gr), 0)
    c_i = lax.broadcasted_iota(I32, (gr, gr), 1)
    sh = tc.bit_length() - 1
    same = (r_i >> sh) == (c_i >> sh)
    causal = [same & (c_i <= r_i), same & (c_i >= r_i)]
    bd = [m.astype(BF16) for m in causal]
    row_chunk = lax.broadcasted_iota(I32, (gr, 1), 0) >> sh
    col_chunk = lax.broadcasted_iota(I32, (1, gr), 1) >> sh

    def rows_of(x, r):
        return jnp.concatenate(
            [jnp.broadcast_to(x[c * tc + r:c * tc + r + 1], (tc, x.shape[1])) for c in range(cpg)], axis=0)

    def groups(units):
        n = range(len(units))
        ds_ = [u[0] for u in units]
        sl = [(pl.ds(u[2], gr), slice(u[1] * HG_K, (u[1] + 1) * HG_K)) for u in units]
        sts = [u[3] for u in units]
        parts = [_split_hi_lo(ls[ds_[i], sl[i][0], sl[i][1]]) for i in n]
        cums = [jnp.dot(bd[ds_[i]], parts[i][0], preferred_element_type=F32)
                + jnp.dot(bd[ds_[i]], parts[i][1], preferred_element_type=F32) for i in n]
        q = [qs[sl[i][0], sl[i][1]] for i in n]
        k = [ks[ds_[i], sl[i][0], sl[i][1]] for i in n]
        v = [v_ref[sl[i][0], sl[i][1]] for i in n]
        ref_b = [rows_of(cums[i], tc // 2 if ds_[i] else tc // 2 - 1) for i in n]
        end_b = [rows_of(cums[i], 0 if ds_[i] else tc - 1) for i in n]
        att = [_bdot_nt(q[i] * jnp.exp(cums[i] - ref_b[i]), k[i] * jnp.exp(ref_b[i] - cums[i])) for i in n]
        o = [_bdot(jnp.where(causal[ds_[i]], att[i], 0.0), v[i]) for i in n]
        kd_t = [(k[i] * jnp.exp(end_b[i] - cums[i])).T for i in n]
        inc = [_bdot(jnp.concatenate([jnp.where(col_chunk == c, kd_t[i], 0.0) for c in range(cpg)], axis=0), v[i])
               for i in n]
        dec_t = [jnp.exp(end_b[i]).T for i in n]
        stacks = []
        for i in n:
            st, s_in = sts[i], [None] * cpg
            for c in (range(cpg - 1, -1, -1) if ds_[i] else range(cpg)):
                s_in[c] = st
                st = st * dec_t[i][:, c * tc:c * tc + 1] + inc[i][c * HG_K:(c + 1) * HG_K]
            sts[i] = st
            stacks.append(jnp.concatenate(s_in, axis=0))
        qe = [q[i] * jnp.exp(cums[i]) for i in n]
        o = [o[i] + _bdot(jnp.concatenate([jnp.where(row_chunk == c, qe[i], 0.0) for c in range(cpg)], axis=1),
                          stacks[i]) for i in n]
        return o, sts

    def init(d, j):
        return s0_ref[d, j] if has_init else jnp.zeros((HG_K, HG_V), F32)

    heads = range(hpb)
    if ng == 1:
        o, sts = groups([(d, j, 0, init(d, j)) for j in heads for d in (0, 1)])
        outs = [o[2 * j] + o[2 * j + 1] for j in heads]
        finals = [(sts[2 * j], sts[2 * j + 1]) for j in heads]
    else:
        os_[...] = jnp.zeros_like(os_)

        def body(i, carry):
            g0 = (pl.multiple_of(i * gr, gr), pl.multiple_of((ng - 1 - i) * gr, gr))
            o, sts = groups([(d, j, g0[d], carry[j][d]) for j in heads for d in (0, 1)])
            for j in heads:
                cs = slice(j * HG_V, (j + 1) * HG_V)
                for d in (0, 1):
                    os_[pl.ds(g0[d], gr), cs] = os_[pl.ds(g0[d], gr), cs] + o[2 * j + d]
            return tuple((sts[2 * j], sts[2 * j + 1]) for j in heads)

        finals = lax.fori_loop(0, ng, body, tuple((init(0, j), init(1, j)) for j in heads))
        outs = [os_[:, j * HG_V:(j + 1) * HG_V] for j in heads]
    for j in heads:
        cs = slice(j * HG_V, (j + 1) * HG_V)
        if want_state:
            st_ref[0, j] = finals[j][0]
            st_ref[1, j] = finals[j][1]
        o_ref[:, cs] = _rms(outs[j], nw_ref[:, cs]) * _silu(g_ref[:, cs])


def _gla_call(proj, hg_lb, hg_nw, *, layer, L, n_seq, row0, s0, want_state, hpb):
    rb = row0 // L
    has_init = s0 is not None
    w = hpb * HG_K

    def col(c0):
        return lambda b, h: (rb + b, c0 // w + h)

    st_spec = pl.BlockSpec((None, 2, hpb, HG_K, HG_V), lambda b, h: (b, 0, h, 0, 0))
    in_specs = [
        pl.BlockSpec((L, w), col(COL_HQ)),
        pl.BlockSpec((L, w), col(COL_HF)),
        pl.BlockSpec((L, w), col(COL_HF + HG_QK)),
        pl.BlockSpec((L, w), col(COL_HI)),
        pl.BlockSpec((L, w), col(COL_HG)),
        pl.BlockSpec((DEPTH + 1, 2, w), lambda b, h: (0, 0, h)),
        pl.BlockSpec((1, w), lambda b, h: (0, h)),
    ]
    args = [proj, proj, proj, proj, proj, hg_lb, hg_nw]
    if has_init:
        in_specs.append(st_spec)
        args.append(s0)
    out_specs = [pl.BlockSpec((L, w), lambda b, h: (b, h))]
    out_shape = [jax.ShapeDtypeStruct((n_seq * L, HG_IV), F32)]
    if want_state:
        out_specs.append(st_spec)
        out_shape.append(jax.ShapeDtypeStruct((n_seq, 2, HG_HEADS, HG_K, HG_V), F32))
    return pl.pallas_call(
        functools.partial(_gla_kernel, L=L, layer=layer, has_init=has_init, want_state=want_state, hpb=hpb),
        grid=(n_seq, HG_HEADS // hpb),
        in_specs=in_specs, out_specs=out_specs, out_shape=out_shape,
        scratch_shapes=[pltpu.VMEM((L, w), F32), pltpu.VMEM((2, L, w), F32),
                        pltpu.VMEM((2, L, w), F32), pltpu.VMEM((L, w), F32)],
        compiler_params=_cparams(("arbitrary", "arbitrary")),
        name=f"gla_L{L}",
    )(*args)


def _ssm_out_kernel(xp_ref, xs_ref, yp_ref, ys_ref, op_ref, os_ref, z_ref, mod_ref, nw_ref, w_ref,
                    out_ref, *, tm):
    i = pl.program_id(0)
    g = _pick(i, tm, yp_ref, ys_ref) * _silu(z_ref[...])
    mix = (_bdot(_rms(g, nw_ref[...]), w_ref[0:SSD_INNER, :])
           + _bdot(_pick(i, tm, op_ref, os_ref), w_ref[SSD_INNER:, :]))
    out_ref[...] = _pick(i, tm, xp_ref, xs_ref) + _mod_vec(mod_ref, _mod_row(i, tm), 2) * mix


def _ssm_out(xp, xs, yp, ys, op, os_, proj, mod_l, nw, w, *, tm):
    return pl.pallas_call(
        functools.partial(_ssm_out_kernel, tm=tm),
        grid=(T // tm,),
        in_specs=_two_src_specs(tm) + _two_src_specs(tm, SSD_INNER) + _two_src_specs(tm, HG_IV) + [
            pl.BlockSpec((tm, SSD_INNER), lambda i: (i, COL_Z // SSD_INNER)),
            pl.BlockSpec((16, N_MOD * D), lambda i: (0, 0)),
            pl.BlockSpec((1, SSD_INNER), lambda i: (0, 0)),
            pl.BlockSpec((SSD_INNER + HG_IV, D), lambda i: (0, 0))],
        out_specs=pl.BlockSpec((tm, D), lambda i: (i, 0)),
        out_shape=jax.ShapeDtypeStruct((T, D), F32),
        compiler_params=_cparams(("arbitrary",)),
        name="ssm_out",
    )(xp, xs, yp, ys, op, os_, proj, mod_l, nw, w)


def _attn_out_kernel(x_ref, ap_ref, as_ref, mod_ref, w_ref, out_ref, *, tm):
    i = pl.program_id(0)
    mix = jnp.dot(_pick(i, tm, ap_ref, as_ref), w_ref[...], preferred_element_type=F32)
    out_ref[...] = x_ref[...] + _mod_vec(mod_ref, _mod_row(i, tm), 2) * mix


def _attn_out(x, ap, as_, mod_l, w, *, tm):
    return pl.pallas_call(
        functools.partial(_attn_out_kernel, tm=tm),
        grid=(T // tm,),
        in_specs=[pl.BlockSpec((tm, D), lambda i: (i, 0))] + _two_src_specs(tm) + [
            pl.BlockSpec((16, N_MOD * D), lambda i: (0, 0)),
            pl.BlockSpec((D, D), lambda i: (0, 0))],
        out_specs=pl.BlockSpec((tm, D), lambda i: (i, 0)),
        out_shape=jax.ShapeDtypeStruct((T, D), F32),
        compiler_params=_cparams(("arbitrary",)),
        name="attn_out",
    )(x, ap, as_, mod_l, w)


def _route_kernel(x_ref, mod_ref, nw_ref, rw_ref, rb_ref, h_ref, meta_ref, metat_ref, cnt_ref):
    i = pl.program_id(0)
    r = _mod_row(i, TB)
    h = _rms(x_ref[...], nw_ref[...]) * (1.0 + _mod_vec(mod_ref, r, 4)) + _mod_vec(mod_ref, r, 3)
    hb = h.astype(BF16)
    h_ref[...] = hb
    logits = jnp.dot(hb, rw_ref[...], preferred_element_type=F32) + rb_ref[...]
    lane = lax.broadcasted_iota(I32, (TB, LANES), 1)
    lane_f = lane.astype(F32)
    vals, hots, idxs = [], [], []
    for _ in range(TOP_K):
        m = jnp.max(logits, axis=-1, keepdims=True)
        idx = jnp.min(jnp.where(logits == m, lane_f, float(LANES)), axis=-1, keepdims=True)
        hot = lane_f == idx
        vals.append(m)
        idxs.append(idx)
        hots.append(hot)
        logits = jnp.where(hot, -jnp.inf, logits)
    es = [jnp.exp(v - vals[0]) for v in vals]
    den = es[0] + es[1] + es[2] + es[3]

    r_i = lax.broadcasted_iota(I32, (TB, TB), 0)
    c_i = lax.broadcasted_iota(I32, (TB, TB), 1)
    below = (c_i < r_i).astype(BF16)
    base = jnp.zeros((1, LANES), F32)
    ranks = []
    for k in range(TOP_K):
        hot_f = hots[k].astype(F32)
        before = jnp.dot(below, hots[k].astype(BF16), preferred_element_type=F32)
        ranks.append(base + before)
        base = base + jnp.sum(hot_f, axis=0, keepdims=True)
    units = jnp.floor((base + (SEG - 1)) * (1.0 / SEG))
    lr = lax.broadcasted_iota(I32, (LANES, LANES), 0)
    lc = lax.broadcasted_iota(I32, (LANES, LANES), 1)
    seg_start = SEG * jnp.dot(jnp.broadcast_to(units, (SUBLANES, LANES)).astype(BF16),
                              (lr < lc).astype(BF16), preferred_element_type=F32)[0:1]
    meta = jnp.zeros((TB, LANES), F32)
    for k in range(TOP_K):
        row = jnp.sum(hots[k].astype(F32) * (seg_start + ranks[k]), axis=-1, keepdims=True)
        meta = jnp.where(lane == k, idxs[k], meta)
        meta = jnp.where(lane == TOP_K + k, es[k] / den, meta)
        meta = jnp.where(lane == 2 * TOP_K + k, row, meta)
    meta_ref[...] = meta
    metat_ref[...] = meta.T
    cnt_ref[...] = base


def _route(x, mod_l, nw, rw, rb):
    return pl.pallas_call(
        _route_kernel,
        grid=(NB,),
        in_specs=[pl.BlockSpec((TB, D), lambda i: (i, 0)),
                  pl.BlockSpec((16, N_MOD * D), lambda i: (0, 0)),
                  pl.BlockSpec((1, D), lambda i: (0, 0)),
                  pl.BlockSpec((D, LANES), lambda i: (0, 0)),
                  pl.BlockSpec((1, LANES), lambda i: (0, 0))],
        out_specs=[pl.BlockSpec((TB, D), lambda i: (i, 0)),
                   pl.BlockSpec((TB, LANES), lambda i: (i, 0)),
                   pl.BlockSpec((LANES, TB), lambda i: (0, i)),
                   pl.BlockSpec((None, 1, LANES), lambda i: (i, 0, 0))],
        out_shape=[jax.ShapeDtypeStruct((T, D), BF16),
                   jax.ShapeDtypeStruct((T, LANES), F32),
                   jax.ShapeDtypeStruct((LANES, T), F32),
                   jax.ShapeDtypeStruct((NB, 1, LANES), F32)],
        compiler_params=_cparams(("arbitrary",)),
        name="moe_route",
    )(x, mod_l, nw, rw, rb)


def _copy(src, dst, sem):
    return pltpu.make_async_copy(src, dst, sem)


def _segment_copies(n, src, dst, sizes, make, wait):
    src = jnp.asarray(src, I32)
    dst = jnp.asarray(dst, I32)
    for size in sizes:
        take = n & size

        @pl.when(take != 0)
        def _(src=src, dst=dst, size=size):
            cp = make(pl.multiple_of(src, SEG), pl.multiple_of(dst, SEG), size)
            if wait:
                cp.wait()
            else:
                cp.start()

        src = src + take
        dst = dst + take


def _sort_push_kernel(tab_ref, tabp_ref, ztab_ref, nv_ref, metat_ref, h_ref, xs_hbm, buf, zb, sem, zsem):
    b = pl.program_id(0)
    slot = b & 1

    def zero_copies(wait):
        def per_expert(e, carry):
            _segment_copies(ztab_ref[1, e], 0, ztab_ref[0, e], PAD_SIZES,
                            lambda s, d, size: _copy(zb.at[pl.ds(0, size)], xs_hbm.at[pl.ds(d, size)], zsem), wait)
            return carry

        lax.fori_loop(0, N_EXP, per_expert, 0)

        def per_tile(j, carry):
            cp = _copy(zb, xs_hbm.at[pl.ds(pl.multiple_of(j * TM_E, TM_E), TM_E)], zsem)
            if wait:
                cp.wait()
            else:
                cp.start()
            return carry

        lax.fori_loop(nv_ref[0], N_ETILES, per_tile, 0)

    @pl.when(b == 0)
    def _():
        zb[...] = jnp.zeros_like(zb)
        zero_copies(False)

    ch = 256
    for c in range(RB // ch):
        rows = (lax.broadcasted_iota(I32, (ch, TB), 0) + c * ch).astype(F32)
        hit = rows == metat_ref[2 * TOP_K:2 * TOP_K + 1, :]
        for k in range(1, TOP_K):
            hit = hit | (rows == metat_ref[2 * TOP_K + k:2 * TOP_K + k + 1, :])
        buf[slot, c * ch:(c + 1) * ch, :] = jnp.dot(hit.astype(BF16), h_ref[...], preferred_element_type=F32)

    def push(t_ref, s, wait):
        def per_expert(e, carry):
            _segment_copies(
                t_ref[0, e], t_ref[1, e], t_ref[2, e], SEG_SIZES,
                lambda so, do, size: _copy(buf.at[s, pl.ds(so, size)], xs_hbm.at[pl.ds(do, size)], sem.at[s]), wait)
            return carry

        lax.fori_loop(0, N_EXP, per_expert, 0)

    def drain(t_ref, s):
        _segment_copies(t_ref[1, N_EXP - 1] + t_ref[0, N_EXP - 1], 0, 0, BUF_SIZES,
                        lambda so, do, size: _copy(buf.at[s, pl.ds(0, size)], xs_hbm.at[pl.ds(0, size)], sem.at[s]),
                        True)

    @pl.when(b > 0)
    def _():
        drain(tabp_ref, 1 - slot)

    push(tab_ref, slot, False)

    @pl.when(b == NB - 1)
    def _():
        drain(tab_ref, slot)

    @pl.when(b == 0)
    def _():
        zero_copies(True)


def _sort_push(tab, ztab, n_valid, metat, h):
    return pl.pallas_call(
        _sort_push_kernel,
        grid=(NB,),
        in_specs=[pl.BlockSpec((None, 3, N_EXP), lambda b: (b, 0, 0), memory_space=pltpu.SMEM),
                  pl.BlockSpec((None, 3, N_EXP), lambda b: (jnp.maximum(b - 1, 0), 0, 0),
                               memory_space=pltpu.SMEM),
                  pl.BlockSpec(memory_space=pltpu.SMEM),
                  pl.BlockSpec(memory_space=pltpu.SMEM),
                  pl.BlockSpec((LANES, TB), lambda b: (0, b)),
                  pl.BlockSpec((TB, D), lambda b: (b, 0))],
        out_specs=pl.BlockSpec(memory_space=pl.ANY),
        out_shape=jax.ShapeDtypeStruct((P_ROWS, D), F32),
        scratch_shapes=[pltpu.VMEM((2, RB, D), F32), pltpu.VMEM((TM_E, D), F32),
                        pltpu.SemaphoreType.DMA((2,)), pltpu.SemaphoreType.DMA(())],
        compiler_params=_cparams(("arbitrary",)),
        name="moe_sort_push",
    )(tab, tab, ztab, n_valid, metat, h)


def _expert_kernel(te_ref, tr_ref, nv_ref, par_ref, nxt_ref, x_ref, w1_hbm, b1_ref, w2_hbm, b2_ref, y_ref,
                   wf1, wf2, w1_s, w2_s, sem, *, layer):
    j = pl.program_id(0)
    valid = j < nv_ref[0]
    fresh = jnp.logical_or(j == 0, te_ref[j] != te_ref[jnp.maximum(j - 1, 0)])

    def fetch(e, s):
        return (_copy(w1_hbm.at[layer, e], wf1.at[s], sem.at[0, s]),
                _copy(w2_hbm.at[layer, e], wf2.at[s], sem.at[1, s]))

    @pl.when(jnp.logical_and(valid, fresh))
    def _():
        s = par_ref[j]

        @pl.when(j == 0)
        def _():
            for cp in fetch(te_ref[0], 0):
                cp.start()

        for cp in fetch(te_ref[j], s):
            cp.wait()

        @pl.when(nxt_ref[j] >= 0)
        def _():
            for cp in fetch(nxt_ref[j], 1 - s):
                cp.start()

        w1_s[...] = wf1[s].astype(BF16)
        w2_s[...] = wf2[s].astype(BF16)

    @pl.when(valid)
    def _():
        rows = lax.broadcasted_iota(I32, (TM_E, 1), 0)
        x = jnp.where(rows < tr_ref[j], x_ref[...], 0.0).astype(BF16)
        gu = jnp.dot(x, w1_s[...], preferred_element_type=F32) + b1_ref[...]
        gate = jnp.minimum(gu[:, :FF], SWIGLU_LIMIT)
        up = jnp.clip(gu[:, FF:], -SWIGLU_LIMIT, SWIGLU_LIMIT)
        act = (up + 1.0) * gate * jax.nn.sigmoid(SWIGLU_ALPHA * gate)
        y_ref[...] = jnp.dot(act.astype(BF16), w2_s[...], preferred_element_type=F32) + b2_ref[...]

    @pl.when(jnp.logical_not(valid))
    def _():
        y_ref[...] = jnp.zeros_like(y_ref)


def _experts(layer, tile_exp, tile_rows, n_valid, run_par, run_next, xs, w1, b1, w2, b2):
    def wmap(j, te, *_):
        return (layer, te[j], 0, 0)

    return pl.pallas_call(
        functools.partial(_expert_kernel, layer=layer),
        grid_spec=pltpu.PrefetchScalarGridSpec(
            num_scalar_prefetch=5,
            grid=(N_ETILES,),
            in_specs=[pl.BlockSpec((TM_E, D), lambda j, *_: (j, 0)),
                      pl.BlockSpec(memory_space=pl.ANY),
                      pl.BlockSpec((None, None, 1, 2 * FF), wmap),
                      pl.BlockSpec(memory_space=pl.ANY),
                      pl.BlockSpec((None, None, 1, D), wmap)],
            out_specs=pl.BlockSpec((TM_E, D), lambda j, *_: (j, 0)),
            scratch_shapes=[pltpu.VMEM((2, D, 2 * FF), F32), pltpu.VMEM((2, FF, D), F32),
                            pltpu.VMEM((D, 2 * FF), BF16), pltpu.VMEM((FF, D), BF16),
                            pltpu.SemaphoreType.DMA((2, 2))]),
        out_shape=jax.ShapeDtypeStruct((P_ROWS, D), F32),
        compiler_params=_cparams(("arbitrary",)),
        name="moe_experts",
    )(tile_exp, tile_rows, n_valid, run_par, run_next, xs, w1, b1.reshape(DEPTH, N_EXP, 1, 2 * FF), w2,
      b2.reshape(DEPTH, N_EXP, 1, D))


def _combine_kernel(tab_ref, tabn_ref, y_hbm, x_ref, meta_ref, mod_ref, out_ref, buf, sem):
    b = pl.program_id(0)
    slot = b & 1

    def pull(t_ref, s, wait):
        def per_expert(e, carry):
            _segment_copies(
                t_ref[0, e], t_ref[2, e], t_ref[1, e], SEG_SIZES,
                lambda so, do, size: _copy(y_hbm.at[pl.ds(so, size)], buf.at[s, pl.ds(do, size)], sem.at[s]), wait)
            return carry

        lax.fori_loop(0, N_EXP, per_expert, 0)

    @pl.when(b == 0)
    def _():
        pull(tab_ref, slot, False)

    @pl.when(b + 1 < NB)
    def _():
        pull(tabn_ref, 1 - slot, False)

    used = tab_ref[1, N_EXP - 1] + tab_ref[0, N_EXP - 1]
    _segment_copies(used, 0, 0, BUF_SIZES,
                    lambda so, do, size: _copy(y_hbm.at[pl.ds(0, size)], buf.at[slot, pl.ds(0, size)], sem.at[slot]),
                    True)

    meta = meta_ref[...]
    ch = 256
    acc = jnp.zeros((TB, D), F32)
    for c in range(RB // ch):
        cols = (lax.broadcasted_iota(I32, (TB, ch), 1) + c * ch).astype(F32)
        g = jnp.zeros((TB, ch), F32)
        for k in range(TOP_K):
            g = g + jnp.where(cols == meta[:, 2 * TOP_K + k:2 * TOP_K + k + 1],
                              meta[:, TOP_K + k:TOP_K + k + 1], 0.0)
        rows = lax.broadcasted_iota(I32, (ch, 1), 0) + c * ch
        y = jnp.where(rows < used, buf[slot, c * ch:(c + 1) * ch, :], 0.0)
        acc = acc + _bdot(g, y)
    out_ref[...] = x_ref[...] + _mod_vec(mod_ref, _mod_row(b, TB), 5) * acc


def _combine(tab, y, x, meta, mod_l):
    return pl.pallas_call(
        _combine_kernel,
        grid=(NB,),
        in_specs=[pl.BlockSpec((None, 3, N_EXP), lambda b: (b, 0, 0), memory_space=pltpu.SMEM),
                  pl.BlockSpec((None, 3, N_EXP), lambda b: (jnp.minimum(b + 1, NB - 1), 0, 0),
                               memory_space=pltpu.SMEM),
                  pl.BlockSpec(memory_space=pl.ANY),
                  pl.BlockSpec((TB, D), lambda b: (b, 0)),
                  pl.BlockSpec((TB, LANES), lambda b: (b, 0)),
                  pl.BlockSpec((16, N_MOD * D), lambda b: (0, 0))],
        out_specs=pl.BlockSpec((TB, D), lambda b: (b, 0)),
        out_shape=jax.ShapeDtypeStruct((T, D), F32),
        scratch_shapes=[pltpu.VMEM((2, RB, D), F32), pltpu.SemaphoreType.DMA((2,))],
        compiler_params=_cparams(("arbitrary",)),
        name="moe_combine",
    )(tab, tab, y, x, meta, mod_l)


def _moe_layer(layer, x, mod_l, nw, rw, rb, w1, b1, w2, b2):
    rw_p = jnp.zeros((D, LANES), BF16).at[:, :N_EXP].set(rw.astype(BF16))
    rb_p = jnp.full((1, LANES), -jnp.inf, F32).at[0, :N_EXP].set(rb)
    h, meta, metat, cnt = _route(x, mod_l, nw, rw_p, rb_p)

    counts = cnt[:, 0, :N_EXP].astype(I32)
    n_seg = (counts + SEG - 1) // SEG * SEG
    seg_start = jnp.cumsum(n_seg, axis=1) - n_seg
    tot = jnp.sum(n_seg, axis=0)
    padded = (tot + TM_E - 1) // TM_E * TM_E
    ends = jnp.cumsum(padded)
    offs = ends - padded
    glob = offs[None, :] + jnp.cumsum(n_seg, axis=0) - n_seg
    tab = jnp.stack([n_seg, seg_start, glob], axis=1)
    ztab = jnp.stack([offs + tot, padded - tot], axis=0)
    n_valid = ends[-1] // TM_E
    starts = jnp.arange(N_ETILES, dtype=I32) * TM_E
    te = jnp.minimum(jnp.sum((ends[None, :] <= starts[:, None]).astype(I32), axis=1), N_EXP - 1)
    last = jnp.sum((ends[:-1] < ends[-1]).astype(I32))
    te = jnp.where(starts < ends[-1], te, last)
    tr = jnp.clip(offs[te] + tot[te] - starts, 0, TM_E)
    nv = n_valid.reshape(1)
    run_start = jnp.concatenate([jnp.ones((1,), I32), (te[1:] != te[:-1]).astype(I32)])
    run_par = (jnp.cumsum(run_start) - 1) & 1
    eid = jnp.arange(N_EXP, dtype=I32)
    later = (eid[None, :] > eid[:, None]) & (padded[None, :] > 0)
    nxt_e = jnp.min(jnp.where(later, eid[None, :], N_EXP), axis=1)
    run_next = jnp.where(nxt_e < N_EXP, nxt_e, -1)[te]

    xs = _sort_push(tab, ztab, nv, metat, h)
    y = _experts(layer, te, tr, nv, run_par, run_next, xs, w1, b1, w2, b2)
    return _combine(tab, y, x, meta, mod_l)


def _head_sel():
    g = np.zeros((D, LANES), np.float32)
    for c in range(D):
        g[c, c // HEAD_DIM] = 1.0
    return jnp.asarray(g, BF16)


def _head_rms(x, sel, sel_t, w):
    n = x.shape[1]
    hi, lo = _split_hi_lo(x * x)
    ss = (jnp.dot(hi, sel[:n], preferred_element_type=F32)
          + jnp.dot(lo, sel[:n], preferred_element_type=F32))
    inv = lax.rsqrt(ss * (1.0 / HEAD_DIM) + EPS)
    return x * _bdot(inv, sel_t[:, :n]) * w


def _rope(x, cos, sin):
    n = x.shape[1]
    lane = lax.broadcasted_iota(I32, x.shape, 1)
    first = (lane % (HEAD_DIM // 2)) < (HEAD_DIM // 4)
    rot = jnp.where(first, -pltpu.roll(x, n - HEAD_DIM // 4, axis=1), pltpu.roll(x, HEAD_DIM // 4, axis=1))
    return x * cos + rot * sin


def _qkv_kernel(x_ref, mod_ref, nw_ref, w_ref, sel_ref, selt_ref, qw_ref, kw_ref, cos_ref, sin_ref,
                q_ref, k_ref, v_ref, kf_ref, vf_ref, *, tm):
    i = pl.program_id(0)
    r = _mod_row(i, tm)
    h = _rms(x_ref[...], nw_ref[...]) * (1.0 + _mod_vec(mod_ref, r, 1)) + _mod_vec(mod_ref, r, 0)
    qkv = _bdot(h, w_ref[...])
    nq, nk = ATT_HEADS * HEAD_DIM, KV_HEADS * HEAD_DIM
    q = _head_rms(qkv[:, :nq], sel_ref[...], selt_ref[...], qw_ref[...])
    k = _head_rms(qkv[:, nq:nq + nk], sel_ref[...], selt_ref[...], kw_ref[...])
    v = qkv[:, nq + nk:]
    kf_ref[...] = k
    vf_ref[...] = v
    latent = i * tm >= NP_TOK
    cos = jnp.where(latent, cos_ref[...], 1.0)
    sin = jnp.where(latent, sin_ref[...], 0.0)
    q = _rope(q, cos, sin) * (HEAD_DIM ** -0.5)
    k = _rope(k, cos[:, :nk], sin[:, :nk])
    q_ref[...] = q.T.astype(BF16)
    v_ref[...] = v.T.astype(BF16)
    for hd in range(KV_HEADS):
        k_ref[hd] = k[:, hd * HEAD_DIM:(hd + 1) * HEAD_DIM].astype(BF16)


def _qkv(x, mod_l, nw, w, qw, kw, cos, sin, *, tm):
    nk = KV_HEADS * HEAD_DIM
    sel = _head_sel()
    nblk = DEC_SEQ // tm

    def cmap(i):
        return (jnp.maximum(i - NP_TOK // tm, 0) % nblk, 0)

    return pl.pallas_call(
        functools.partial(_qkv_kernel, tm=tm),
        grid=(T // tm,),
        in_specs=[pl.BlockSpec((tm, D), lambda i: (i, 0)),
                  pl.BlockSpec((16, N_MOD * D), lambda i: (0, 0)),
                  pl.BlockSpec((1, D), lambda i: (0, 0)),
                  pl.BlockSpec((D, D + 2 * nk), lambda i: (0, 0)),
                  pl.BlockSpec((D, LANES), lambda i: (0, 0)),
                  pl.BlockSpec((LANES, D), lambda i: (0, 0)),
                  pl.BlockSpec((1, D), lambda i: (0, 0)),
                  pl.BlockSpec((1, nk), lambda i: (0, 0)),
                  pl.BlockSpec((tm, D), cmap),
                  pl.BlockSpec((tm, D), cmap)],
        out_specs=[pl.BlockSpec((D, tm), lambda i: (0, i)),
                   pl.BlockSpec((KV_HEADS, tm, HEAD_DIM), lambda i: (0, i, 0)),
                   pl.BlockSpec((nk, tm), lambda i: (0, i)),
                   pl.BlockSpec((tm, nk), lambda i: (i, 0)),
                   pl.BlockSpec((tm, nk), lambda i: (i, 0))],
        out_shape=[jax.ShapeDtypeStruct((D, T), BF16),
                   jax.ShapeDtypeStruct((KV_HEADS, T, HEAD_DIM), BF16),
                   jax.ShapeDtypeStruct((nk, T), BF16),
                   jax.ShapeDtypeStruct((T, nk), F32),
                   jax.ShapeDtypeStruct((T, nk), F32)],
        compiler_params=_cparams(("arbitrary",)),
        name="attn_qkv",
    )(x, mod_l, nw, w, sel, sel.T, qw, kw, cos, sin)


def _attn_kernel(*refs, has_cache):
    if has_cache:
        q_ref, k_ref, v_ref, kc_ref, vc_ref, o_ref = refs
    else:
        q_ref, k_ref, v_ref, o_ref = refs
    k = k_ref[0]
    vt = v_ref[...]
    heads = range(Q_PER_KV)
    qts = [q_ref[j * HEAD_DIM:(j + 1) * HEAD_DIM, :] for j in heads]
    ss = [jnp.dot(k, qt, preferred_element_type=F32) for qt in qts]
    ms = [jnp.max(s, axis=0, keepdims=True) for s in ss]
    if has_cache:
        scs = [jnp.dot(kc_ref[0, 0], qt, preferred_element_type=F32) for qt in qts]
        ms = [jnp.maximum(m, jnp.max(sc, axis=0, keepdims=True)) for m, sc in zip(ms, scs)]
    ps = [jnp.exp(s - m) for s, m in zip(ss, ms)]
    dens = [jnp.sum(p, axis=0, keepdims=True) for p in ps]
    os_ = [jnp.dot(vt, p.astype(BF16), preferred_element_type=F32) for p in ps]
    if has_cache:
        pcs = [jnp.exp(sc - m) for sc, m in zip(scs, ms)]
        dens = [d + jnp.sum(pc, axis=0, keepdims=True) for d, pc in zip(dens, pcs)]
        os_ = [o + jnp.dot(vc_ref[0, 0], pc.astype(BF16), preferred_element_type=F32) for o, pc in zip(os_, pcs)]
    o_ref[...] = jnp.concatenate([o / d for o, d in zip(os_, dens)], axis=0).T.astype(BF16)


def _attn_call(q, k, v, *, L, n_seq, row0, tq, cache):
    rb_q = row0 // tq
    rb_k = row0 // L
    nq = L // tq
    has_cache = cache is not None
    in_specs = [pl.BlockSpec((Q_PER_KV * HEAD_DIM, tq), lambda b, g, t: (g, rb_q + b * nq + t)),
                pl.BlockSpec((1, L, HEAD_DIM), lambda b, g, t: (g, rb_k + b, 0)),
                pl.BlockSpec((HEAD_DIM, L), lambda b, g, t: (g, rb_k + b))]
    args = [q, k, v]
    if has_cache:
        in_specs += [pl.BlockSpec((1, 1, PAST, HEAD_DIM), lambda b, g, t: (b, g, 0, 0)),
                     pl.BlockSpec((1, 1, HEAD_DIM, PAST), lambda b, g, t: (b, g, 0, 0))]
        args += list(cache)
    gw = Q_PER_KV * HEAD_DIM
    return pl.pallas_call(
        functools.partial(_attn_kernel, has_cache=has_cache),
        grid=(n_seq, KV_HEADS, nq),
        in_specs=in_specs,
        out_specs=pl.BlockSpec((tq, gw), lambda b, g, t: (b * nq + t, g)),
        out_shape=jax.ShapeDtypeStruct((n_seq * L, D), BF16),
        compiler_params=_cparams(("arbitrary", "arbitrary", "arbitrary")),
        name=f"attn_L{L}",
    )(*args)


def _rope_tables():
    rows = DEC_SEQ // GRID_W
    row = jnp.repeat(jnp.arange(rows), GRID_W)
    colp = jnp.tile(jnp.arange(GRID_W), rows)
    pos = jnp.stack([row, colp], axis=-1).astype(F32)
    half = HEAD_DIM // 2
    inv_freq = ROPE_THETA ** (-jnp.arange(0, half, 2, dtype=F32) / half)
    ang = pos[:, :, None] * inv_freq
    ang = jnp.concatenate([ang, ang], axis=-1).reshape(DEC_SEQ, HEAD_DIM)
    ang = jnp.tile(ang, (1, ATT_HEADS))
    return jnp.cos(ang), jnp.sin(ang)


def _final_kernel(x_ref, w_ref, o_ref):
    o_ref[...] = _rms(x_ref[...], w_ref[...])


def _final_norm(x, w, *, row0, n_rows, tm):
    rb = row0 // tm
    return pl.pallas_call(
        _final_kernel,
        grid=(n_rows // tm,),
        in_specs=[pl.BlockSpec((tm, D), lambda i: (rb + i, 0)),
                  pl.BlockSpec((1, D), lambda i: (0, 0))],
        out_specs=pl.BlockSpec((tm, D), lambda i: (i, 0)),
        out_shape=jax.ShapeDtypeStruct((n_rows, D), F32),
        compiler_params=_cparams(("arbitrary",)),
        name="final_norm",
    )(x, w)


def _ssd_params(conv_w, conv_b, dt_bias, a_log, d_skip):
    def dt_layout(p):
        g = p.reshape(2, SSD_G, HPG).transpose(1, 0, 2).reshape(SSD_G, 2 * HPG)
        return jnp.zeros((SSD_G, LANES), F32).at[:, :2 * HPG].set(g).reshape(1, SSD_G * LANES)

    ni = SSD_INNER
    nb = SSD_G * SSD_N
    return dict(
        cwx=conv_w[:, :ni], cwb=conv_w[:, ni:ni + nb], cwc=conv_w[:, ni + nb:],
        cbx=conv_b[None, :ni], cbb=conv_b[None, ni:ni + nb], cbc=conv_b[None, ni + nb:],
        dtb=dt_layout(dt_bias), alog=dt_layout(a_log),
        dcol=jnp.repeat(d_skip, SSD_P)[None, :], rsel=_ssd_rsel())


def _ssm_in_weights(w):
    ni, cd = SSD_INNER, SSD_INNER + 2 * SSD_G * SSD_N
    o_xbc, o_dt = ni, ni + cd
    o_hq = o_dt + 2 * SSD_HEADS
    o_hf, o_hi = o_hq + HG_QK, o_hq + 3 * HG_QK
    o_hg = o_hi + HG_IV
    main = jnp.concatenate([w[:, :ni], w[:, o_hq:o_hf], w[:, o_hi:o_hg], w[:, o_hg:o_hg + HG_IV],
                            w[:, o_hf:o_hi], w[:, o_xbc:o_dt]], axis=1).astype(BF16)
    wdt = w[:, o_dt:o_hq].reshape(D, 2, SSD_G, HPG).transpose(0, 2, 1, 3).reshape(D, SSD_G, 2 * HPG)
    wdt = jnp.zeros((D, SSD_G, LANES), F32).at[:, :, :2 * HPG].set(wdt).reshape(D, SSD_G * LANES)
    return main, wdt.astype(BF16)


def kernel(x_prompt, x_sample, c, c_ctx, state_ssd, state_hgrn, cache_k, cache_v, mod_w, mod_b, norm1_w,
           norm2_w, ssm_in_w, ssd_conv_w, ssd_conv_b, ssd_dt_bias, ssd_a_log, ssd_d, ssd_norm_w, hg_lb,
           hg_norm_w, ssm_out_w, attn_in_w, q_norm_w, k_norm_w, attn_out_w, router_w, router_b, exp_w1,
           exp_b1, exp_w2, exp_b2, final_norm_w):
    xp = x_prompt.reshape(NP_TOK, D)
    xs = x_sample.reshape(NS_TOK, D)
    cvec = jnp.zeros((16, D), F32).at[0].set(c_ctx).at[1:1 + DEC_BATCH].set(c)
    mod = _mod_table(cvec, mod_w, mod_b)

    w_main, w_dt = _ssm_in_weights(ssm_in_w[0])
    proj, dtp = _inproj(xp, xs, mod[0], norm1_w[0][None], w_main, w_dt, tm=1024, tn=1280)
    prm = _ssd_params(ssd_conv_w[0], ssd_conv_b[0], ssd_dt_bias[0], ssd_a_log[0], ssd_d[0])
    yp, st_ssd = _ssd_call(proj, dtp, prm, L=SEQ, n_seq=BATCH, row0=0, s0=None, want_state=True)
    (ys,) = _ssd_call(proj, dtp, prm, L=DEC_SEQ, n_seq=DEC_BATCH, row0=NP_TOK, s0=state_ssd[:, 0],
                      want_state=False)
    hg_nw = hg_norm_w[0][None]
    op, st_hg = _gla_call(proj, hg_lb, hg_nw, layer=0, L=SEQ, n_seq=BATCH, row0=0, s0=None, want_state=True,
                          hpb=4)
    (os_,) = _gla_call(proj, hg_lb, hg_nw, layer=0, L=DEC_SEQ, n_seq=DEC_BATCH, row0=NP_TOK,
                       s0=state_hgrn[:, 0], want_state=False, hpb=2)
    x = _ssm_out(xp, xs, yp, ys, op, os_, proj, mod[0], ssd_norm_w[0][None], ssm_out_w[0].astype(BF16), tm=512)
    x = _moe_layer(0, x, mod[0], norm2_w[0][None], router_w[0], router_b[0], exp_w1, exp_b1, exp_w2, exp_b2)

    cos, sin = _rope_tables()
    q, k, v, kf, vf = _qkv(x, mod[1], norm1_w[1][None], attn_in_w[0].astype(BF16),
                           jnp.tile(q_norm_w[0], ATT_HEADS)[None], jnp.tile(k_norm_w[0], KV_HEADS)[None],
                           cos, sin, tm=256)
    ap = _attn_call(q, k, v, L=SEQ, n_seq=BATCH, row0=0, tq=SEQ, cache=None)
    ck = cache_k[:, 0].transpose(0, 2, 1, 3).astype(BF16)
    cv = cache_v[:, 0].transpose(0, 2, 3, 1).astype(BF16)
    as_ = _attn_call(q, k, v, L=DEC_SEQ, n_seq=DEC_BATCH, row0=NP_TOK, tq=256, cache=(ck, cv))
    x = _attn_out(x, ap, as_, mod[1], attn_out_w[0].astype(BF16), tm=512)
    x = _moe_layer(1, x, mod[1], norm2_w[1][None], router_w[1], router_b[1], exp_w1, exp_b1, exp_w2, exp_b2)

    y_prompt = _final_norm(x, final_norm_w[None], row0=0, n_rows=NP_TOK, tm=512).reshape(BATCH, SEQ, D)
    y_sample = _final_norm(x, final_norm_w[None], row0=NP_TOK, n_rows=NS_TOK, tm=512).reshape(
        DEC_BATCH, DEC_SEQ, D)
    new_ssd = st_ssd.reshape(BATCH, 1, 2, SSD_HEADS, SSD_P, SSD_N)
    new_hg = st_hg.reshape(BATCH, 1, 2, HG_HEADS, HG_K, HG_V)
    new_k = kf[:NP_TOK].reshape(BATCH, 1, SEQ, KV_HEADS, HEAD_DIM)
    new_v = vf[:NP_TOK].reshape(BATCH, 1, SEQ, KV_HEADS, HEAD_DIM)
    return (y_prompt, y_sample, new_ssd, new_hg, new_k, new_v)
```

```python
import functools

import jax
import jax.numpy as jnp
import numpy as np
from jax import lax
from jax.experimental import pallas as pl
from jax.experimental.pallas import tpu as pltpu

F32 = jnp.float32
BF16 = jnp.bfloat16
I32 = jnp.int32

D = 1024
BATCH, SEQ = 32, 256
DEC_BATCH, DEC_SEQ = 8, 1024
PAST = 256
NP_TOK = BATCH * SEQ
NS_TOK = DEC_BATCH * DEC_SEQ
T = NP_TOK + NS_TOK
DEPTH = 2
N_MOD = 6
EPS = 1e-6
GRID_W = 64
ROPE_THETA = 10000.0

SSD_HEADS, SSD_P, SSD_N, SSD_G = 16, 64, 128, 2
SSD_INNER = SSD_HEADS * SSD_P
SSD_CONV = 5
HPG = SSD_HEADS // SSD_G
GW = HPG * SSD_P
HG_HEADS, HG_K, HG_V = 8, 128, 128
HG_QK = HG_HEADS * HG_K
HG_IV = HG_HEADS * HG_V
ATT_HEADS, KV_HEADS, HEAD_DIM = 16, 4, 64
Q_PER_KV = ATT_HEADS // KV_HEADS
N_EXP, TOP_K, FF = 32, 4, 1024
SWIGLU_ALPHA, SWIGLU_LIMIT = 1.702, 7.0

LANES = 128
SUBLANES = 8
VMEM_LIMIT = 56 * 1024 * 1024

COL_HQ, COL_HF, COL_XBC = 0, 1024, 3072
PROJ_F = 4608
COL_Z, COL_HI, COL_HG = 0, 1024, 2048
PROJ_B = 3072

SSD_TC = 256
GLA_TC = 64
GLA_GROUP = 256

TB = 512
NB = T // TB
SEG = SUBLANES
RB = TB * TOP_K + N_EXP * SEG
TM_E = 512
SEG_SIZES = tuple(SEG << i for i in range((TB // SEG).bit_length() - 1, -1, -1))
PAD_SIZES = tuple(s for s in SEG_SIZES if s < TM_E)
BUF_SIZES = tuple(SEG << i for i in range((RB // SEG).bit_length() - 1, -1, -1))
P_ROWS_MAX = T * TOP_K + NB * N_EXP * (SEG - 1) + N_EXP * (TM_E - SEG)
N_ETILES = -(-P_ROWS_MAX // TM_E)
P_ROWS = N_ETILES * TM_E


def _cparams(sem):
    return pltpu.CompilerParams(dimension_semantics=sem, vmem_limit_bytes=VMEM_LIMIT)


def _silu(x):
    return x * jax.nn.sigmoid(x)


def _bdot(a, b):
    return jnp.dot(a.astype(BF16), b.astype(BF16), preferred_element_type=F32)


def _bdot_nt(a, b):
    return lax.dot_general(a.astype(BF16), b.astype(BF16), (((1,), (1,)), ((), ())),
                           preferred_element_type=F32)


def _mod_row(i, tm):
    start = i * tm
    return jnp.where(start < NP_TOK, 0, 1 + (start - NP_TOK) // DEC_SEQ)


def _mod_vec(mod_ref, r, k):
    return mod_ref[pl.ds(r, 1), k * D:(k + 1) * D]


def _rms(x, w):
    return x * lax.rsqrt(jnp.mean(x * x, axis=-1, keepdims=True) + EPS) * w


def _two_src_specs(tm, width=D):
    npt = NP_TOK // tm
    return [pl.BlockSpec((tm, width), lambda i, *_: (jnp.minimum(i, npt - 1), 0)),
            pl.BlockSpec((tm, width), lambda i, *_: (jnp.maximum(i - npt, 0), 0))]


def _pick(i, tm, p_ref, s_ref):
    return jnp.where(i * tm < NP_TOK, p_ref[...], s_ref[...])


def _mod_kernel(c_ref, w_ref, b_ref, o_ref):
    o_ref[0] = _bdot(_silu(c_ref[...]), w_ref[0]) + b_ref[0]


def _mod_table(cvec, mod_w, mod_b):
    tn = 1536
    return pl.pallas_call(
        _mod_kernel,
        grid=(DEPTH, N_MOD * D // tn),
        in_specs=[pl.BlockSpec((16, D), lambda l, n: (0, 0)),
                  pl.BlockSpec((1, D, tn), lambda l, n: (l, 0, n)),
                  pl.BlockSpec((1, 1, tn), lambda l, n: (l, 0, n))],
        out_specs=pl.BlockSpec((1, 16, tn), lambda l, n: (l, 0, n)),
        out_shape=jax.ShapeDtypeStruct((DEPTH, 16, N_MOD * D), F32),
        compiler_params=_cparams(("arbitrary", "arbitrary")),
        name="mod_table",
    )(cvec, mod_w, mod_b.reshape(DEPTH, 1, N_MOD * D))


def _inproj_kernel(xp_ref, xs_ref, mod_ref, nw_ref, w_ref, wx_ref, of_ref, ob_ref, ox_ref, h_ref, *, tm, nf):
    i = pl.program_id(0)
    n = pl.program_id(1)

    @pl.when(n == 0)
    def _():
        x = _pick(i, tm, xp_ref, xs_ref)
        r = _mod_row(i, tm)
        h = _rms(x, nw_ref[...]) * (1.0 + _mod_vec(mod_ref, r, 1)) + _mod_vec(mod_ref, r, 0)
        hb = h.astype(BF16)
        h_ref[...] = hb
        ox_ref[...] = jnp.dot(hb, wx_ref[...], preferred_element_type=F32)

    @pl.when(n < nf)
    def _():
        of_ref[...] = jnp.dot(h_ref[...], w_ref[...], preferred_element_type=F32)

    @pl.when(n >= nf)
    def _():
        ob_ref[...] = jnp.dot(h_ref[...], w_ref[...], preferred_element_type=F32).astype(BF16)


def _inproj(xp, xs, mod_l, nw, w, wx, *, tm, tn):
    nx = wx.shape[1]
    nf, nb = PROJ_F // tn, PROJ_B // tn
    return pl.pallas_call(
        functools.partial(_inproj_kernel, tm=tm, nf=nf),
        grid=(T // tm, nf + nb),
        in_specs=_two_src_specs(tm) + [
            pl.BlockSpec((16, N_MOD * D), lambda i, n: (0, 0)),
            pl.BlockSpec((1, D), lambda i, n: (0, 0)),
            pl.BlockSpec((D, tn), lambda i, n: (0, n)),
            pl.BlockSpec((D, nx), lambda i, n: (0, 0))],
        out_specs=[pl.BlockSpec((tm, tn), lambda i, n: (i, jnp.minimum(n, nf - 1))),
                   pl.BlockSpec((tm, tn), lambda i, n: (i, jnp.maximum(n - nf, 0))),
                   pl.BlockSpec((tm, nx), lambda i, n: (i, 0))],
        out_shape=[jax.ShapeDtypeStruct((T, PROJ_F), F32), jax.ShapeDtypeStruct((T, PROJ_B), BF16),
                   jax.ShapeDtypeStruct((T, nx), F32)],
        scratch_shapes=[pltpu.VMEM((tm, D), BF16)],
        compiler_params=_cparams(("arbitrary", "arbitrary")),
        name="inproj",
    )(xp, xs, mod_l, nw, w, wx)


def _tri(n, upper):
    r = lax.broadcasted_iota(I32, (n, n), 0)
    c = lax.broadcasted_iota(I32, (n, n), 1)
    return (c >= r) if upper else (c <= r)


def _conv_silu(x, w, b, L):
    row = lax.broadcasted_iota(I32, (L, 1), 0)
    acc = x * w[2:3] + b
    for k in (0, 1, 3, 4):
        off = k - 2
        shifted = pltpu.roll(x, (-off) % L, axis=0)
        ok = (row + off >= 0) & (row + off < L)
        acc = acc + jnp.where(ok, shifted, 0.0) * w[k:k + 1]
    return _silu(acc)


def _split_hi_lo(x):
    hi = x.astype(BF16)
    return hi, (x - hi.astype(F32)).astype(BF16)


def _ssd_kernel(*refs, L, has_init, want_state):
    (x_ref, b_ref, c_ref, dt_ref, cwx_ref, cwb_ref, cwc_ref, cbx_ref, cbb_ref, cbc_ref,
     dtb_ref, alog_ref, dcol_ref, rsel_ref) = refs[:14]
    pos = 14
    s0_ref = None
    if has_init:
        s0_ref = refs[pos]
        pos += 1
    y_ref = refs[pos]
    pos += 1
    st_ref = None
    if want_state:
        st_ref = refs[pos]
        pos += 1
    xs_s, bs_s, cs_s, y_s = refs[pos:pos + 4]

    tc = SSD_TC
    nc = L // tc
    xs_s[...] = _conv_silu(x_ref[...], cwx_ref[...], cbx_ref[...], L)
    bs_s[...] = _conv_silu(b_ref[...], cwb_ref[...], cbb_ref[...], L)
    cs_s[...] = _conv_silu(c_ref[...], cwc_ref[...], cbc_ref[...], L)
    y_s[...] = xs_s[...] * dcol_ref[...]

    z = dt_ref[...] + dtb_ref[...]
    dt_all = jnp.maximum(z, 0.0) + jnp.log(1.0 + jnp.exp(-jnp.abs(z)))
    a_row = -jnp.exp(alog_ref[...])

    dirs = (0, 1)
    tris = [_tri(tc, False), _tri(tc, True)]
    tri_b = [t.astype(BF16) for t in tris]
    sts = [s0_ref[d].reshape(GW, SSD_N).T if has_init else None for d in dirs]
    for ci in range(nc):
        cidx = (ci, nc - 1 - ci)
        sls = [slice(c * tc, (c + 1) * tc) for c in cidx]
        x = [xs_s[s] for s in sls]
        bm = [bs_s[s] for s in sls]
        cm = [cs_s[s] for s in sls]
        dt = [dt_all[s] for s in sls]
        parts = [_split_hi_lo(dt[d] * a_row) for d in dirs]
        cum = [jnp.dot(tri_b[d], parts[d][0], preferred_element_type=F32)
               + jnp.dot(tri_b[d], parts[d][1], preferred_element_type=F32) for d in dirs]
        cum_t = [cum[d].T for d in dirs]
        end = [cum[0][tc - 1:tc], cum[1][0:1]]
        ex = [_bdot(jnp.concatenate(
            [dt[d], jnp.exp(cum[d]), jnp.exp(end[d] - cum[d]),
             jnp.broadcast_to(jnp.exp(end[d]), (SUBLANES, LANES))], axis=0), rsel_ref[d])
            for d in dirs]
        xdt = [x[d] * ex[d][0:tc] for d in dirs]
        cb = [_bdot_nt(cm[0], bm[0])]
        cb.append(cb[0] if cidx[1] == cidx[0] else _bdot_nt(cm[1], bm[1]))
        lms = [[jnp.exp(jnp.where(tris[d], cum[d][:, d * HPG + h:d * HPG + h + 1]
                                  - cum_t[d][d * HPG + h:d * HPG + h + 1, :], -jnp.inf))
                for h in range(HPG)] for d in dirs]
        ys = [[_bdot(cb[d] * lms[d][h], xdt[d][:, h * SSD_P:(h + 1) * SSD_P]) for h in range(HPG)]
              for d in dirs]
        y = [jnp.concatenate(ys[d], axis=1) for d in dirs]
        for d in dirs:
            if sts[d] is not None:
                y[d] = y[d] + _bdot(cm[d], sts[d]) * ex[d][tc:2 * tc]
        if cidx[0] == cidx[1]:
            y_s[sls[0]] = y_s[sls[0]] + (y[0] + y[1])
        else:
            for d in dirs:
                y_s[sls[d]] = y_s[sls[d]] + y[d]
        sc = [_bdot(bm[d].T, xdt[d] * ex[d][2 * tc:3 * tc]) for d in dirs]
        sts = [sc[d] if sts[d] is None else sts[d] * ex[d][3 * tc:3 * tc + 1] + sc[d] for d in dirs]
    if want_state:
        for d in dirs:
            st_ref[d] = sts[d].T.reshape(HPG, SSD_P, SSD_N)

    y_ref[...] = y_s[...]


def _ssd_rsel():
    r = np.zeros((2, LANES, GW), np.float32)
    for d in range(2):
        for h in range(HPG):
            r[d, d * HPG + h, h * SSD_P:(h + 1) * SSD_P] = 1.0
    return jnp.asarray(r, BF16)


def _ssd_call(proj, dtp, prm, *, L, n_seq, row0, s0, want_state):
    rb = row0 // L
    has_init = s0 is not None
    xbc = COL_XBC
    in_specs = [
        pl.BlockSpec((L, GW), lambda b, g: (rb + b, xbc // GW + g)),
        pl.BlockSpec((L, SSD_N), lambda b, g: (rb + b, (xbc + SSD_INNER) // SSD_N + g)),
        pl.BlockSpec((L, SSD_N), lambda b, g: (rb + b, (xbc + SSD_INNER + SSD_G * SSD_N) // SSD_N + g)),
        pl.BlockSpec((L, LANES), lambda b, g: (rb + b, g)),
        pl.BlockSpec((SSD_CONV, GW), lambda b, g: (0, g)),
        pl.BlockSpec((SSD_CONV, SSD_N), lambda b, g: (0, g)),
        pl.BlockSpec((SSD_CONV, SSD_N), lambda b, g: (0, g)),
        pl.BlockSpec((1, GW), lambda b, g: (0, g)),
        pl.BlockSpec((1, SSD_N), lambda b, g: (0, g)),
        pl.BlockSpec((1, SSD_N), lambda b, g: (0, g)),
        pl.BlockSpec((1, LANES), lambda b, g: (0, g)),
        pl.BlockSpec((1, LANES), lambda b, g: (0, g)),
        pl.BlockSpec((1, GW), lambda b, g: (0, g)),
        pl.BlockSpec((2, LANES, GW), lambda b, g: (0, 0, 0)),
    ]
    args = [proj, proj, proj, dtp, prm["cwx"], prm["cwb"], prm["cwc"], prm["cbx"], prm["cbb"], prm["cbc"],
            prm["dtb"], prm["alog"], prm["dcol"], prm["rsel"]]
    if has_init:
        in_specs.append(pl.BlockSpec((None, 2, HPG, SSD_P, SSD_N), lambda b, g: (b, 0, g, 0, 0)))
        args.append(s0)
    out_specs = [pl.BlockSpec((L, GW), lambda b, g: (b, g))]
    out_shape = [jax.ShapeDtypeStruct((n_seq * L, SSD_INNER), F32)]
    if want_state:
        out_specs.append(pl.BlockSpec((None, 2, HPG, SSD_P, SSD_N), lambda b, g: (b, 0, g, 0, 0)))
        out_shape.append(jax.ShapeDtypeStruct((n_seq, 2, SSD_HEADS, SSD_P, SSD_N), F32))
    return pl.pallas_call(
        functools.partial(_ssd_kernel, L=L, has_init=has_init, want_state=want_state),
        grid=(n_seq, SSD_G),
        in_specs=in_specs, out_specs=out_specs, out_shape=out_shape,
        scratch_shapes=[pltpu.VMEM((L, GW), F32), pltpu.VMEM((L, SSD_N), F32),
                        pltpu.VMEM((L, SSD_N), F32), pltpu.VMEM((L, GW), F32)],
        compiler_params=_cparams(("arbitrary", "arbitrary")),
        name=f"ssd_L{L}",
    )(*args)


def _gla_kernel(*refs, L, layer, has_init, want_state, hpb):
    q_ref, ff_ref, fb_ref, v_ref, g_ref, lb_ref, nw_ref = refs[:7]
    pos = 7
    s0_ref = None
    if has_init:
        s0_ref = refs[pos]
        pos += 1
    o_ref = refs[pos]
    pos += 1
    st_ref = None
    if want_state:
        st_ref = refs[pos]
        pos += 1
    qs, ks, ls, os_ = refs[pos:pos + 4]

    tc, gr = GLA_TC, GLA_GROUP
    ng, cpg = L // gr, gr // tc
    lbr = lb_ref[...]
    e = jnp.exp(lbr - jnp.max(lbr, axis=0, keepdims=True))
    sm = e / jnp.sum(e, axis=0, keepdims=True)
    lb = sm[0]
    for j in range(1, layer + 1):
        lb = lb + sm[j]

    qs[...] = _silu(q_ref[...])
    for d, fr in ((0, ff_ref), (1, fb_ref)):
        logit = fr[...]
        lbd = lb[d:d + 1]
        f = lbd + (1.0 - lbd) * jax.nn.sigmoid(logit)
        ks[d] = (1.0 - lbd) * jax.nn.sigmoid(-logit)
        ls[d] = jnp.log(f)

    r_i = lax.broadcasted_iota(I32, (gr, gr), 0)
    c_i = lax.broadcasted_iota(I32, (gr, gr), 1)
    sh = tc.bit_length() - 1
    same = (r_i >> sh) == (c_i >> sh)
    causal = [same & (c_i <= r_i), same & (c_i >= r_i)]
    bd = [m.astype(BF16) for m in causal]
    row_chunk = lax.broadcasted_iota(I32, (gr, 1), 0) >> sh
    col_chunk = lax.broadcasted_iota(I32, (1, gr), 1) >> sh

    def rows_of(x, r):
        return jnp.concatenate(
            [jnp.broadcast_to(x[c * tc + r:c * tc + r + 1], (tc, x.shape[1])) for c in range(cpg)], axis=0)

    def groups(units):
        n = range(len(units))
        ds_ = [u[0] for u in units]
        sl = [(pl.ds(u[2], gr), slice(u[1] * HG_K, (u[1] + 1) * HG_K)) for u in units]
        sts = [u[3] for u in units]
        parts = [_split_hi_lo(ls[ds_[i], sl[i][0], sl[i][1]]) for i in n]
        cums = [jnp.dot(bd[ds_[i]], parts[i][0], preferred_element_type=F32)
                + jnp.dot(bd[ds_[i]], parts[i][1], preferred_element_type=F32) for i in n]
        q = [qs[sl[i][0], sl[i][1]] for i in n]
        k = [ks[ds_[i], sl[i][0], sl[i][1]] for i in n]
        v = [v_ref[sl[i][0], sl[i][1]] for i in n]
        ref_b = [rows_of(cums[i], tc // 2 if ds_[i] else tc // 2 - 1) for i in n]
        end_b = [rows_of(cums[i], 0 if ds_[i] else tc - 1) for i in n]
        att = [_bdot_nt(q[i] * jnp.exp(cums[i] - ref_b[i]), k[i] * jnp.exp(ref_b[i] - cums[i])) for i in n]
        o = [_bdot(jnp.where(causal[ds_[i]], att[i], 0.0), v[i]) for i in n]
        kd_t = [(k[i] * jnp.exp(end_b[i] - cums[i])).T for i in n]
        inc = [_bdot(jnp.concatenate([jnp.where(col_chunk == c, kd_t[i], 0.0) for c in range(cpg)], axis=0), v[i])
               for i in n]
        dec_t = [jnp.exp(end_b[i]).T for i in n]
        stacks = []
        for i in n:
            st, s_in = sts[i], [None] * cpg
            for c in (range(cpg - 1, -1, -1) if ds_[i] else range(cpg)):
                s_in[c] = st
                st = st * dec_t[i][:, c * tc:c * tc + 1] + inc[i][c * HG_K:(c + 1) * HG_K]
            sts[i] = st
            stacks.append(jnp.concatenate(s_in, axis=0))
        qe = [q[i] * jnp.exp(cums[i]) for i in n]
        o = [o[i] + _bdot(jnp.concatenate([jnp.where(row_chunk == c, qe[i], 0.0) for c in range(cpg)], axis=1),
                          stacks[i]) for i in n]
        return o, sts

    def init(d, j):
        return s0_ref[d, j] if has_init else jnp.zeros((HG_K, HG_V), F32)

    heads = range(hpb)
    if ng == 1:
        o, sts = groups([(d, j, 0, init(d, j)) for j in heads for d in (0, 1)])
        outs = [o[2 * j] + o[2 * j + 1] for j in heads]
        finals = [(sts[2 * j], sts[2 * j + 1]) for j in heads]
    else:
        os_[...] = jnp.zeros_like(os_)

        def body(i, carry):
            g0 = (pl.multiple_of(i * gr, gr), pl.multiple_of((ng - 1 - i) * gr, gr))
            o, sts = groups([(d, j, g0[d], carry[j][d]) for j in heads for d in (0, 1)])
            for j in heads:
                cs = slice(j * HG_V, (j + 1) * HG_V)
                for d in (0, 1):
                    os_[pl.ds(g0[d], gr), cs] = os_[pl.ds(g0[d], gr), cs] + o[2 * j + d]
            return tuple((sts[2 * j], sts[2 * j + 1]) for j in heads)

        finals = lax.fori_loop(0, ng, body, tuple((init(0, j), init(1, j)) for j in heads))
        outs = [os_[:, j * HG_V:(j + 1) * HG_V] for j in heads]
    for j in heads:
        cs = slice(j * HG_V, (j + 1) * HG_V)
        if want_state:
            st_ref[0, j] = finals[j][0]
            st_ref[1, j] = finals[j][1]
        o_ref[:, cs] = (_rms(outs[j], nw_ref[:, cs]) * _silu(g_ref[:, cs].astype(F32))).astype(BF16)


def _gla_call(proj, projb, hg_lb, hg_nw, *, layer, L, n_seq, row0, s0, want_state, hpb):
    rb = row0 // L
    has_init = s0 is not None
    w = hpb * HG_K

    def col(c0):
        return lambda b, h: (rb + b, c0 // w + h)

    st_spec = pl.BlockSpec((None, 2, hpb, HG_K, HG_V), lambda b, h: (b, 0, h, 0, 0))
    in_specs = [
        pl.BlockSpec((L, w), col(COL_HQ)),
        pl.BlockSpec((L, w), col(COL_HF)),
        pl.BlockSpec((L, w), col(COL_HF + HG_QK)),
        pl.BlockSpec((L, w), col(COL_HI)),
        pl.BlockSpec((L, w), col(COL_HG)),
        pl.BlockSpec((DEPTH + 1, 2, w), lambda b, h: (0, 0, h)),
        pl.BlockSpec((1, w), lambda b, h: (0, h)),
    ]
    args = [proj, proj, proj, projb, projb, hg_lb, hg_nw]
    if has_init:
        in_specs.append(st_spec)
        args.append(s0)
    out_specs = [pl.BlockSpec((L, w), lambda b, h: (b, h))]
    out_shape = [jax.ShapeDtypeStruct((n_seq * L, HG_IV), BF16)]
    if want_state:
        out_specs.append(st_spec)
        out_shape.append(jax.ShapeDtypeStruct((n_seq, 2, HG_HEADS, HG_K, HG_V), F32))
    return pl.pallas_call(
        functools.partial(_gla_kernel, L=L, layer=layer, has_init=has_init, want_state=want_state, hpb=hpb),
        grid=(n_seq, HG_HEADS // hpb),
        in_specs=in_specs, out_specs=out_specs, out_shape=out_shape,
        scratch_shapes=[pltpu.VMEM((L, w), F32), pltpu.VMEM((2, L, w), F32),
                        pltpu.VMEM((2, L, w), F32), pltpu.VMEM((L, w), F32)],
        compiler_params=_cparams(("arbitrary", "arbitrary")),
        name=f"gla_L{L}",
    )(*args)


def _ssm_out_kernel(xp_ref, xs_ref, yp_ref, ys_ref, op_ref, os_ref, z_ref, mod_ref, nw_ref, w_ref,
                    out_ref, *, tm):
    i = pl.program_id(0)
    g = _pick(i, tm, yp_ref, ys_ref) * _silu(z_ref[...].astype(F32))
    mix = (_bdot(_rms(g, nw_ref[...]), w_ref[0:SSD_INNER, :])
           + _bdot(_pick(i, tm, op_ref, os_ref), w_ref[SSD_INNER:, :]))
    out_ref[...] = _pick(i, tm, xp_ref, xs_ref) + _mod_vec(mod_ref, _mod_row(i, tm), 2) * mix


def _ssm_out(xp, xs, yp, ys, op, os_, proj, mod_l, nw, w, *, tm):
    return pl.pallas_call(
        functools.partial(_ssm_out_kernel, tm=tm),
        grid=(T // tm,),
        in_specs=_two_src_specs(tm) + _two_src_specs(tm, SSD_INNER) + _two_src_specs(tm, HG_IV) + [
            pl.BlockSpec((tm, SSD_INNER), lambda i: (i, COL_Z // SSD_INNER)),
            pl.BlockSpec((16, N_MOD * D), lambda i: (0, 0)),
            pl.BlockSpec((1, SSD_INNER), lambda i: (0, 0)),
            pl.BlockSpec((SSD_INNER + HG_IV, D), lambda i: (0, 0))],
        out_specs=pl.BlockSpec((tm, D), lambda i: (i, 0)),
        out_shape=jax.ShapeDtypeStruct((T, D), F32),
        compiler_params=_cparams(("arbitrary",)),
        name="ssm_out",
    )(xp, xs, yp, ys, op, os_, proj, mod_l, nw, w)


def _attn_out_kernel(x_ref, ap_ref, as_ref, mod_ref, w_ref, out_ref, *, tm):
    i = pl.program_id(0)
    mix = jnp.dot(_pick(i, tm, ap_ref, as_ref), w_ref[...], preferred_element_type=F32)
    out_ref[...] = x_ref[...] + _mod_vec(mod_ref, _mod_row(i, tm), 2) * mix


def _attn_out(x, ap, as_, mod_l, w, *, tm):
    return pl.pallas_call(
        functools.partial(_attn_out_kernel, tm=tm),
        grid=(T // tm,),
        in_specs=[pl.BlockSpec((tm, D), lambda i: (i, 0))] + _two_src_specs(tm) + [
            pl.BlockSpec((16, N_MOD * D), lambda i: (0, 0)),
            pl.BlockSpec((D, D), lambda i: (0, 0))],
        out_specs=pl.BlockSpec((tm, D), lambda i: (i, 0)),
        out_shape=jax.ShapeDtypeStruct((T, D), F32),
        compiler_params=_cparams(("arbitrary",)),
        name="attn_out",
    )(x, ap, as_, mod_l, w)


def _route_kernel(x_ref, mod_ref, nw_ref, rw_ref, rb_ref, h_ref, meta_ref, metat_ref, cnt_ref):
    i = pl.program_id(0)
    r = _mod_row(i, TB)
    h = _rms(x_ref[...], nw_ref[...]) * (1.0 + _mod_vec(mod_ref, r, 4)) + _mod_vec(mod_ref, r, 3)
    hb = h.astype(BF16)
    h_ref[...] = hb
    logits = jnp.dot(hb, rw_ref[...], preferred_element_type=F32) + rb_ref[...]
    lane = lax.broadcasted_iota(I32, (TB, LANES), 1)
    lane_f = lane.astype(F32)
    vals, hots, idxs = [], [], []
    for _ in range(TOP_K):
        m = jnp.max(logits, axis=-1, keepdims=True)
        idx = jnp.min(jnp.where(logits == m, lane_f, float(LANES)), axis=-1, keepdims=True)
        hot = lane_f == idx
        vals.append(m)
        idxs.append(idx)
        hots.append(hot)
        logits = jnp.where(hot, -jnp.inf, logits)
    es = [jnp.exp(v - vals[0]) for v in vals]
    den = es[0] + es[1] + es[2] + es[3]

    r_i = lax.broadcasted_iota(I32, (TB, TB), 0)
    c_i = lax.broadcasted_iota(I32, (TB, TB), 1)
    below = (c_i < r_i).astype(BF16)
    base = jnp.zeros((1, LANES), F32)
    ranks = []
    for k in range(TOP_K):
        hot_f = hots[k].astype(F32)
        before = jnp.dot(below, hots[k].astype(BF16), preferred_element_type=F32)
        ranks.append(base + before)
        base = base + jnp.sum(hot_f, axis=0, keepdims=True)
    units = jnp.floor((base + (SEG - 1)) * (1.0 / SEG))
    lr = lax.broadcasted_iota(I32, (LANES, LANES), 0)
    lc = lax.broadcasted_iota(I32, (LANES, LANES), 1)
    seg_start = SEG * jnp.dot(jnp.broadcast_to(units, (SUBLANES, LANES)).astype(BF16),
                              (lr < lc).astype(BF16), preferred_element_type=F32)[0:1]
    meta = jnp.zeros((TB, LANES), F32)
    for k in range(TOP_K):
        row = jnp.sum(hots[k].astype(F32) * (seg_start + ranks[k]), axis=-1, keepdims=True)
        meta = jnp.where(lane == k, idxs[k], meta)
        meta = jnp.where(lane == TOP_K + k, es[k] / den, meta)
        meta = jnp.where(lane == 2 * TOP_K + k, row, meta)
    meta_ref[...] = meta
    metat_ref[...] = meta.T
    cnt_ref[...] = base


def _route(x, mod_l, nw, rw, rb):
    return pl.pallas_call(
        _route_kernel,
        grid=(NB,),
        in_specs=[pl.BlockSpec((TB, D), lambda i: (i, 0)),
                  pl.BlockSpec((16, N_MOD * D), lambda i: (0, 0)),
                  pl.BlockSpec((1, D), lambda i: (0, 0)),
                  pl.BlockSpec((D, LANES), lambda i: (0, 0)),
                  pl.BlockSpec((1, LANES), lambda i: (0, 0))],
        out_specs=[pl.BlockSpec((TB, D), lambda i: (i, 0)),
                   pl.BlockSpec((TB, LANES), lambda i: (i, 0)),
                   pl.BlockSpec((LANES, TB), lambda i: (0, i)),
                   pl.BlockSpec((None, 1, LANES), lambda i: (i, 0, 0))],
        out_shape=[jax.ShapeDtypeStruct((T, D), BF16),
                   jax.ShapeDtypeStruct((T, LANES), F32),
                   jax.ShapeDtypeStruct((LANES, T), F32),
                   jax.ShapeDtypeStruct((NB, 1, LANES), F32)],
        compiler_params=_cparams(("arbitrary",)),
        name="moe_route",
    )(x, mod_l, nw, rw, rb)


def _copy(src, dst, sem):
    return pltpu.make_async_copy(src, dst, sem)


def _segment_copies(n, src, dst, sizes, make, wait):
    src = jnp.asarray(src, I32)
    dst = jnp.asarray(dst, I32)
    for size in sizes:
        take = n & size

        @pl.when(take != 0)
        def _(src=src, dst=dst, size=size):
            cp = make(pl.multiple_of(src, SEG), pl.multiple_of(dst, SEG), size)
            if wait:
                cp.wait()
            else:
                cp.start()

        src = src + take
        dst = dst + take


def _sort_push_kernel(tab_ref, tabp_ref, ztab_ref, nv_ref, metat_ref, h_ref, xs_hbm, buf, zb, sem, zsem):
    b = pl.program_id(0)
    slot = b & 1

    def zero_copies(wait):
        def per_expert(e, carry):
            _segment_copies(ztab_ref[1, e], 0, ztab_ref[0, e], PAD_SIZES,
                            lambda s, d, size: _copy(zb.at[pl.ds(0, size)], xs_hbm.at[pl.ds(d, size)], zsem), wait)
            return carry

        lax.fori_loop(0, N_EXP, per_expert, 0)

        def per_tile(j, carry):
            cp = _copy(zb, xs_hbm.at[pl.ds(pl.multiple_of(j * TM_E, TM_E), TM_E)], zsem)
            if wait:
                cp.wait()
            else:
                cp.start()
            return carry

        lax.fori_loop(nv_ref[0], N_ETILES, per_tile, 0)

    @pl.when(b == 0)
    def _():
        zb[...] = jnp.zeros_like(zb)
        zero_copies(False)

    ch = 256
    for c in range(RB // ch):
        rows = (lax.broadcasted_iota(I32, (ch, TB), 0) + c * ch).astype(F32)
        hit = rows == metat_ref[2 * TOP_K:2 * TOP_K + 1, :]
        for k in range(1, TOP_K):
            hit = hit | (rows == metat_ref[2 * TOP_K + k:2 * TOP_K + k + 1, :])
        buf[slot, c * ch:(c + 1) * ch, :] = jnp.dot(hit.astype(BF16), h_ref[...], preferred_element_type=F32)

    def push(t_ref, s, wait):
        def per_expert(e, carry):
            _segment_copies(
                t_ref[0, e], t_ref[1, e], t_ref[2, e], SEG_SIZES,
                lambda so, do, size: _copy(buf.at[s, pl.ds(so, size)], xs_hbm.at[pl.ds(do, size)], sem.at[s]), wait)
            return carry

        lax.fori_loop(0, N_EXP, per_expert, 0)

    def drain(t_ref, s):
        _segment_copies(t_ref[1, N_EXP - 1] + t_ref[0, N_EXP - 1], 0, 0, BUF_SIZES,
                        lambda so, do, size: _copy(buf.at[s, pl.ds(0, size)], xs_hbm.at[pl.ds(0, size)], sem.at[s]),
                        True)

    @pl.when(b > 0)
    def _():
        drain(tabp_ref, 1 - slot)

    push(tab_ref, slot, False)

    @pl.when(b == NB - 1)
    def _():
        drain(tab_ref, slot)

    @pl.when(b == 0)
    def _():
        zero_copies(True)


def _sort_push(tab, ztab, n_valid, metat, h):
    return pl.pallas_call(
        _sort_push_kernel,
        grid=(NB,),
        in_specs=[pl.BlockSpec((None, 3, N_EXP), lambda b: (b, 0, 0), memory_space=pltpu.SMEM),
                  pl.BlockSpec((None, 3, N_EXP), lambda b: (jnp.maximum(b - 1, 0), 0, 0),
                               memory_space=pltpu.SMEM),
                  pl.BlockSpec(memory_space=pltpu.SMEM),
                  pl.BlockSpec(memory_space=pltpu.SMEM),
                  pl.BlockSpec((LANES, TB), lambda b: (0, b)),
                  pl.BlockSpec((TB, D), lambda b: (b, 0))],
        out_specs=pl.BlockSpec(memory_space=pl.ANY),
        out_shape=jax.ShapeDtypeStruct((P_ROWS, D), F32),
        scratch_shapes=[pltpu.VMEM((2, RB, D), F32), pltpu.VMEM((TM_E, D), F32),
                        pltpu.SemaphoreType.DMA((2,)), pltpu.SemaphoreType.DMA(())],
        compiler_params=_cparams(("arbitrary",)),
        name="moe_sort_push",
    )(tab, tab, ztab, n_valid, metat, h)


def _expert_kernel(te_ref, tr_ref, nv_ref, par_ref, nxt_ref, x_ref, w1_hbm, b1_ref, w2_hbm, b2_ref, y_ref,
                   wf1, wf2, w1_s, w2_s, sem, *, layer):
    j = pl.program_id(0)
    valid = j < nv_ref[0]
    fresh = jnp.logical_or(j == 0, te_ref[j] != te_ref[jnp.maximum(j - 1, 0)])

    def fetch(e, s):
        return (_copy(w1_hbm.at[layer, e], wf1.at[s], sem.at[0, s]),
                _copy(w2_hbm.at[layer, e], wf2.at[s], sem.at[1, s]))

    @pl.when(jnp.logical_and(valid, fresh))
    def _():
        s = par_ref[j]

        @pl.when(j == 0)
        def _():
            for cp in fetch(te_ref[0], 0):
                cp.start()

        for cp in fetch(te_ref[j], s):
            cp.wait()

        @pl.when(nxt_ref[j] >= 0)
        def _():
            for cp in fetch(nxt_ref[j], 1 - s):
                cp.start()

        w1_s[...] = wf1[s].astype(BF16)
        w2_s[...] = wf2[s].astype(BF16)

    @pl.when(valid)
    def _():
        rows = lax.broadcasted_iota(I32, (TM_E, 1), 0)
        x = jnp.where(rows < tr_ref[j], x_ref[...], 0.0).astype(BF16)
        gu = jnp.dot(x, w1_s[...], preferred_element_type=F32) + b1_ref[...]
        gate = jnp.minimum(gu[:, :FF], SWIGLU_LIMIT)
        up = jnp.clip(gu[:, FF:], -SWIGLU_LIMIT, SWIGLU_LIMIT)
        act = (up + 1.0) * gate * jax.nn.sigmoid(SWIGLU_ALPHA * gate)
        y_ref[...] = jnp.dot(act.astype(BF16), w2_s[...], preferred_element_type=F32) + b2_ref[...]

    @pl.when(jnp.logical_not(valid))
    def _():
        y_ref[...] = jnp.zeros_like(y_ref)


def _experts(layer, tile_exp, tile_rows, n_valid, run_par, run_next, xs, w1, b1, w2, b2):
    def wmap(j, te, *_):
        return (layer, te[j], 0, 0)

    return pl.pallas_call(
        functools.partial(_expert_kernel, layer=layer),
        grid_spec=pltpu.PrefetchScalarGridSpec(
            num_scalar_prefetch=5,
            grid=(N_ETILES,),
            in_specs=[pl.BlockSpec((TM_E, D), lambda j, *_: (j, 0)),
                      pl.BlockSpec(memory_space=pl.ANY),
                      pl.BlockSpec((None, None, 1, 2 * FF), wmap),
                      pl.BlockSpec(memory_space=pl.ANY),
                      pl.BlockSpec((None, None, 1, D), wmap)],
            out_specs=pl.BlockSpec((TM_E, D), lambda j, *_: (j, 0)),
            scratch_shapes=[pltpu.VMEM((2, D, 2 * FF), F32), pltpu.VMEM((2, FF, D), F32),
                            pltpu.VMEM((D, 2 * FF), BF16), pltpu.VMEM((FF, D), BF16),
                            pltpu.SemaphoreType.DMA((2, 2))]),
        out_shape=jax.ShapeDtypeStruct((P_ROWS, D), F32),
        compiler_params=_cparams(("arbitrary",)),
        name="moe_experts",
    )(tile_exp, tile_rows, n_valid, run_par, run_next, xs, w1, b1.reshape(DEPTH, N_EXP, 1, 2 * FF), w2,
      b2.reshape(DEPTH, N_EXP, 1, D))


def _combine_kernel(*refs, final):
    if final:
        tab_ref, tabn_ref, y_hbm, x_ref, meta_ref, mod_ref, fw_ref, outp_ref, outs_ref, buf, sem = refs
    else:
        tab_ref, tabn_ref, y_hbm, x_ref, meta_ref, mod_ref, out_ref, buf, sem = refs
    b = pl.program_id(0)
    slot = b & 1

    def pull(t_ref, s, wait):
        def per_expert(e, carry):
            _segment_copies(
                t_ref[0, e], t_ref[2, e], t_ref[1, e], SEG_SIZES,
                lambda so, do, size: _copy(y_hbm.at[pl.ds(so, size)], buf.at[s, pl.ds(do, size)], sem.at[s]), wait)
            return carry

        lax.fori_loop(0, N_EXP, per_expert, 0)

    @pl.when(b == 0)
    def _():
        pull(tab_ref, slot, False)

    @pl.when(b + 1 < NB)
    def _():
        pull(tabn_ref, 1 - slot, False)

    used = tab_ref[1, N_EXP - 1] + tab_ref[0, N_EXP - 1]
    _segment_copies(used, 0, 0, BUF_SIZES,
                    lambda so, do, size: _copy(y_hbm.at[pl.ds(0, size)], buf.at[slot, pl.ds(0, size)], sem.at[slot]),
                    True)

    meta = meta_ref[...]
    ch = 256
    acc = jnp.zeros((TB, D), F32)
    for c in range(RB // ch):
        cols = (lax.broadcasted_iota(I32, (TB, ch), 1) + c * ch).astype(F32)
        g = jnp.zeros((TB, ch), F32)
        for k in range(TOP_K):
            g = g + jnp.where(cols == meta[:, 2 * TOP_K + k:2 * TOP_K + k + 1],
                              meta[:, TOP_K + k:TOP_K + k + 1], 0.0)
        rows = lax.broadcasted_iota(I32, (ch, 1), 0) + c * ch
        y = jnp.where(rows < used, buf[slot, c * ch:(c + 1) * ch, :], 0.0)
        acc = acc + _bdot(g, y)
    out = x_ref[...] + _mod_vec(mod_ref, _mod_row(b, TB), 5) * acc
    if final:
        y_out = _rms(out, fw_ref[...])

        @pl.when(b < NP_TOK // TB)
        def _():
            outp_ref[...] = y_out

        @pl.when(b >= NP_TOK // TB)
        def _():
            outs_ref[...] = y_out
    else:
        out_ref[...] = out


def _combine(tab, y, x, meta, mod_l, final_w=None):
    final = final_w is not None
    nbp = NP_TOK // TB
    in_specs = [pl.BlockSpec((None, 3, N_EXP), lambda b: (b, 0, 0), memory_space=pltpu.SMEM),
                pl.BlockSpec((None, 3, N_EXP), lambda b: (jnp.minimum(b + 1, NB - 1), 0, 0),
                             memory_space=pltpu.SMEM),
                pl.BlockSpec(memory_space=pl.ANY),
                pl.BlockSpec((TB, D), lambda b: (b, 0)),
                pl.BlockSpec((TB, LANES), lambda b: (b, 0)),
                pl.BlockSpec((16, N_MOD * D), lambda b: (0, 0))]
    args = [tab, tab, y, x, meta, mod_l]
    if final:
        in_specs.append(pl.BlockSpec((1, D), lambda b: (0, 0)))
        args.append(final_w)
        out_specs = [pl.BlockSpec((TB, D), lambda b: (jnp.minimum(b, nbp - 1), 0)),
                     pl.BlockSpec((TB, D), lambda b: (jnp.maximum(b - nbp, 0), 0))]
        out_shape = [jax.ShapeDtypeStruct((NP_TOK, D), F32), jax.ShapeDtypeStruct((NS_TOK, D), F32)]
    else:
        out_specs = pl.BlockSpec((TB, D), lambda b: (b, 0))
        out_shape = jax.ShapeDtypeStruct((T, D), F32)
    return pl.pallas_call(
        functools.partial(_combine_kernel, final=final),
        grid=(NB,),
        in_specs=in_specs, out_specs=out_specs, out_shape=out_shape,
        scratch_shapes=[pltpu.VMEM((2, RB, D), F32), pltpu.SemaphoreType.DMA((2,))],
        compiler_params=_cparams(("arbitrary",)),
        name="moe_combine_final" if final else "moe_combine",
    )(*args)


def _moe_layer(layer, x, mod_l, nw, rw, rb, w1, b1, w2, b2, final_w=None):
    rw_p = jnp.zeros((D, LANES), BF16).at[:, :N_EXP].set(rw.astype(BF16))
    rb_p = jnp.full((1, LANES), -jnp.inf, F32).at[0, :N_EXP].set(rb)
    h, meta, metat, cnt = _route(x, mod_l, nw, rw_p, rb_p)

    counts = cnt[:, 0, :N_EXP].astype(I32)
    n_seg = (counts + SEG - 1) // SEG * SEG
    seg_start = jnp.cumsum(n_seg, axis=1) - n_seg
    tot = jnp.sum(n_seg, axis=0)
    padded = (tot + TM_E - 1) // TM_E * TM_E
    ends = jnp.cumsum(padded)
    offs = ends - padded
    glob = offs[None, :] + jnp.cumsum(n_seg, axis=0) - n_seg
    tab = jnp.stack([n_seg, seg_start, glob], axis=1)
    ztab = jnp.stack([offs + tot, padded - tot], axis=0)
    n_valid = ends[-1] // TM_E
    starts = jnp.arange(N_ETILES, dtype=I32) * TM_E
    te = jnp.minimum(jnp.sum((ends[None, :] <= starts[:, None]).astype(I32), axis=1), N_EXP - 1)
    last = jnp.sum((ends[:-1] < ends[-1]).astype(I32))
    te = jnp.where(starts < ends[-1], te, last)
    tr = jnp.clip(offs[te] + tot[te] - starts, 0, TM_E)
    nv = n_valid.reshape(1)
    run_start = jnp.concatenate([jnp.ones((1,), I32), (te[1:] != te[:-1]).astype(I32)])
    run_par = (jnp.cumsum(run_start) - 1) & 1
    eid = jnp.arange(N_EXP, dtype=I32)
    later = (eid[None, :] > eid[:, None]) & (padded[None, :] > 0)
    nxt_e = jnp.min(jnp.where(later, eid[None, :], N_EXP), axis=1)
    run_next = jnp.where(nxt_e < N_EXP, nxt_e, -1)[te]

    xs = _sort_push(tab, ztab, nv, metat, h)
    y = _experts(layer, te, tr, nv, run_par, run_next, xs, w1, b1, w2, b2)
    return _combine(tab, y, x, meta, mod_l, final_w)


def _head_sel():
    g = np.zeros((D, LANES), np.float32)
    for c in range(D):
        g[c, c // HEAD_DIM] = 1.0
    return jnp.asarray(g, BF16)


def _head_rms(x, sel, sel_t, w):
    n = x.shape[1]
    hi, lo = _split_hi_lo(x * x)
    ss = (jnp.dot(hi, sel[:n], preferred_element_type=F32)
          + jnp.dot(lo, sel[:n], preferred_element_type=F32))
    inv = lax.rsqrt(ss * (1.0 / HEAD_DIM) + EPS)
    return x * _bdot(inv, sel_t[:, :n]) * w


def _rope(x, cos, sin):
    n = x.shape[1]
    lane = lax.broadcasted_iota(I32, x.shape, 1)
    first = (lane % (HEAD_DIM // 2)) < (HEAD_DIM // 4)
    rot = jnp.where(first, -pltpu.roll(x, n - HEAD_DIM // 4, axis=1), pltpu.roll(x, HEAD_DIM // 4, axis=1))
    return x * cos + rot * sin


def _qkv_kernel(x_ref, mod_ref, nw_ref, w_ref, sel_ref, selt_ref, qw_ref, kw_ref, cos_ref, sin_ref,
                q_ref, k_ref, v_ref, kf_ref, vf_ref, *, tm):
    i = pl.program_id(0)
    r = _mod_row(i, tm)
    h = _rms(x_ref[...], nw_ref[...]) * (1.0 + _mod_vec(mod_ref, r, 1)) + _mod_vec(mod_ref, r, 0)
    qkv = _bdot(h, w_ref[...])
    nq, nk = ATT_HEADS * HEAD_DIM, KV_HEADS * HEAD_DIM
    q = _head_rms(qkv[:, :nq], sel_ref[...], selt_ref[...], qw_ref[...])
    k = _head_rms(qkv[:, nq:nq + nk], sel_ref[...], selt_ref[...], kw_ref[...])
    v = qkv[:, nq + nk:]
    kf_ref[...] = k
    vf_ref[...] = v
    latent = i * tm >= NP_TOK
    cos = jnp.where(latent, cos_ref[...], 1.0)
    sin = jnp.where(latent, sin_ref[...], 0.0)
    q = _rope(q, cos, sin) * (HEAD_DIM ** -0.5)
    k = _rope(k, cos[:, :nk], sin[:, :nk])
    q_ref[...] = q.T.astype(BF16)
    v_ref[...] = v.T.astype(BF16)
    for hd in range(KV_HEADS):
        k_ref[hd] = k[:, hd * HEAD_DIM:(hd + 1) * HEAD_DIM].astype(BF16)


def _qkv(x, mod_l, nw, w, qw, kw, cos, sin, *, tm):
    nk = KV_HEADS * HEAD_DIM
    sel = _head_sel()
    nblk = DEC_SEQ // tm

    def cmap(i):
        return (jnp.maximum(i - NP_TOK // tm, 0) % nblk, 0)

    return pl.pallas_call(
        functools.partial(_qkv_kernel, tm=tm),
        grid=(T // tm,),
        in_specs=[pl.BlockSpec((tm, D), lambda i: (i, 0)),
                  pl.BlockSpec((16, N_MOD * D), lambda i: (0, 0)),
                  pl.BlockSpec((1, D), lambda i: (0, 0)),
                  pl.BlockSpec((D, D + 2 * nk), lambda i: (0, 0)),
                  pl.BlockSpec((D, LANES), lambda i: (0, 0)),
                  pl.BlockSpec((LANES, D), lambda i: (0, 0)),
                  pl.BlockSpec((1, D), lambda i: (0, 0)),
                  pl.BlockSpec((1, nk), lambda i: (0, 0)),
                  pl.BlockSpec((tm, D), cmap),
                  pl.BlockSpec((tm, D), cmap)],
        out_specs=[pl.BlockSpec((D, tm), lambda i: (0, i)),
                   pl.BlockSpec((KV_HEADS, tm, HEAD_DIM), lambda i: (0, i, 0)),
                   pl.BlockSpec((nk, tm), lambda i: (0, i)),
                   pl.BlockSpec((tm, nk), lambda i: (i, 0)),
                   pl.BlockSpec((tm, nk), lambda i: (i, 0))],
        out_shape=[jax.ShapeDtypeStruct((D, T), BF16),
                   jax.ShapeDtypeStruct((KV_HEADS, T, HEAD_DIM), BF16),
                   jax.ShapeDtypeStruct((nk, T), BF16),
                   jax.ShapeDtypeStruct((T, nk), F32),
                   jax.ShapeDtypeStruct((T, nk), F32)],
        compiler_params=_cparams(("arbitrary",)),
        name="attn_qkv",
    )(x, mod_l, nw, w, sel, sel.T, qw, kw, cos, sin)


def _attn_kernel(*refs, has_cache):
    if has_cache:
        q_ref, k_ref, v_ref, kc_ref, vc_ref, o_ref = refs
    else:
        q_ref, k_ref, v_ref, o_ref = refs
    k = k_ref[0]
    vt = v_ref[...]
    heads = range(Q_PER_KV)
    qts = [q_ref[j * HEAD_DIM:(j + 1) * HEAD_DIM, :] for j in heads]
    ss = [jnp.dot(k, qt, preferred_element_type=F32) for qt in qts]
    ms = [jnp.max(s, axis=0, keepdims=True) for s in ss]
    if has_cache:
        scs = [jnp.dot(kc_ref[0, 0], qt, preferred_element_type=F32) for qt in qts]
        ms = [jnp.maximum(m, jnp.max(sc, axis=0, keepdims=True)) for m, sc in zip(ms, scs)]
    ps = [jnp.exp(s - m) for s, m in zip(ss, ms)]
    dens = [jnp.sum(p, axis=0, keepdims=True) for p in ps]
    os_ = [jnp.dot(vt, p.astype(BF16), preferred_element_type=F32) for p in ps]
    if has_cache:
        pcs = [jnp.exp(sc - m) for sc, m in zip(scs, ms)]
        dens = [d + jnp.sum(pc, axis=0, keepdims=True) for d, pc in zip(dens, pcs)]
        os_ = [o + jnp.dot(vc_ref[0, 0], pc.astype(BF16), preferred_element_type=F32) for o, pc in zip(os_, pcs)]
    o_ref[...] = jnp.concatenate([o / d for o, d in zip(os_, dens)], axis=0).T.astype(BF16)


def _attn_call(q, k, v, *, L, n_seq, row0, tq, cache):
    rb_q = row0 // tq
    rb_k = row0 // L
    nq = L // tq
    has_cache = cache is not None
    in_specs = [pl.BlockSpec((Q_PER_KV * HEAD_DIM, tq), lambda b, g, t: (g, rb_q + b * nq + t)),
                pl.BlockSpec((1, L, HEAD_DIM), lambda b, g, t: (g, rb_k + b, 0)),
                pl.BlockSpec((HEAD_DIM, L), lambda b, g, t: (g, rb_k + b))]
    args = [q, k, v]
    if has_cache:
        in_specs += [pl.BlockSpec((1, 1, PAST, HEAD_DIM), lambda b, g, t: (b, g, 0, 0)),
                     pl.BlockSpec((1, 1, HEAD_DIM, PAST), lambda b, g, t: (b, g, 0, 0))]
        args += list(cache)
    gw = Q_PER_KV * HEAD_DIM
    return pl.pallas_call(
        functools.partial(_attn_kernel, has_cache=has_cache),
        grid=(n_seq, KV_HEADS, nq),
        in_specs=in_specs,
        out_specs=pl.BlockSpec((tq, gw), lambda b, g, t: (b * nq + t, g)),
        out_shape=jax.ShapeDtypeStruct((n_seq * L, D), BF16),
        compiler_params=_cparams(("arbitrary", "arbitrary", "arbitrary")),
        name=f"attn_L{L}",
    )(*args)


def _rope_tables():
    rows = DEC_SEQ // GRID_W
    row = jnp.repeat(jnp.arange(rows), GRID_W)
    colp = jnp.tile(jnp.arange(GRID_W), rows)
    pos = jnp.stack([row, colp], axis=-1).astype(F32)
    half = HEAD_DIM // 2
    inv_freq = ROPE_THETA ** (-jnp.arange(0, half, 2, dtype=F32) / half)
    ang = pos[:, :, None] * inv_freq
    ang = jnp.concatenate([ang, ang], axis=-1).reshape(DEC_SEQ, HEAD_DIM)
    ang = jnp.tile(ang, (1, ATT_HEADS))
    return jnp.cos(ang), jnp.sin(ang)


def _ssd_params(conv_w, conv_b, dt_bias, a_log, d_skip):
    def dt_layout(p):
        g = p.reshape(2, SSD_G, HPG).transpose(1, 0, 2).reshape(SSD_G, 2 * HPG)
        return jnp.zeros((SSD_G, LANES), F32).at[:, :2 * HPG].set(g).reshape(1, SSD_G * LANES)

    ni = SSD_INNER
    nb = SSD_G * SSD_N
    return dict(
        cwx=conv_w[:, :ni], cwb=conv_w[:, ni:ni + nb], cwc=conv_w[:, ni + nb:],
        cbx=conv_b[None, :ni], cbb=conv_b[None, ni:ni + nb], cbc=conv_b[None, ni + nb:],
        dtb=dt_layout(dt_bias), alog=dt_layout(a_log),
        dcol=jnp.repeat(d_skip, SSD_P)[None, :], rsel=_ssd_rsel())


def _ssm_in_weights(w):
    ni, cd = SSD_INNER, SSD_INNER + 2 * SSD_G * SSD_N
    o_xbc, o_dt = ni, ni + cd
    o_hq = o_dt + 2 * SSD_HEADS
    o_hf, o_hi = o_hq + HG_QK, o_hq + 3 * HG_QK
    o_hg = o_hi + HG_IV
    main = jnp.concatenate([w[:, o_hq:o_hf], w[:, o_hf:o_hi], w[:, o_xbc:o_dt],
                            w[:, :ni], w[:, o_hi:o_hg], w[:, o_hg:o_hg + HG_IV]], axis=1).astype(BF16)
    wdt = w[:, o_dt:o_hq].reshape(D, 2, SSD_G, HPG).transpose(0, 2, 1, 3).reshape(D, SSD_G, 2 * HPG)
    wdt = jnp.zeros((D, SSD_G, LANES), F32).at[:, :, :2 * HPG].set(wdt).reshape(D, SSD_G * LANES)
    return main, wdt.astype(BF16)


def kernel(x_prompt, x_sample, c, c_ctx, state_ssd, state_hgrn, cache_k, cache_v, mod_w, mod_b, norm1_w,
           norm2_w, ssm_in_w, ssd_conv_w, ssd_conv_b, ssd_dt_bias, ssd_a_log, ssd_d, ssd_norm_w, hg_lb,
           hg_norm_w, ssm_out_w, attn_in_w, q_norm_w, k_norm_w, attn_out_w, router_w, router_b, exp_w1,
           exp_b1, exp_w2, exp_b2, final_norm_w):
    xp = x_prompt.reshape(NP_TOK, D)
    xs = x_sample.reshape(NS_TOK, D)
    cvec = jnp.zeros((16, D), F32).at[0].set(c_ctx).at[1:1 + DEC_BATCH].set(c)
    mod = _mod_table(cvec, mod_w, mod_b)

    w_main, w_dt = _ssm_in_weights(ssm_in_w[0])
    proj, projb, dtp = _inproj(xp, xs, mod[0], norm1_w[0][None], w_main, w_dt, tm=1024, tn=1536)
    prm = _ssd_params(ssd_conv_w[0], ssd_conv_b[0], ssd_dt_bias[0], ssd_a_log[0], ssd_d[0])
    yp, st_ssd = _ssd_call(proj, dtp, prm, L=SEQ, n_seq=BATCH, row0=0, s0=None, want_state=True)
    (ys,) = _ssd_call(proj, dtp, prm, L=DEC_SEQ, n_seq=DEC_BATCH, row0=NP_TOK, s0=state_ssd[:, 0],
                      want_state=False)
    hg_nw = hg_norm_w[0][None]
    op, st_hg = _gla_call(proj, projb, hg_lb, hg_nw, layer=0, L=SEQ, n_seq=BATCH, row0=0, s0=None,
                          want_state=True, hpb=4)
    (os_,) = _gla_call(proj, projb, hg_lb, hg_nw, layer=0, L=DEC_SEQ, n_seq=DEC_BATCH, row0=NP_TOK,
                       s0=state_hgrn[:, 0], want_state=False, hpb=2)
    x = _ssm_out(xp, xs, yp, ys, op, os_, projb, mod[0], ssd_norm_w[0][None], ssm_out_w[0].astype(BF16), tm=512)
    x = _moe_layer(0, x, mod[0], norm2_w[0][None], router_w[0], router_b[0], exp_w1, exp_b1, exp_w2, exp_b2)

    cos, sin = _rope_tables()
    q, k, v, kf, vf = _qkv(x, mod[1], norm1_w[1][None], attn_in_w[0].astype(BF16),
                           jnp.tile(q_norm_w[0], ATT_HEADS)[None], jnp.tile(k_norm_w[0], KV_HEADS)[None],
                           cos, sin, tm=256)
    ap = _attn_call(q, k, v, L=SEQ, n_seq=BATCH, row0=0, tq=SEQ, cache=None)
    ck = cache_k[:, 0].transpose(0, 2, 1, 3).astype(BF16)
    cv = cache_v[:, 0].transpose(0, 2, 3, 1).astype(BF16)
    as_ = _attn_call(q, k, v, L=DEC_SEQ, n_seq=DEC_BATCH, row0=NP_TOK, tq=256, cache=(ck, cv))
    x = _attn_out(x, ap, as_, mod[1], attn_out_w[0].astype(BF16), tm=512)
    y_prompt, y_sample = _moe_layer(1, x, mod[1], norm2_w[1][None], router_w[1], router_b[1], exp_w1, exp_b1,
                                    exp_w2, exp_b2, final_w=final_norm_w[None])
    y_prompt = y_prompt.reshape(BATCH, SEQ, D)
    y_sample = y_sample.reshape(DEC_BATCH, DEC_SEQ, D)
    new_ssd = st_ssd.reshape(BATCH, 1, 2, SSD_HEADS, SSD_P, SSD_N)
    new_hg = st_hg.reshape(BATCH, 1, 2, HG_HEADS, HG_K, HG_V)
    new_k = kf[:NP_TOK].reshape(BATCH, 1, SEQ, KV_HEADS, HEAD_DIM)
    new_v = vf[:NP_TOK].reshape(BATCH, 1, SEQ, KV_HEADS, HEAD_DIM)
    return (y_prompt, y_sample, new_ssd, new_hg, new_k, new_v)
```

```python
import functools

import jax
import jax.numpy as jnp
import numpy as np
from jax import lax
from jax.experimental import pallas as pl
from jax.experimental.pallas import tpu as pltpu

F32 = jnp.float32
BF16 = jnp.bfloat16
I32 = jnp.int32

D = 1024
BATCH, SEQ = 32, 256
DEC_BATCH, DEC_SEQ = 8, 1024
PAST = 256
NP_TOK = BATCH * SEQ
NS_TOK = DEC_BATCH * DEC_SEQ
T = NP_TOK + NS_TOK
DEPTH = 2
N_MOD = 6
EPS = 1e-6
GRID_W = 64
ROPE_THETA = 10000.0

SSD_HEADS, SSD_P, SSD_N, SSD_G = 16, 64, 128, 2
SSD_INNER = SSD_HEADS * SSD_P
SSD_CONV = 5
HPG = SSD_HEADS // SSD_G
GW = HPG * SSD_P
HG_HEADS, HG_K, HG_V = 8, 128, 128
HG_QK = HG_HEADS * HG_K
HG_IV = HG_HEADS * HG_V
ATT_HEADS, KV_HEADS, HEAD_DIM = 16, 4, 64
Q_PER_KV = ATT_HEADS // KV_HEADS
N_EXP, TOP_K, FF = 32, 4, 1024
SWIGLU_ALPHA, SWIGLU_LIMIT = 1.702, 7.0

LANES = 128
SUBLANES = 8
VMEM_LIMIT = 56 * 1024 * 1024

COL_HQ, COL_HF, COL_XBC = 0, 1024, 3072
PROJ_F = 4608
COL_Z, COL_HI, COL_HG = 0, 1024, 2048
PROJ_B = 3072

SSD_TC = 256
GLA_TC = 64
GLA_GROUP = 256

TB = 512
NB = T // TB
SEG = SUBLANES
RB = TB * TOP_K + N_EXP * SEG
TM_E = 512
SEG_SIZES = tuple(SEG << i for i in range((TB // SEG).bit_length() - 1, -1, -1))
PAD_SIZES = tuple(s for s in SEG_SIZES if s < TM_E)
SEG_LARGE = 128
BUF_SIZES = tuple(SEG << i for i in range((RB // SEG).bit_length() - 1, -1, -1))
P_ROWS_MAX = T * TOP_K + NB * N_EXP * (SEG - 1) + N_EXP * (TM_E - SEG)
N_ETILES = -(-P_ROWS_MAX // TM_E)
P_ROWS = N_ETILES * TM_E


def _cparams(sem):
    return pltpu.CompilerParams(dimension_semantics=sem, vmem_limit_bytes=VMEM_LIMIT)


def _silu(x):
    return x * jax.nn.sigmoid(x)


def _bdot(a, b):
    return jnp.dot(a.astype(BF16), b.astype(BF16), preferred_element_type=F32)


def _bdot_nt(a, b):
    return lax.dot_general(a.astype(BF16), b.astype(BF16), (((1,), (1,)), ((), ())),
                           preferred_element_type=F32)


def _mod_row(i, tm):
    start = i * tm
    return jnp.where(start < NP_TOK, 0, 1 + (start - NP_TOK) // DEC_SEQ)


def _mod_vec(mod_ref, r, k):
    return mod_ref[pl.ds(r, 1), k * D:(k + 1) * D]


def _rms(x, w):
    return x * lax.rsqrt(jnp.mean(x * x, axis=-1, keepdims=True) + EPS) * w


def _two_src_specs(tm, width=D):
    npt = NP_TOK // tm
    return [pl.BlockSpec((tm, width), lambda i, *_: (jnp.minimum(i, npt - 1), 0)),
            pl.BlockSpec((tm, width), lambda i, *_: (jnp.maximum(i - npt, 0), 0))]


def _pick(i, tm, p_ref, s_ref):
    return jnp.where(i * tm < NP_TOK, p_ref[...], s_ref[...])


def _mod_kernel(c_ref, w_ref, b_ref, o_ref):
    o_ref[0] = _bdot(_silu(c_ref[...]), w_ref[0]) + b_ref[0]


def _mod_table(cvec, mod_w, mod_b):
    tn = 1536
    return pl.pallas_call(
        _mod_kernel,
        grid=(DEPTH, N_MOD * D // tn),
        in_specs=[pl.BlockSpec((16, D), lambda l, n: (0, 0)),
                  pl.BlockSpec((1, D, tn), lambda l, n: (l, 0, n)),
                  pl.BlockSpec((1, 1, tn), lambda l, n: (l, 0, n))],
        out_specs=pl.BlockSpec((1, 16, tn), lambda l, n: (l, 0, n)),
        out_shape=jax.ShapeDtypeStruct((DEPTH, 16, N_MOD * D), F32),
        compiler_params=_cparams(("arbitrary", "arbitrary")),
        name="mod_table",
    )(cvec, mod_w, mod_b.reshape(DEPTH, 1, N_MOD * D))


def _inproj_kernel(xp_ref, xs_ref, mod_ref, nw_ref, w_ref, wx_ref, of_ref, ob_ref, ox_ref, h_ref, *, tm, nf):
    i = pl.program_id(0)
    n = pl.program_id(1)

    @pl.when(n == 0)
    def _():
        x = _pick(i, tm, xp_ref, xs_ref)
        r = _mod_row(i, tm)
        h = _rms(x, nw_ref[...]) * (1.0 + _mod_vec(mod_ref, r, 1)) + _mod_vec(mod_ref, r, 0)
        hb = h.astype(BF16)
        h_ref[...] = hb
        ox_ref[...] = jnp.dot(hb, wx_ref[...], preferred_element_type=F32)

    @pl.when(n < nf)
    def _():
        of_ref[...] = jnp.dot(h_ref[...], w_ref[...], preferred_element_type=F32)

    @pl.when(n >= nf)
    def _():
        ob_ref[...] = jnp.dot(h_ref[...], w_ref[...], preferred_element_type=F32).astype(BF16)


def _inproj(xp, xs, mod_l, nw, w, wx, *, tm, tn):
    nx = wx.shape[1]
    nf, nb = PROJ_F // tn, PROJ_B // tn
    return pl.pallas_call(
        functools.partial(_inproj_kernel, tm=tm, nf=nf),
        grid=(T // tm, nf + nb),
        in_specs=_two_src_specs(tm) + [
            pl.BlockSpec((16, N_MOD * D), lambda i, n: (0, 0)),
            pl.BlockSpec((1, D), lambda i, n: (0, 0)),
            pl.BlockSpec((D, tn), lambda i, n: (0, n)),
            pl.BlockSpec((D, nx), lambda i, n: (0, 0))],
        out_specs=[pl.BlockSpec((tm, tn), lambda i, n: (i, jnp.minimum(n, nf - 1))),
                   pl.BlockSpec((tm, tn), lambda i, n: (i, jnp.maximum(n - nf, 0))),
                   pl.BlockSpec((tm, nx), lambda i, n: (i, 0))],
        out_shape=[jax.ShapeDtypeStruct((T, PROJ_F), F32), jax.ShapeDtypeStruct((T, PROJ_B), BF16),
                   jax.ShapeDtypeStruct((T, nx), F32)],
        scratch_shapes=[pltpu.VMEM((tm, D), BF16)],
        compiler_params=_cparams(("arbitrary", "arbitrary")),
        name="inproj",
    )(xp, xs, mod_l, nw, w, wx)


def _tri(n, upper):
    r = lax.broadcasted_iota(I32, (n, n), 0)
    c = lax.broadcasted_iota(I32, (n, n), 1)
    return (c >= r) if upper else (c <= r)


def _conv_silu(x, w, b, L):
    row = lax.broadcasted_iota(I32, (L, 1), 0)
    acc = x * w[2:3] + b
    for k in (0, 1, 3, 4):
        off = k - 2
        shifted = pltpu.roll(x, (-off) % L, axis=0)
        ok = (row + off >= 0) & (row + off < L)
        acc = acc + jnp.where(ok, shifted, 0.0) * w[k:k + 1]
    return _silu(acc)


def _split_hi_lo(x):
    hi = x.astype(BF16)
    return hi, (x - hi.astype(F32)).astype(BF16)


def _ssd_kernel(*refs, L, has_init, want_state):
    (x_ref, b_ref, c_ref, dt_ref, cwx_ref, cwb_ref, cwc_ref, cbx_ref, cbb_ref, cbc_ref,
     dtb_ref, alog_ref, dcol_ref, rsel_ref) = refs[:14]
    pos = 14
    s0_ref = None
    if has_init:
        s0_ref = refs[pos]
        pos += 1
    y_ref = refs[pos]
    pos += 1
    st_ref = None
    if want_state:
        st_ref = refs[pos]
        pos += 1
    xs_s, bs_s, cs_s, y_s = refs[pos:pos + 4]

    tc = SSD_TC
    nc = L // tc
    xs_s[...] = _conv_silu(x_ref[...], cwx_ref[...], cbx_ref[...], L)
    bs_s[...] = _conv_silu(b_ref[...], cwb_ref[...], cbb_ref[...], L)
    cs_s[...] = _conv_silu(c_ref[...], cwc_ref[...], cbc_ref[...], L)
    y_s[...] = xs_s[...] * dcol_ref[...]

    z = dt_ref[...] + dtb_ref[...]
    dt_all = jnp.maximum(z, 0.0) + jnp.log(1.0 + jnp.exp(-jnp.abs(z)))
    a_row = -jnp.exp(alog_ref[...])

    dirs = (0, 1)
    tris = [_tri(tc, False), _tri(tc, True)]
    tri_b = [t.astype(BF16) for t in tris]
    sts = [s0_ref[d].reshape(GW, SSD_N).T if has_init else None for d in dirs]
    for ci in range(nc):
        cidx = (ci, nc - 1 - ci)
        sls = [slice(c * tc, (c + 1) * tc) for c in cidx]
        x = [xs_s[s] for s in sls]
        bm = [bs_s[s] for s in sls]
        cm = [cs_s[s] for s in sls]
        dt = [dt_all[s] for s in sls]
        parts = [_split_hi_lo(dt[d] * a_row) for d in dirs]
        cum = [jnp.dot(tri_b[d], parts[d][0], preferred_element_type=F32)
               + jnp.dot(tri_b[d], parts[d][1], preferred_element_type=F32) for d in dirs]
        cum_t = [cum[d].T for d in dirs]
        end = [cum[0][tc - 1:tc], cum[1][0:1]]
        ex = [_bdot(jnp.concatenate(
            [dt[d], jnp.exp(cum[d]), jnp.exp(end[d] - cum[d]),
             jnp.broadcast_to(jnp.exp(end[d]), (SUBLANES, LANES))], axis=0), rsel_ref[d])
            for d in dirs]
        xdt = [x[d] * ex[d][0:tc] for d in dirs]
        cb = [_bdot_nt(cm[0], bm[0])]
        cb.append(cb[0] if cidx[1] == cidx[0] else _bdot_nt(cm[1], bm[1]))
        lms = [[jnp.exp(jnp.where(tris[d], cum[d][:, d * HPG + h:d * HPG + h + 1]
                                  - cum_t[d][d * HPG + h:d * HPG + h + 1, :], -jnp.inf))
                for h in range(HPG)] for d in dirs]
        ys = [[_bdot(cb[d] * lms[d][h], xdt[d][:, h * SSD_P:(h + 1) * SSD_P]) for h in range(HPG)]
              for d in dirs]
        y = [jnp.concatenate(ys[d], axis=1) for d in dirs]
        for d in dirs:
            if sts[d] is not None:
                y[d] = y[d] + _bdot(cm[d], sts[d]) * ex[d][tc:2 * tc]
        if cidx[0] == cidx[1]:
            y_s[sls[0]] = y_s[sls[0]] + (y[0] + y[1])
        else:
            for d in dirs:
                y_s[sls[d]] = y_s[sls[d]] + y[d]
        sc = [_bdot(bm[d].T, xdt[d] * ex[d][2 * tc:3 * tc]) for d in dirs]
        sts = [sc[d] if sts[d] is None else sts[d] * ex[d][3 * tc:3 * tc + 1] + sc[d] for d in dirs]
    if want_state:
        for d in dirs:
            st_ref[d] = sts[d].T.reshape(HPG, SSD_P, SSD_N)

    y_ref[...] = y_s[...]


def _ssd_rsel():
    r = np.zeros((2, LANES, GW), np.float32)
    for d in range(2):
        for h in range(HPG):
            r[d, d * HPG + h, h * SSD_P:(h + 1) * SSD_P] = 1.0
    return jnp.asarray(r, BF16)


def _ssd_call(proj, dtp, prm, *, L, n_seq, row0, s0, want_state):
    rb = row0 // L
    has_init = s0 is not None
    xbc = COL_XBC
    in_specs = [
        pl.BlockSpec((L, GW), lambda b, g: (rb + b, xbc // GW + g)),
        pl.BlockSpec((L, SSD_N), lambda b, g: (rb + b, (xbc + SSD_INNER) // SSD_N + g)),
        pl.BlockSpec((L, SSD_N), lambda b, g: (rb + b, (xbc + SSD_INNER + SSD_G * SSD_N) // SSD_N + g)),
        pl.BlockSpec((L, LANES), lambda b, g: (rb + b, g)),
        pl.BlockSpec((SSD_CONV, GW), lambda b, g: (0, g)),
        pl.BlockSpec((SSD_CONV, SSD_N), lambda b, g: (0, g)),
        pl.BlockSpec((SSD_CONV, SSD_N), lambda b, g: (0, g)),
        pl.BlockSpec((1, GW), lambda b, g: (0, g)),
        pl.BlockSpec((1, SSD_N), lambda b, g: (0, g)),
        pl.BlockSpec((1, SSD_N), lambda b, g: (0, g)),
        pl.BlockSpec((1, LANES), lambda b, g: (0, g)),
        pl.BlockSpec((1, LANES), lambda b, g: (0, g)),
        pl.BlockSpec((1, GW), lambda b, g: (0, g)),
        pl.BlockSpec((2, LANES, GW), lambda b, g: (0, 0, 0)),
    ]
    args = [proj, proj, proj, dtp, prm["cwx"], prm["cwb"], prm["cwc"], prm["cbx"], prm["cbb"], prm["cbc"],
            prm["dtb"], prm["alog"], prm["dcol"], prm["rsel"]]
    if has_init:
        in_specs.append(pl.BlockSpec((None, 2, HPG, SSD_P, SSD_N), lambda b, g: (b, 0, g, 0, 0)))
        args.append(s0)
    out_specs = [pl.BlockSpec((L, GW), lambda b, g: (b, g))]
    out_shape = [jax.ShapeDtypeStruct((n_seq * L, SSD_INNER), F32)]
    if want_state:
        out_specs.append(pl.BlockSpec((None, 2, HPG, SSD_P, SSD_N), lambda b, g: (b, 0, g, 0, 0)))
        out_shape.append(jax.ShapeDtypeStruct((n_seq, 2, SSD_HEADS, SSD_P, SSD_N), F32))
    return pl.pallas_call(
        functools.partial(_ssd_kernel, L=L, has_init=has_init, want_state=want_state),
        grid=(n_seq, SSD_G),
        in_specs=in_specs, out_specs=out_specs, out_shape=out_shape,
        scratch_shapes=[pltpu.VMEM((L, GW), F32), pltpu.VMEM((L, SSD_N), F32),
                        pltpu.VMEM((L, SSD_N), F32), pltpu.VMEM((L, GW), F32)],
        compiler_params=_cparams(("arbitrary", "arbitrary")),
        name=f"ssd_L{L}",
    )(*args)


def _gla_kernel(*refs, L, layer, has_init, want_state, hpb):
    q_ref, ff_ref, fb_ref, v_ref, g_ref, lb_ref, nw_ref = refs[:7]
    pos = 7
    s0_ref = None
    if has_init:
        s0_ref = refs[pos]
        pos += 1
    o_ref = refs[pos]
    pos += 1
    st_ref = None
    if want_state:
        st_ref = refs[pos]
        pos += 1
    qs, ks, ls, os_ = refs[pos:pos + 4]

    tc, gr = GLA_TC, GLA_GROUP
    ng, cpg = L // gr, gr // tc
    lbr = lb_ref[...]
    e = jnp.exp(lbr - jnp.max(lbr, axis=0, keepdims=True))
    sm = e / jnp.sum(e, axis=0, keepdims=True)
    lb = sm[0]
    for j in range(1, layer + 1):
        lb = lb + sm[j]

    qs[...] = _silu(q_ref[...])
    for d, fr in ((0, ff_ref), (1, fb_ref)):
        logit = fr[...]
        lbd = lb[d:d + 1]
        f = lbd + (1.0 - lbd) * jax.nn.sigmoid(logit)
        ks[d] = (1.0 - lbd) * jax.nn.sigmoid(-logit)
        ls[d] = jnp.log(f)

    r_i = lax.broadcasted_iota(I32, (gr, gr), 0)
    c_i = lax.broadcasted_iota(I32, (gr, gr), 1)
    sh = tc.bit_length() - 1
    same = (r_i >> sh) == (c_i >> sh)
    causal = [same & (c_i <= r_i), same & (c_i >= r_i)]
    bd = [m.astype(BF16) for m in causal]
    row_chunk = lax.broadcasted_iota(I32, (gr, 1), 0) >> sh
    col_chunk = lax.broadcasted_iota(I32, (1, gr), 1) >> sh

    def rows_of(x, r):
        return jnp.concatenate(
            [jnp.broadcast_to(x[c * tc + r:c * tc + r + 1], (tc, x.shape[1])) for c in range(cpg)], axis=0)

    def groups(units):
        n = range(len(units))
        ds_ = [u[0] for u in units]
        sl = [(pl.ds(u[2], gr), slice(u[1] * HG_K, (u[1] + 1) * HG_K)) for u in units]
        sts = [u[3] for u in units]
        parts = [_split_hi_lo(ls[ds_[i], sl[i][0], sl[i][1]]) for i in n]
        cums = [jnp.dot(bd[ds_[i]], parts[i][0], preferred_element_type=F32)
                + jnp.dot(bd[ds_[i]], parts[i][1], preferred_element_type=F32) for i in n]
        q = [qs[sl[i][0], sl[i][1]] for i in n]
        k = [ks[ds_[i], sl[i][0], sl[i][1]] for i in n]
        v = [v_ref[sl[i][0], sl[i][1]] for i in n]
        ref_b = [rows_of(cums[i], tc // 2 if ds_[i] else tc // 2 - 1) for i in n]
        end_b = [rows_of(cums[i], 0 if ds_[i] else tc - 1) for i in n]
        att = [_bdot_nt(q[i] * jnp.exp(cums[i] - ref_b[i]), k[i] * jnp.exp(ref_b[i] - cums[i])) for i in n]
        o = [_bdot(jnp.where(causal[ds_[i]], att[i], 0.0), v[i]) for i in n]
        kd_t = [(k[i] * jnp.exp(end_b[i] - cums[i])).T for i in n]
        inc = [_bdot(jnp.concatenate([jnp.where(col_chunk == c, kd_t[i], 0.0) for c in range(cpg)], axis=0), v[i])
               for i in n]
        dec_t = [jnp.exp(end_b[i]).T for i in n]
        stacks = []
        for i in n:
            st, s_in = sts[i], [None] * cpg
            for c in (range(cpg - 1, -1, -1) if ds_[i] else range(cpg)):
                s_in[c] = st
                st = st * dec_t[i][:, c * tc:c * tc + 1] + inc[i][c * HG_K:(c + 1) * HG_K]
            sts[i] = st
            stacks.append(jnp.concatenate(s_in, axis=0))
        qe = [q[i] * jnp.exp(cums[i]) for i in n]
        o = [o[i] + _bdot(jnp.concatenate([jnp.where(row_chunk == c, qe[i], 0.0) for c in range(cpg)], axis=1),
                          stacks[i]) for i in n]
        return o, sts

    def init(d, j):
        return s0_ref[d, j] if has_init else jnp.zeros((HG_K, HG_V), F32)

    heads = range(hpb)
    if ng == 1:
        o, sts = groups([(d, j, 0, init(d, j)) for j in heads for d in (0, 1)])
        outs = [o[2 * j] + o[2 * j + 1] for j in heads]
        finals = [(sts[2 * j], sts[2 * j + 1]) for j in heads]
    else:
        os_[...] = jnp.zeros_like(os_)

        def body(i, carry):
            g0 = (pl.multiple_of(i * gr, gr), pl.multiple_of((ng - 1 - i) * gr, gr))
            o, sts = groups([(d, j, g0[d], carry[j][d]) for j in heads for d in (0, 1)])
            for j in heads:
                cs = slice(j * HG_V, (j + 1) * HG_V)
                for d in (0, 1):
                    os_[pl.ds(g0[d], gr), cs] = os_[pl.ds(g0[d], gr), cs] + o[2 * j + d]
            return tuple((sts[2 * j], sts[2 * j + 1]) for j in heads)

        finals = lax.fori_loop(0, ng, body, tuple((init(0, j), init(1, j)) for j in heads))
        outs = [os_[:, j * HG_V:(j + 1) * HG_V] for j in heads]
    for j in heads:
        cs = slice(j * HG_V, (j + 1) * HG_V)
        if want_state:
            st_ref[0, j] = finals[j][0]
            st_ref[1, j] = finals[j][1]
        o_ref[:, cs] = (_rms(outs[j], nw_ref[:, cs]) * _silu(g_ref[:, cs].astype(F32))).astype(BF16)


def _gla_call(proj, projb, hg_lb, hg_nw, *, layer, L, n_seq, row0, s0, want_state, hpb):
    rb = row0 // L
    has_init = s0 is not None
    w = hpb * HG_K

    def col(c0):
        return lambda b, h: (rb + b, c0 // w + h)

    st_spec = pl.BlockSpec((None, 2, hpb, HG_K, HG_V), lambda b, h: (b, 0, h, 0, 0))
    in_specs = [
        pl.BlockSpec((L, w), col(COL_HQ)),
        pl.BlockSpec((L, w), col(COL_HF)),
        pl.BlockSpec((L, w), col(COL_HF + HG_QK)),
        pl.BlockSpec((L, w), col(COL_HI)),
        pl.BlockSpec((L, w), col(COL_HG)),
        pl.BlockSpec((DEPTH + 1, 2, w), lambda b, h: (0, 0, h)),
        pl.BlockSpec((1, w), lambda b, h: (0, h)),
    ]
    args = [proj, proj, proj, projb, projb, hg_lb, hg_nw]
    if has_init:
        in_specs.append(st_spec)
        args.append(s0)
    out_specs = [pl.BlockSpec((L, w), lambda b, h: (b, h))]
    out_shape = [jax.ShapeDtypeStruct((n_seq * L, HG_IV), BF16)]
    if want_state:
        out_specs.append(st_spec)
        out_shape.append(jax.ShapeDtypeStruct((n_seq, 2, HG_HEADS, HG_K, HG_V), F32))
    return pl.pallas_call(
        functools.partial(_gla_kernel, L=L, layer=layer, has_init=has_init, want_state=want_state, hpb=hpb),
        grid=(n_seq, HG_HEADS // hpb),
        in_specs=in_specs, out_specs=out_specs, out_shape=out_shape,
        scratch_shapes=[pltpu.VMEM((L, w), F32), pltpu.VMEM((2, L, w), F32),
                        pltpu.VMEM((2, L, w), F32), pltpu.VMEM((L, w), F32)],
        compiler_params=_cparams(("arbitrary", "arbitrary")),
        name=f"gla_L{L}",
    )(*args)


def _ssm_out_kernel(xp_ref, xs_ref, yp_ref, ys_ref, op_ref, os_ref, z_ref, mod_ref, nw_ref, w_ref,
                    out_ref, *, tm):
    i = pl.program_id(0)
    g = _pick(i, tm, yp_ref, ys_ref) * _silu(z_ref[...].astype(F32))
    mix = (_bdot(_rms(g, nw_ref[...]), w_ref[0:SSD_INNER, :])
           + _bdot(_pick(i, tm, op_ref, os_ref), w_ref[SSD_INNER:, :]))
    out_ref[...] = _pick(i, tm, xp_ref, xs_ref) + _mod_vec(mod_ref, _mod_row(i, tm), 2) * mix


def _ssm_out(xp, xs, yp, ys, op, os_, proj, mod_l, nw, w, *, tm):
    return pl.pallas_call(
        functools.partial(_ssm_out_kernel, tm=tm),
        grid=(T // tm,),
        in_specs=_two_src_specs(tm) + _two_src_specs(tm, SSD_INNER) + _two_src_specs(tm, HG_IV) + [
            pl.BlockSpec((tm, SSD_INNER), lambda i: (i, COL_Z // SSD_INNER)),
            pl.BlockSpec((16, N_MOD * D), lambda i: (0, 0)),
            pl.BlockSpec((1, SSD_INNER), lambda i: (0, 0)),
            pl.BlockSpec((SSD_INNER + HG_IV, D), lambda i: (0, 0))],
        out_specs=pl.BlockSpec((tm, D), lambda i: (i, 0)),
        out_shape=jax.ShapeDtypeStruct((T, D), F32),
        compiler_params=_cparams(("arbitrary",)),
        name="ssm_out",
    )(xp, xs, yp, ys, op, os_, proj, mod_l, nw, w)


def _attn_out_kernel(x_ref, ap_ref, as_ref, mod_ref, w_ref, out_ref, *, tm):
    i = pl.program_id(0)
    mix = jnp.dot(_pick(i, tm, ap_ref, as_ref), w_ref[...], preferred_element_type=F32)
    out_ref[...] = x_ref[...] + _mod_vec(mod_ref, _mod_row(i, tm), 2) * mix


def _attn_out(x, ap, as_, mod_l, w, *, tm):
    return pl.pallas_call(
        functools.partial(_attn_out_kernel, tm=tm),
        grid=(T // tm,),
        in_specs=[pl.BlockSpec((tm, D), lambda i: (i, 0))] + _two_src_specs(tm) + [
            pl.BlockSpec((16, N_MOD * D), lambda i: (0, 0)),
            pl.BlockSpec((D, D), lambda i: (0, 0))],
        out_specs=pl.BlockSpec((tm, D), lambda i: (i, 0)),
        out_shape=jax.ShapeDtypeStruct((T, D), F32),
        compiler_params=_cparams(("arbitrary",)),
        name="attn_out",
    )(x, ap, as_, mod_l, w)


def _route_kernel(x_ref, mod_ref, nw_ref, rw_ref, rb_ref, h_ref, meta_ref, metat_ref, cnt_ref):
    i = pl.program_id(0)
    r = _mod_row(i, TB)
    h = _rms(x_ref[...], nw_ref[...]) * (1.0 + _mod_vec(mod_ref, r, 4)) + _mod_vec(mod_ref, r, 3)
    hb = h.astype(BF16)
    h_ref[...] = hb
    logits = jnp.dot(hb, rw_ref[...], preferred_element_type=F32) + rb_ref[...]
    lane = lax.broadcasted_iota(I32, (TB, LANES), 1)
    lane_f = lane.astype(F32)
    vals, hots, idxs = [], [], []
    for _ in range(TOP_K):
        m = jnp.max(logits, axis=-1, keepdims=True)
        idx = jnp.min(jnp.where(logits == m, lane_f, float(LANES)), axis=-1, keepdims=True)
        hot = lane_f == idx
        vals.append(m)
        idxs.append(idx)
        hots.append(hot)
        logits = jnp.where(hot, -jnp.inf, logits)
    es = [jnp.exp(v - vals[0]) for v in vals]
    den = es[0] + es[1] + es[2] + es[3]

    r_i = lax.broadcasted_iota(I32, (TB, TB), 0)
    c_i = lax.broadcasted_iota(I32, (TB, TB), 1)
    below = (c_i < r_i).astype(BF16)
    base = jnp.zeros((1, LANES), F32)
    ranks = []
    for k in range(TOP_K):
        hot_f = hots[k].astype(F32)
        before = jnp.dot(below, hots[k].astype(BF16), preferred_element_type=F32)
        ranks.append(base + before)
        base = base + jnp.sum(hot_f, axis=0, keepdims=True)
    units = jnp.floor((base + (SEG - 1)) * (1.0 / SEG))
    lr = lax.broadcasted_iota(I32, (LANES, LANES), 0)
    lc = lax.broadcasted_iota(I32, (LANES, LANES), 1)
    seg_start = SEG * jnp.dot(jnp.broadcast_to(units, (SUBLANES, LANES)).astype(BF16),
                              (lr < lc).astype(BF16), preferred_element_type=F32)[0:1]
    meta = jnp.zeros((TB, LANES), F32)
    for k in range(TOP_K):
        row = jnp.sum(hots[k].astype(F32) * (seg_start + ranks[k]), axis=-1, keepdims=True)
        meta = jnp.where(lane == k, idxs[k], meta)
        meta = jnp.where(lane == TOP_K + k, es[k] / den, meta)
        meta = jnp.where(lane == 2 * TOP_K + k, row, meta)
    meta_ref[...] = meta
    metat_ref[...] = meta.T
    cnt_ref[...] = base


def _route(x, mod_l, nw, rw, rb):
    return pl.pallas_call(
        _route_kernel,
        grid=(NB,),
        in_specs=[pl.BlockSpec((TB, D), lambda i: (i, 0)),
                  pl.BlockSpec((16, N_MOD * D), lambda i: (0, 0)),
                  pl.BlockSpec((1, D), lambda i: (0, 0)),
                  pl.BlockSpec((D, LANES), lambda i: (0, 0)),
                  pl.BlockSpec((1, LANES), lambda i: (0, 0))],
        out_specs=[pl.BlockSpec((TB, D), lambda i: (i, 0)),
                   pl.BlockSpec((TB, LANES), lambda i: (i, 0)),
                   pl.BlockSpec((LANES, TB), lambda i: (0, i)),
                   pl.BlockSpec((None, 1, LANES), lambda i: (i, 0, 0))],
        out_shape=[jax.ShapeDtypeStruct((T, D), BF16),
                   jax.ShapeDtypeStruct((T, LANES), F32),
                   jax.ShapeDtypeStruct((LANES, T), F32),
                   jax.ShapeDtypeStruct((NB, 1, LANES), F32)],
        compiler_params=_cparams(("arbitrary",)),
        name="moe_route",
    )(x, mod_l, nw, rw, rb)


def _copy(src, dst, sem):
    return pltpu.make_async_copy(src, dst, sem)


def _segment_copies(n, src, dst, sizes, make, wait):
    src = jnp.asarray(src, I32)
    dst = jnp.asarray(dst, I32)

    def pieces(src, dst, group):
        for size in group:
            take = n & size

            @pl.when(take != 0)
            def _(src=src, dst=dst, size=size):
                cp = make(pl.multiple_of(src, SEG), pl.multiple_of(dst, SEG), size)
                if wait:
                    cp.wait()
                else:
                    cp.start()

            src = src + take
            dst = dst + take

    large = tuple(s for s in sizes if s >= SEG_LARGE)
    if large:
        @pl.when(n >= SEG_LARGE)
        def _():
            pieces(src, dst, large)

        skip = n & ~(SEG_LARGE - 1)
        src = src + skip
        dst = dst + skip
    pieces(src, dst, tuple(s for s in sizes if s < SEG_LARGE))


def _sort_push_kernel(tab_ref, tabp_ref, ztab_ref, nv_ref, metat_ref, h_ref, xs_hbm, buf, zb, sem, zsem):
    b = pl.program_id(0)
    slot = b & 1

    def zero_copies(wait):
        def per_expert(e, carry):
            _segment_copies(ztab_ref[1, e], 0, ztab_ref[0, e], PAD_SIZES,
                            lambda s, d, size: _copy(zb.at[pl.ds(0, size)], xs_hbm.at[pl.ds(d, size)], zsem), wait)
            return carry

        lax.fori_loop(0, N_EXP, per_expert, 0)

        def per_tile(j, carry):
            cp = _copy(zb, xs_hbm.at[pl.ds(pl.multiple_of(j * TM_E, TM_E), TM_E)], zsem)
            if wait:
                cp.wait()
            else:
                cp.start()
            return carry

        lax.fori_loop(nv_ref[0], N_ETILES, per_tile, 0)

    @pl.when(b == 0)
    def _():
        zb[...] = jnp.zeros_like(zb)
        zero_copies(False)

    ch = 256
    for c in range(RB // ch):
        rows = (lax.broadcasted_iota(I32, (ch, TB), 0) + c * ch).astype(F32)
        hit = rows == metat_ref[2 * TOP_K:2 * TOP_K + 1, :]
        for k in range(1, TOP_K):
            hit = hit | (rows == metat_ref[2 * TOP_K + k:2 * TOP_K + k + 1, :])
        buf[slot, c * ch:(c + 1) * ch, :] = jnp.dot(hit.astype(BF16), h_ref[...], preferred_element_type=F32)

    def push(t_ref, s, wait):
        def per_expert(e, carry):
            _segment_copies(
                t_ref[0, e], t_ref[1, e], t_ref[2, e], SEG_SIZES,
                lambda so, do, size: _copy(buf.at[s, pl.ds(so, size)], xs_hbm.at[pl.ds(do, size)], sem.at[s]), wait)
            return carry

        lax.fori_loop(0, N_EXP, per_expert, 0)

    def drain(t_ref, s):
        _segment_copies(t_ref[1, N_EXP - 1] + t_ref[0, N_EXP - 1], 0, 0, BUF_SIZES,
                        lambda so, do, size: _copy(buf.at[s, pl.ds(0, size)], xs_hbm.at[pl.ds(0, size)], sem.at[s]),
                        True)

    @pl.when(b > 0)
    def _():
        drain(tabp_ref, 1 - slot)

    push(tab_ref, slot, False)

    @pl.when(b == NB - 1)
    def _():
        drain(tab_ref, slot)

    @pl.when(b == 0)
    def _():
        zero_copies(True)


def _sort_push(tab, ztab, n_valid, metat, h):
    return pl.pallas_call(
        _sort_push_kernel,
        grid=(NB,),
        in_specs=[pl.BlockSpec((None, 3, N_EXP), lambda b: (b, 0, 0), memory_space=pltpu.SMEM),
                  pl.BlockSpec((None, 3, N_EXP), lambda b: (jnp.maximum(b - 1, 0), 0, 0),
                               memory_space=pltpu.SMEM),
                  pl.BlockSpec(memory_space=pltpu.SMEM),
                  pl.BlockSpec(memory_space=pltpu.SMEM),
                  pl.BlockSpec((LANES, TB), lambda b: (0, b)),
                  pl.BlockSpec((TB, D), lambda b: (b, 0))],
        out_specs=pl.BlockSpec(memory_space=pl.ANY),
        out_shape=jax.ShapeDtypeStruct((P_ROWS, D), F32),
        scratch_shapes=[pltpu.VMEM((2, RB, D), F32), pltpu.VMEM((TM_E, D), F32),
                        pltpu.SemaphoreType.DMA((2,)), pltpu.SemaphoreType.DMA(())],
        compiler_params=_cparams(("arbitrary",)),
        name="moe_sort_push",
    )(tab, tab, ztab, n_valid, metat, h)


def _expert_kernel(te_ref, tr_ref, nv_ref, par_ref, nxt_ref, x_ref, w1_hbm, b1_ref, w2_hbm, b2_ref, y_ref,
                   wf1, wf2, w1_s, w2_s, sem, *, layer):
    j = pl.program_id(0)
    valid = j < nv_ref[0]
    fresh = jnp.logical_or(j == 0, te_ref[j] != te_ref[jnp.maximum(j - 1, 0)])

    def fetch(e, s):
        return (_copy(w1_hbm.at[layer, e], wf1.at[s], sem.at[0, s]),
                _copy(w2_hbm.at[layer, e], wf2.at[s], sem.at[1, s]))

    @pl.when(jnp.logical_and(valid, fresh))
    def _():
        s = par_ref[j]

        @pl.when(j == 0)
        def _():
            for cp in fetch(te_ref[0], 0):
                cp.start()

        for cp in fetch(te_ref[j], s):
            cp.wait()

        @pl.when(nxt_ref[j] >= 0)
        def _():
            for cp in fetch(nxt_ref[j], 1 - s):
                cp.start()

        w1_s[...] = wf1[s].astype(BF16)
        w2_s[...] = wf2[s].astype(BF16)

    @pl.when(valid)
    def _():
        rows = lax.broadcasted_iota(I32, (TM_E, 1), 0)
        x = jnp.where(rows < tr_ref[j], x_ref[...], 0.0).astype(BF16)
        gu = jnp.dot(x, w1_s[...], preferred_element_type=F32) + b1_ref[...]
        gate = jnp.minimum(gu[:, :FF], SWIGLU_LIMIT)
        up = jnp.clip(gu[:, FF:], -SWIGLU_LIMIT, SWIGLU_LIMIT)
        act = (up + 1.0) * gate * jax.nn.sigmoid(SWIGLU_ALPHA * gate)
        y_ref[...] = jnp.dot(act.astype(BF16), w2_s[...], preferred_element_type=F32) + b2_ref[...]

    @pl.when(jnp.logical_not(valid))
    def _():
        y_ref[...] = jnp.zeros_like(y_ref)


def _experts(layer, tile_exp, tile_rows, n_valid, run_par, run_next, xs, w1, b1, w2, b2):
    def wmap(j, te, *_):
        return (layer, te[j], 0, 0)

    return pl.pallas_call(
        functools.partial(_expert_kernel, layer=layer),
        grid_spec=pltpu.PrefetchScalarGridSpec(
            num_scalar_prefetch=5,
            grid=(N_ETILES,),
            in_specs=[pl.BlockSpec((TM_E, D), lambda j, *_: (j, 0)),
                      pl.BlockSpec(memory_space=pl.ANY),
                      pl.BlockSpec((None, None, 1, 2 * FF), wmap),
                      pl.BlockSpec(memory_space=pl.ANY),
                      pl.BlockSpec((None, None, 1, D), wmap)],
            out_specs=pl.BlockSpec((TM_E, D), lambda j, *_: (j, 0)),
            scratch_shapes=[pltpu.VMEM((2, D, 2 * FF), F32), pltpu.VMEM((2, FF, D), F32),
                            pltpu.VMEM((D, 2 * FF), BF16), pltpu.VMEM((FF, D), BF16),
                            pltpu.SemaphoreType.DMA((2, 2))]),
        out_shape=jax.ShapeDtypeStruct((P_ROWS, D), F32),
        compiler_params=_cparams(("arbitrary",)),
        name="moe_experts",
    )(tile_exp, tile_rows, n_valid, run_par, run_next, xs, w1, b1.reshape(DEPTH, N_EXP, 1, 2 * FF), w2,
      b2.reshape(DEPTH, N_EXP, 1, D))


def _combine_kernel(*refs, final):
    if final:
        tab_ref, tabn_ref, y_hbm, x_ref, meta_ref, mod_ref, fw_ref, outp_ref, outs_ref, buf, sem = refs
    else:
        tab_ref, tabn_ref, y_hbm, x_ref, meta_ref, mod_ref, out_ref, buf, sem = refs
    b = pl.program_id(0)
    slot = b & 1

    def pull(t_ref, s, wait):
        def per_expert(e, carry):
            _segment_copies(
                t_ref[0, e], t_ref[2, e], t_ref[1, e], SEG_SIZES,
                lambda so, do, size: _copy(y_hbm.at[pl.ds(so, size)], buf.at[s, pl.ds(do, size)], sem.at[s]), wait)
            return carry

        lax.fori_loop(0, N_EXP, per_expert, 0)

    @pl.when(b == 0)
    def _():
        pull(tab_ref, slot, False)

    @pl.when(b + 1 < NB)
    def _():
        pull(tabn_ref, 1 - slot, False)

    used = tab_ref[1, N_EXP - 1] + tab_ref[0, N_EXP - 1]
    _segment_copies(used, 0, 0, BUF_SIZES,
                    lambda so, do, size: _copy(y_hbm.at[pl.ds(0, size)], buf.at[slot, pl.ds(0, size)], sem.at[slot]),
                    True)

    meta = meta_ref[...]
    ch = 256
    acc = jnp.zeros((TB, D), F32)
    for c in range(RB // ch):
        cols = (lax.broadcasted_iota(I32, (TB, ch), 1) + c * ch).astype(F32)
        g = jnp.zeros((TB, ch), F32)
        for k in range(TOP_K):
            g = g + jnp.where(cols == meta[:, 2 * TOP_K + k:2 * TOP_K + k + 1],
                              meta[:, TOP_K + k:TOP_K + k + 1], 0.0)
        rows = lax.broadcasted_iota(I32, (ch, 1), 0) + c * ch
        y = jnp.where(rows < used, buf[slot, c * ch:(c + 1) * ch, :], 0.0)
        acc = acc + _bdot(g, y)
    out = x_ref[...] + _mod_vec(mod_ref, _mod_row(b, TB), 5) * acc
    if final:
        y_out = _rms(out, fw_ref[...])

        @pl.when(b < NP_TOK // TB)
        def _():
            outp_ref[...] = y_out

        @pl.when(b >= NP_TOK // TB)
        def _():
            outs_ref[...] = y_out
    else:
        out_ref[...] = out


def _combine(tab, y, x, meta, mod_l, final_w=None):
    final = final_w is not None
    nbp = NP_TOK // TB
    in_specs = [pl.BlockSpec((None, 3, N_EXP), lambda b: (b, 0, 0), memory_space=pltpu.SMEM),
                pl.BlockSpec((None, 3, N_EXP), lambda b: (jnp.minimum(b + 1, NB - 1), 0, 0),
                             memory_space=pltpu.SMEM),
                pl.BlockSpec(memory_space=pl.ANY),
                pl.BlockSpec((TB, D), lambda b: (b, 0)),
                pl.BlockSpec((TB, LANES), lambda b: (b, 0)),
                pl.BlockSpec((16, N_MOD * D), lambda b: (0, 0))]
    args = [tab, tab, y, x, meta, mod_l]
    if final:
        in_specs.append(pl.BlockSpec((1, D), lambda b: (0, 0)))
        args.append(final_w)
        out_specs = [pl.BlockSpec((TB, D), lambda b: (jnp.minimum(b, nbp - 1), 0)),
                     pl.BlockSpec((TB, D), lambda b: (jnp.maximum(b - nbp, 0), 0))]
        out_shape = [jax.ShapeDtypeStruct((NP_TOK, D), F32), jax.ShapeDtypeStruct((NS_TOK, D), F32)]
    else:
        out_specs = pl.BlockSpec((TB, D), lambda b: (b, 0))
        out_shape = jax.ShapeDtypeStruct((T, D), F32)
    return pl.pallas_call(
        functools.partial(_combine_kernel, final=final),
        grid=(NB,),
        in_specs=in_specs, out_specs=out_specs, out_shape=out_shape,
        scratch_shapes=[pltpu.VMEM((2, RB, D), F32), pltpu.SemaphoreType.DMA((2,))],
        compiler_params=_cparams(("arbitrary",)),
        name="moe_combine_final" if final else "moe_combine",
    )(*args)


def _moe_layer(layer, x, mod_l, nw, rw, rb, w1, b1, w2, b2, final_w=None):
    rw_p = jnp.zeros((D, LANES), BF16).at[:, :N_EXP].set(rw.astype(BF16))
    rb_p = jnp.full((1, LANES), -jnp.inf, F32).at[0, :N_EXP].set(rb)
    h, meta, metat, cnt = _route(x, mod_l, nw, rw_p, rb_p)

    counts = cnt[:, 0, :N_EXP].astype(I32)
    n_seg = (counts + SEG - 1) // SEG * SEG
    seg_start = jnp.cumsum(n_seg, axis=1) - n_seg
    tot = jnp.sum(n_seg, axis=0)
    padded = (tot + TM_E - 1) // TM_E * TM_E
    ends = jnp.cumsum(padded)
    offs = ends - padded
    glob = offs[None, :] + jnp.cumsum(n_seg, axis=0) - n_seg
    tab = jnp.stack([n_seg, seg_start, glob], axis=1)
    ztab = jnp.stack([offs + tot, padded - tot], axis=0)
    n_valid = ends[-1] // TM_E
    starts = jnp.arange(N_ETILES, dtype=I32) * TM_E
    te = jnp.minimum(jnp.sum((ends[None, :] <= starts[:, None]).astype(I32), axis=1), N_EXP - 1)
    last = jnp.sum((ends[:-1] < ends[-1]).astype(I32))
    te = jnp.where(starts < ends[-1], te, last)
    tr = jnp.clip(offs[te] + tot[te] - starts, 0, TM_E)
    nv = n_valid.reshape(1)
    run_start = jnp.concatenate([jnp.ones((1,), I32), (te[1:] != te[:-1]).astype(I32)])
    run_par = (jnp.cumsum(run_start) - 1) & 1
    eid = jnp.arange(N_EXP, dtype=I32)
    later = (eid[None, :] > eid[:, None]) & (padded[None, :] > 0)
    nxt_e = jnp.min(jnp.where(later, eid[None, :], N_EXP), axis=1)
    run_next = jnp.where(nxt_e < N_EXP, nxt_e, -1)[te]

    xs = _sort_push(tab, ztab, nv, metat, h)
    y = _experts(layer, te, tr, nv, run_par, run_next, xs, w1, b1, w2, b2)
    return _combine(tab, y, x, meta, mod_l, final_w)


def _head_sel():
    g = np.zeros((D, LANES), np.float32)
    for c in range(D):
        g[c, c // HEAD_DIM] = 1.0
    return jnp.asarray(g, BF16)


def _head_rms(x, sel, sel_t, w):
    n = x.shape[1]
    hi, lo = _split_hi_lo(x * x)
    ss = (jnp.dot(hi, sel[:n], preferred_element_type=F32)
          + jnp.dot(lo, sel[:n], preferred_element_type=F32))
    inv = lax.rsqrt(ss * (1.0 / HEAD_DIM) + EPS)
    return x * _bdot(inv, sel_t[:, :n]) * w


def _rope(x, cos, sin):
    n = x.shape[1]
    lane = lax.broadcasted_iota(I32, x.shape, 1)
    first = (lane % (HEAD_DIM // 2)) < (HEAD_DIM // 4)
    rot = jnp.where(first, -pltpu.roll(x, n - HEAD_DIM // 4, axis=1), pltpu.roll(x, HEAD_DIM // 4, axis=1))
    return x * cos + rot * sin


def _qkv_kernel(x_ref, mod_ref, nw_ref, w_ref, sel_ref, selt_ref, qw_ref, kw_ref, cos_ref, sin_ref,
                q_ref, k_ref, v_ref, kf_ref, vf_ref, *, tm):
    i = pl.program_id(0)
    r = _mod_row(i, tm)
    h = _rms(x_ref[...], nw_ref[...]) * (1.0 + _mod_vec(mod_ref, r, 1)) + _mod_vec(mod_ref, r, 0)
    qkv = _bdot(h, w_ref[...])
    nq, nk = ATT_HEADS * HEAD_DIM, KV_HEADS * HEAD_DIM
    q = _head_rms(qkv[:, :nq], sel_ref[...], selt_ref[...], qw_ref[...])
    k = _head_rms(qkv[:, nq:nq + nk], sel_ref[...], selt_ref[...], kw_ref[...])
    v = qkv[:, nq + nk:]
    kf_ref[...] = k
    vf_ref[...] = v
    latent = i * tm >= NP_TOK
    cos = jnp.where(latent, cos_ref[...], 1.0)
    sin = jnp.where(latent, sin_ref[...], 0.0)
    q = _rope(q, cos, sin) * (HEAD_DIM ** -0.5)
    k = _rope(k, cos[:, :nk], sin[:, :nk])
    q_ref[...] = q.T.astype(BF16)
    v_ref[...] = v.T.astype(BF16)
    for hd in range(KV_HEADS):
        k_ref[hd] = k[:, hd * HEAD_DIM:(hd + 1) * HEAD_DIM].astype(BF16)


def _qkv(x, mod_l, nw, w, qw, kw, cos, sin, *, tm):
    nk = KV_HEADS * HEAD_DIM
    sel = _head_sel()
    nblk = DEC_SEQ // tm

    def cmap(i):
        return (jnp.maximum(i - NP_TOK // tm, 0) % nblk, 0)

    return pl.pallas_call(
        functools.partial(_qkv_kernel, tm=tm),
        grid=(T // tm,),
        in_specs=[pl.BlockSpec((tm, D), lambda i: (i, 0)),
                  pl.BlockSpec((16, N_MOD * D), lambda i: (0, 0)),
                  pl.BlockSpec((1, D), lambda i: (0, 0)),
                  pl.BlockSpec((D, D + 2 * nk), lambda i: (0, 0)),
                  pl.BlockSpec((D, LANES), lambda i: (0, 0)),
                  pl.BlockSpec((LANES, D), lambda i: (0, 0)),
                  pl.BlockSpec((1, D), lambda i: (0, 0)),
                  pl.BlockSpec((1, nk), lambda i: (0, 0)),
                  pl.BlockSpec((tm, D), cmap),
                  pl.BlockSpec((tm, D), cmap)],
        out_specs=[pl.BlockSpec((D, tm), lambda i: (0, i)),
                   pl.BlockSpec((KV_HEADS, tm, HEAD_DIM), lambda i: (0, i, 0)),
                   pl.BlockSpec((nk, tm), lambda i: (0, i)),
                   pl.BlockSpec((tm, nk), lambda i: (i, 0)),
                   pl.BlockSpec((tm, nk), lambda i: (i, 0))],
        out_shape=[jax.ShapeDtypeStruct((D, T), BF16),
                   jax.ShapeDtypeStruct((KV_HEADS, T, HEAD_DIM), BF16),
                   jax.ShapeDtypeStruct((nk, T), BF16),
                   jax.ShapeDtypeStruct((T, nk), F32),
                   jax.ShapeDtypeStruct((T, nk), F32)],
        compiler_params=_cparams(("arbitrary",)),
        name="attn_qkv",
    )(x, mod_l, nw, w, sel, sel.T, qw, kw, cos, sin)


def _attn_kernel(*refs, has_cache):
    if has_cache:
        q_ref, k_ref, v_ref, kc_ref, vc_ref, o_ref = refs
    else:
        q_ref, k_ref, v_ref, o_ref = refs
    k = k_ref[0]
    vt = v_ref[...]
    heads = range(Q_PER_KV)
    qts = [q_ref[j * HEAD_DIM:(j + 1) * HEAD_DIM, :] for j in heads]
    ss = [jnp.dot(k, qt, preferred_element_type=F32) for qt in qts]
    ms = [jnp.max(s, axis=0, keepdims=True) for s in ss]
    if has_cache:
        scs = [jnp.dot(kc_ref[0, 0], qt, preferred_element_type=F32) for qt in qts]
        ms = [jnp.maximum(m, jnp.max(sc, axis=0, keepdims=True)) for m, sc in zip(ms, scs)]
    ps = [jnp.exp(s - m) for s, m in zip(ss, ms)]
    dens = [jnp.sum(p, axis=0, keepdims=True) for p in ps]
    os_ = [jnp.dot(vt, p.astype(BF16), preferred_element_type=F32) for p in ps]
    if has_cache:
        pcs = [jnp.exp(sc - m) for sc, m in zip(scs, ms)]
        dens = [d + jnp.sum(pc, axis=0, keepdims=True) for d, pc in zip(dens, pcs)]
        os_ = [o + jnp.dot(vc_ref[0, 0], pc.astype(BF16), preferred_element_type=F32) for o, pc in zip(os_, pcs)]
    o_ref[...] = jnp.concatenate([o / d for o, d in zip(os_, dens)], axis=0).T.astype(BF16)


def _attn_call(q, k, v, *, L, n_seq, row0, tq, cache):
    rb_q = row0 // tq
    rb_k = row0 // L
    nq = L // tq
    has_cache = cache is not None
    in_specs = [pl.BlockSpec((Q_PER_KV * HEAD_DIM, tq), lambda b, g, t: (g, rb_q + b * nq + t)),
                pl.BlockSpec((1, L, HEAD_DIM), lambda b, g, t: (g, rb_k + b, 0)),
                pl.BlockSpec((HEAD_DIM, L), lambda b, g, t: (g, rb_k + b))]
    args = [q, k, v]
    if has_cache:
        in_specs += [pl.BlockSpec((1, 1, PAST, HEAD_DIM), lambda b, g, t: (b, g, 0, 0)),
                     pl.BlockSpec((1, 1, HEAD_DIM, PAST), lambda b, g, t: (b, g, 0, 0))]
        args += list(cache)
    gw = Q_PER_KV * HEAD_DIM
    return pl.pallas_call(
        functools.partial(_attn_kernel, has_cache=has_cache),
        grid=(n_seq, KV_HEADS, nq),
        in_specs=in_specs,
        out_specs=pl.BlockSpec((tq, gw), lambda b, g, t: (b * nq + t, g)),
        out_shape=jax.ShapeDtypeStruct((n_seq * L, D), BF16),
        compiler_params=_cparams(("arbitrary", "arbitrary", "arbitrary")),
        name=f"attn_L{L}",
    )(*args)


def _rope_tables():
    rows = DEC_SEQ // GRID_W
    row = jnp.repeat(jnp.arange(rows), GRID_W)
    colp = jnp.tile(jnp.arange(GRID_W), rows)
    pos = jnp.stack([row, colp], axis=-1).astype(F32)
    half = HEAD_DIM // 2
    inv_freq = ROPE_THETA ** (-jnp.arange(0, half, 2, dtype=F32) / half)
    ang = pos[:, :, None] * inv_freq
    ang = jnp.concatenate([ang, ang], axis=-1).reshape(DEC_SEQ, HEAD_DIM)
    ang = jnp.tile(ang, (1, ATT_HEADS))
    return jnp.cos(ang), jnp.sin(ang)


def _ssd_params(conv_w, conv_b, dt_bias, a_log, d_skip):
    def dt_layout(p):
        g = p.reshape(2, SSD_G, HPG).transpose(1, 0, 2).reshape(SSD_G, 2 * HPG)
        return jnp.zeros((SSD_G, LANES), F32).at[:, :2 * HPG].set(g).reshape(1, SSD_G * LANES)

    ni = SSD_INNER
    nb = SSD_G * SSD_N
    return dict(
        cwx=conv_w[:, :ni], cwb=conv_w[:, ni:ni + nb], cwc=conv_w[:, ni + nb:],
        cbx=conv_b[None, :ni], cbb=conv_b[None, ni:ni + nb], cbc=conv_b[None, ni + nb:],
        dtb=dt_layout(dt_bias), alog=dt_layout(a_log),
        dcol=jnp.repeat(d_skip, SSD_P)[None, :], rsel=_ssd_rsel())


def _ssm_in_weights(w):
    ni, cd = SSD_INNER, SSD_INNER + 2 * SSD_G * SSD_N
    o_xbc, o_dt = ni, ni + cd
    o_hq = o_dt + 2 * SSD_HEADS
    o_hf, o_hi = o_hq + HG_QK, o_hq + 3 * HG_QK
    o_hg = o_hi + HG_IV
    main = jnp.concatenate([w[:, o_hq:o_hf], w[:, o_hf:o_hi], w[:, o_xbc:o_dt],
                            w[:, :ni], w[:, o_hi:o_hg], w[:, o_hg:o_hg + HG_IV]], axis=1).astype(BF16)
    wdt = w[:, o_dt:o_hq].reshape(D, 2, SSD_G, HPG).transpose(0, 2, 1, 3).reshape(D, SSD_G, 2 * HPG)
    wdt = jnp.zeros((D, SSD_G, LANES), F32).at[:, :, :2 * HPG].set(wdt).reshape(D, SSD_G * LANES)
    return main, wdt.astype(BF16)


def kernel(x_prompt, x_sample, c, c_ctx, state_ssd, state_hgrn, cache_k, cache_v, mod_w, mod_b, norm1_w,
           norm2_w, ssm_in_w, ssd_conv_w, ssd_conv_b, ssd_dt_bias, ssd_a_log, ssd_d, ssd_norm_w, hg_lb,
           hg_norm_w, ssm_out_w, attn_in_w, q_norm_w, k_norm_w, attn_out_w, router_w, router_b, exp_w1,
           exp_b1, exp_w2, exp_b2, final_norm_w):
    xp = x_prompt.reshape(NP_TOK, D)
    xs = x_sample.reshape(NS_TOK, D)
    cvec = jnp.zeros((16, D), F32).at[0].set(c_ctx).at[1:1 + DEC_BATCH].set(c)
    mod = _mod_table(cvec, mod_w, mod_b)

    w_main, w_dt = _ssm_in_weights(ssm_in_w[0])
    proj, projb, dtp = _inproj(xp, xs, mod[0], norm1_w[0][None], w_main, w_dt, tm=1024, tn=1536)
    prm = _ssd_params(ssd_conv_w[0], ssd_conv_b[0], ssd_dt_bias[0], ssd_a_log[0], ssd_d[0])
    yp, st_ssd = _ssd_call(proj, dtp, prm, L=SEQ, n_seq=BATCH, row0=0, s0=None, want_state=True)
    (ys,) = _ssd_call(proj, dtp, prm, L=DEC_SEQ, n_seq=DEC_BATCH, row0=NP_TOK, s0=state_ssd[:, 0],
                      want_state=False)
    hg_nw = hg_norm_w[0][None]
    op, st_hg = _gla_call(proj, projb, hg_lb, hg_nw, layer=0, L=SEQ, n_seq=BATCH, row0=0, s0=None,
                          want_state=True, hpb=8)
    (os_,) = _gla_call(proj, projb, hg_lb, hg_nw, layer=0, L=DEC_SEQ, n_seq=DEC_BATCH, row0=NP_TOK,
                       s0=state_hgrn[:, 0], want_state=False, hpb=4)
    x = _ssm_out(xp, xs, yp, ys, op, os_, projb, mod[0], ssd_norm_w[0][None], ssm_out_w[0].astype(BF16), tm=512)
    x = _moe_layer(0, x, mod[0], norm2_w[0][None], router_w[0], router_b[0], exp_w1, exp_b1, exp_w2, exp_b2)

    cos, sin = _rope_tables()
    q, k, v, kf, vf = _qkv(x, mod[1], norm1_w[1][None], attn_in_w[0].astype(BF16),
                           jnp.tile(q_norm_w[0], ATT_HEADS)[None], jnp.tile(k_norm_w[0], KV_HEADS)[None],
                           cos, sin, tm=256)
    ap = _attn_call(q, k, v, L=SEQ, n_seq=BATCH, row0=0, tq=SEQ, cache=None)
    ck = cache_k[:, 0].transpose(0, 2, 1, 3).astype(BF16)
    cv = cache_v[:, 0].transpose(0, 2, 3, 1).astype(BF16)
    as_ = _attn_call(q, k, v, L=DEC_SEQ, n_seq=DEC_BATCH, row0=NP_TOK, tq=512, cache=(ck, cv))
    x = _attn_out(x, ap, as_, mod[1], attn_out_w[0].astype(BF16), tm=512)
    y_prompt, y_sample = _moe_layer(1, x, mod[1], norm2_w[1][None], router_w[1], router_b[1], exp_w1, exp_b1,
                                    exp_w2, exp_b2, final_w=final_norm_w[None])
    y_prompt = y_prompt.reshape(BATCH, SEQ, D)
    y_sample = y_sample.reshape(DEC_BATCH, DEC_SEQ, D)
    new_ssd = st_ssd.reshape(BATCH, 1, 2, SSD_HEADS, SSD_P, SSD_N)
    new_hg = st_hg.reshape(BATCH, 1, 2, HG_HEADS, HG_K, HG_V)
    new_k = kf[:NP_TOK].reshape(BATCH, 1, SEQ, KV_HEADS, HEAD_DIM)
    new_v = vf[:NP_TOK].reshape(BATCH, 1, SEQ, KV_HEADS, HEAD_DIM)
    return (y_prompt, y_sample, new_ssd, new_hg, new_k, new_v)
```

```python
import functools

import jax
import jax.numpy as jnp
import numpy as np
from jax import lax
from jax.experimental import pallas as pl
from jax.experimental.pallas import tpu as pltpu

F32 = jnp.float32
BF16 = jnp.bfloat16
I32 = jnp.int32

D = 1024
BATCH, SEQ = 32, 256
DEC_BATCH, DEC_SEQ = 8, 1024
PAST = 256
NP_TOK = BATCH * SEQ
NS_TOK = DEC_BATCH * DEC_SEQ
T = NP_TOK + NS_TOK
DEPTH = 2
N_MOD = 6
EPS = 1e-6
GRID_W = 64
ROPE_THETA = 10000.0

SSD_HEADS, SSD_P, SSD_N, SSD_G = 16, 64, 128, 2
SSD_INNER = SSD_HEADS * SSD_P
SSD_CONV = 5
HPG = SSD_HEADS // SSD_G
GW = HPG * SSD_P
HG_HEADS, HG_K, HG_V = 8, 128, 128
HG_QK = HG_HEADS * HG_K
HG_IV = HG_HEADS * HG_V
ATT_HEADS, KV_HEADS, HEAD_DIM = 16, 4, 64
Q_PER_KV = ATT_HEADS // KV_HEADS
N_EXP, TOP_K, FF = 32, 4, 1024
SWIGLU_ALPHA, SWIGLU_LIMIT = 1.702, 7.0

LANES = 128
SUBLANES = 8
VMEM_LIMIT = 56 * 1024 * 1024

COL_HQ, COL_HF, COL_XBC = 0, 1024, 3072
PROJ_F = 4608
COL_Z, COL_HI, COL_HG = 0, 1024, 2048
PROJ_B = 3072

SSD_TC = {SEQ: 256, DEC_SEQ: 128}
GLA_TC = 64
GLA_GROUP = 256

TB = 512
NB = T // TB
SEG = SUBLANES
RB = TB * TOP_K + N_EXP * SEG
TM_E = 512
SEG_SIZES = tuple(SEG << i for i in range((TB // SEG).bit_length() - 1, -1, -1))
PAD_SIZES = tuple(s for s in SEG_SIZES if s < TM_E)
SEG_LARGE = 128
BUF_SIZES = tuple(SEG << i for i in range((RB // SEG).bit_length() - 1, -1, -1))
P_ROWS_MAX = T * TOP_K + NB * N_EXP * (SEG - 1) + N_EXP * (TM_E - SEG)
N_ETILES = -(-P_ROWS_MAX // TM_E)
P_ROWS = N_ETILES * TM_E


def _cparams(sem):
    return pltpu.CompilerParams(dimension_semantics=sem, vmem_limit_bytes=VMEM_LIMIT)


def _silu(x):
    return x * jax.nn.sigmoid(x)


def _bdot(a, b):
    return jnp.dot(a.astype(BF16), b.astype(BF16), preferred_element_type=F32)


def _bdot_nt(a, b):
    return lax.dot_general(a.astype(BF16), b.astype(BF16), (((1,), (1,)), ((), ())),
                           preferred_element_type=F32)


def _mod_row(i, tm):
    start = i * tm
    return jnp.where(start < NP_TOK, 0, 1 + (start - NP_TOK) // DEC_SEQ)


def _mod_vec(mod_ref, r, k):
    return mod_ref[pl.ds(r, 1), k * D:(k + 1) * D]


def _rms(x, w):
    return x * lax.rsqrt(jnp.mean(x * x, axis=-1, keepdims=True) + EPS) * w


def _two_src_specs(tm, width=D):
    npt = NP_TOK // tm
    return [pl.BlockSpec((tm, width), lambda i, *_: (jnp.minimum(i, npt - 1), 0)),
            pl.BlockSpec((tm, width), lambda i, *_: (jnp.maximum(i - npt, 0), 0))]


def _pick(i, tm, p_ref, s_ref):
    return jnp.where(i * tm < NP_TOK, p_ref[...], s_ref[...])


def _mod_kernel(c_ref, w_ref, b_ref, o_ref):
    o_ref[0] = _bdot(_silu(c_ref[...]), w_ref[0]) + b_ref[0]


def _mod_table(cvec, mod_w, mod_b):
    tn = 1536
    return pl.pallas_call(
        _mod_kernel,
        grid=(DEPTH, N_MOD * D // tn),
        in_specs=[pl.BlockSpec((16, D), lambda l, n: (0, 0)),
                  pl.BlockSpec((1, D, tn), lambda l, n: (l, 0, n)),
                  pl.BlockSpec((1, 1, tn), lambda l, n: (l, 0, n))],
        out_specs=pl.BlockSpec((1, 16, tn), lambda l, n: (l, 0, n)),
        out_shape=jax.ShapeDtypeStruct((DEPTH, 16, N_MOD * D), F32),
        compiler_params=_cparams(("arbitrary", "arbitrary")),
        name="mod_table",
    )(cvec, mod_w, mod_b.reshape(DEPTH, 1, N_MOD * D))


def _inproj_kernel(xp_ref, xs_ref, mod_ref, nw_ref, w_ref, wx_ref, of_ref, ob_ref, ox_ref, h_ref, *, tm, nf):
    i = pl.program_id(0)
    n = pl.program_id(1)

    @pl.when(n == 0)
    def _():
        x = _pick(i, tm, xp_ref, xs_ref)
        r = _mod_row(i, tm)
        h = _rms(x, nw_ref[...]) * (1.0 + _mod_vec(mod_ref, r, 1)) + _mod_vec(mod_ref, r, 0)
        hb = h.astype(BF16)
        h_ref[...] = hb
        ox_ref[...] = jnp.dot(hb, wx_ref[...], preferred_element_type=F32)

    @pl.when(n < nf)
    def _():
        of_ref[...] = jnp.dot(h_ref[...], w_ref[...], preferred_element_type=F32)

    @pl.when(n >= nf)
    def _():
        ob_ref[...] = jnp.dot(h_ref[...], w_ref[...], preferred_element_type=F32).astype(BF16)


def _inproj(xp, xs, mod_l, nw, w, wx, *, tm, tn):
    nx = wx.shape[1]
    nf, nb = PROJ_F // tn, PROJ_B // tn
    return pl.pallas_call(
        functools.partial(_inproj_kernel, tm=tm, nf=nf),
        grid=(T // tm, nf + nb),
        in_specs=_two_src_specs(tm) + [
            pl.BlockSpec((16, N_MOD * D), lambda i, n: (0, 0)),
            pl.BlockSpec((1, D), lambda i, n: (0, 0)),
            pl.BlockSpec((D, tn), lambda i, n: (0, n)),
            pl.BlockSpec((D, nx), lambda i, n: (0, 0))],
        out_specs=[pl.BlockSpec((tm, tn), lambda i, n: (i, jnp.minimum(n, nf - 1))),
                   pl.BlockSpec((tm, tn), lambda i, n: (i, jnp.maximum(n - nf, 0))),
                   pl.BlockSpec((tm, nx), lambda i, n: (i, 0))],
        out_shape=[jax.ShapeDtypeStruct((T, PROJ_F), F32), jax.ShapeDtypeStruct((T, PROJ_B), BF16),
                   jax.ShapeDtypeStruct((T, nx), F32)],
        scratch_shapes=[pltpu.VMEM((tm, D), BF16)],
        compiler_params=_cparams(("arbitrary", "arbitrary")),
        name="inproj",
    )(xp, xs, mod_l, nw, w, wx)


def _tri(n, upper):
    r = lax.broadcasted_iota(I32, (n, n), 0)
    c = lax.broadcasted_iota(I32, (n, n), 1)
    return (c >= r) if upper else (c <= r)


def _conv_silu(x, w, b, L):
    row = lax.broadcasted_iota(I32, (L, 1), 0)
    acc = x * w[2:3] + b
    for k in (0, 1, 3, 4):
        off = k - 2
        shifted = pltpu.roll(x, (-off) % L, axis=0)
        ok = (row + off >= 0) & (row + off < L)
        acc = acc + jnp.where(ok, shifted, 0.0) * w[k:k + 1]
    return _silu(acc)


def _split_hi_lo(x):
    hi = x.astype(BF16)
    return hi, (x - hi.astype(F32)).astype(BF16)


def _ssd_kernel(*refs, L, has_init, want_state):
    (x_ref, b_ref, c_ref, dt_ref, cwx_ref, cwb_ref, cwc_ref, cbx_ref, cbb_ref, cbc_ref,
     dtb_ref, alog_ref, dcol_ref, rsel_ref) = refs[:14]
    pos = 14
    s0_ref = None
    if has_init:
        s0_ref = refs[pos]
        pos += 1
    y_ref = refs[pos]
    pos += 1
    st_ref = None
    if want_state:
        st_ref = refs[pos]
        pos += 1
    xs_s, bs_s, cs_s, y_s = refs[pos:pos + 4]

    tc = SSD_TC[L]
    nc = L // tc
    xs_s[...] = _conv_silu(x_ref[...], cwx_ref[...], cbx_ref[...], L)
    bs_s[...] = _conv_silu(b_ref[...], cwb_ref[...], cbb_ref[...], L)
    cs_s[...] = _conv_silu(c_ref[...], cwc_ref[...], cbc_ref[...], L)
    y_s[...] = xs_s[...] * dcol_ref[...]

    z = dt_ref[...] + dtb_ref[...]
    dt_all = jnp.maximum(z, 0.0) + jnp.log(1.0 + jnp.exp(-jnp.abs(z)))
    a_row = -jnp.exp(alog_ref[...])

    dirs = (0, 1)
    tris = [_tri(tc, False), _tri(tc, True)]
    tri_b = [t.astype(BF16) for t in tris]
    sts = [s0_ref[d].reshape(GW, SSD_N).T if has_init else None for d in dirs]
    for ci in range(nc):
        cidx = (ci, nc - 1 - ci)
        sls = [slice(c * tc, (c + 1) * tc) for c in cidx]
        x = [xs_s[s] for s in sls]
        bm = [bs_s[s] for s in sls]
        cm = [cs_s[s] for s in sls]
        dt = [dt_all[s] for s in sls]
        parts = [_split_hi_lo(dt[d] * a_row) for d in dirs]
        cum = [jnp.dot(tri_b[d], parts[d][0], preferred_element_type=F32)
               + jnp.dot(tri_b[d], parts[d][1], preferred_element_type=F32) for d in dirs]
        cum_t = [cum[d].T for d in dirs]
        end = [cum[0][tc - 1:tc], cum[1][0:1]]
        ex = [_bdot(jnp.concatenate(
            [dt[d], jnp.exp(cum[d]), jnp.exp(end[d] - cum[d]),
             jnp.broadcast_to(jnp.exp(end[d]), (SUBLANES, LANES))], axis=0), rsel_ref[d])
            for d in dirs]
        xdt = [x[d] * ex[d][0:tc] for d in dirs]
        cb = [_bdot_nt(cm[0], bm[0])]
        cb.append(cb[0] if cidx[1] == cidx[0] else _bdot_nt(cm[1], bm[1]))
        lms = [[jnp.exp(jnp.where(tris[d], cum[d][:, d * HPG + h:d * HPG + h + 1]
                                  - cum_t[d][d * HPG + h:d * HPG + h + 1, :], -jnp.inf))
                for h in range(HPG)] for d in dirs]
        ys = [[_bdot(cb[d] * lms[d][h], xdt[d][:, h * SSD_P:(h + 1) * SSD_P]) for h in range(HPG)]
              for d in dirs]
        y = [jnp.concatenate(ys[d], axis=1) for d in dirs]
        for d in dirs:
            if sts[d] is not None:
                y[d] = y[d] + _bdot(cm[d], sts[d]) * ex[d][tc:2 * tc]
        if cidx[0] == cidx[1]:
            y_s[sls[0]] = y_s[sls[0]] + (y[0] + y[1])
        else:
            for d in dirs:
                y_s[sls[d]] = y_s[sls[d]] + y[d]
        sc = [_bdot(bm[d].T, xdt[d] * ex[d][2 * tc:3 * tc]) for d in dirs]
        sts = [sc[d] if sts[d] is None else sts[d] * ex[d][3 * tc:3 * tc + 1] + sc[d] for d in dirs]
    if want_state:
        for d in dirs:
            st_ref[d] = sts[d].T.reshape(HPG, SSD_P, SSD_N)

    y_ref[...] = y_s[...]


def _ssd_rsel():
    r = np.zeros((2, LANES, GW), np.float32)
    for d in range(2):
        for h in range(HPG):
            r[d, d * HPG + h, h * SSD_P:(h + 1) * SSD_P] = 1.0
    return jnp.asarray(r, BF16)


def _ssd_call(proj, dtp, prm, *, L, n_seq, row0, s0, want_state):
    rb = row0 // L
    has_init = s0 is not None
    xbc = COL_XBC
    in_specs = [
        pl.BlockSpec((L, GW), lambda b, g: (rb + b, xbc // GW + g)),
        pl.BlockSpec((L, SSD_N), lambda b, g: (rb + b, (xbc + SSD_INNER) // SSD_N + g)),
        pl.BlockSpec((L, SSD_N), lambda b, g: (rb + b, (xbc + SSD_INNER + SSD_G * SSD_N) // SSD_N + g)),
        pl.BlockSpec((L, LANES), lambda b, g: (rb + b, g)),
        pl.BlockSpec((SSD_CONV, GW), lambda b, g: (0, g)),
        pl.BlockSpec((SSD_CONV, SSD_N), lambda b, g: (0, g)),
        pl.BlockSpec((SSD_CONV, SSD_N), lambda b, g: (0, g)),
        pl.BlockSpec((1, GW), lambda b, g: (0, g)),
        pl.BlockSpec((1, SSD_N), lambda b, g: (0, g)),
        pl.BlockSpec((1, SSD_N), lambda b, g: (0, g)),
        pl.BlockSpec((1, LANES), lambda b, g: (0, g)),
        pl.BlockSpec((1, LANES), lambda b, g: (0, g)),
        pl.BlockSpec((1, GW), lambda b, g: (0, g)),
        pl.BlockSpec((2, LANES, GW), lambda b, g: (0, 0, 0)),
    ]
    args = [proj, proj, proj, dtp, prm["cwx"], prm["cwb"], prm["cwc"], prm["cbx"], prm["cbb"], prm["cbc"],
            prm["dtb"], prm["alog"], prm["dcol"], prm["rsel"]]
    if has_init:
        in_specs.append(pl.BlockSpec((None, 2, HPG, SSD_P, SSD_N), lambda b, g: (b, 0, g, 0, 0)))
        args.append(s0)
    out_specs = [pl.BlockSpec((L, GW), lambda b, g: (b, g))]
    out_shape = [jax.ShapeDtypeStruct((n_seq * L, SSD_INNER), F32)]
    if want_state:
        out_specs.append(pl.BlockSpec((None, 2, HPG, SSD_P, SSD_N), lambda b, g: (b, 0, g, 0, 0)))
        out_shape.append(jax.ShapeDtypeStruct((n_seq, 2, SSD_HEADS, SSD_P, SSD_N), F32))
    return pl.pallas_call(
        functools.partial(_ssd_kernel, L=L, has_init=has_init, want_state=want_state),
        grid=(n_seq, SSD_G),
        in_specs=in_specs, out_specs=out_specs, out_shape=out_shape,
        scratch_shapes=[pltpu.VMEM((L, GW), F32), pltpu.VMEM((L, SSD_N), F32),
                        pltpu.VMEM((L, SSD_N), F32), pltpu.VMEM((L, GW), F32)],
        compiler_params=_cparams(("arbitrary", "arbitrary")),
        name=f"ssd_L{L}",
    )(*args)


def _gla_kernel(*refs, L, layer, has_init, want_state, hpb):
    q_ref, ff_ref, fb_ref, v_ref, g_ref, lb_ref, nw_ref = refs[:7]
    pos = 7
    s0_ref = None
    if has_init:
        s0_ref = refs[pos]
        pos += 1
    o_ref = refs[pos]
    pos += 1
    st_ref = None
    if want_state:
        st_ref = refs[pos]
        pos += 1
    qs, ks, ls, os_ = refs[pos:pos + 4]

    tc, gr = GLA_TC, GLA_GROUP
    ng, cpg = L // gr, gr // tc
    lbr = lb_ref[...]
    e = jnp.exp(lbr - jnp.max(lbr, axis=0, keepdims=True))
    sm = e / jnp.sum(e, axis=0, keepdims=True)
    lb = sm[0]
    for j in range(1, layer + 1):
        lb = lb + sm[j]

    qs[...] = _silu(q_ref[...])
    for d, fr in ((0, ff_ref), (1, fb_ref)):
        logit = fr[...]
        lbd = lb[d:d + 1]
        f = lbd + (1.0 - lbd) * jax.nn.sigmoid(logit)
        ks[d] = (1.0 - lbd) * jax.nn.sigmoid(-logit)
        ls[d] = jnp.log(f)

    r_i = lax.broadcasted_iota(I32, (gr, gr), 0)
    c_i = lax.broadcasted_iota(I32, (gr, gr), 1)
    sh = tc.bit_length() - 1
    same = (r_i >> sh) == (c_i >> sh)
    causal = [same & (c_i <= r_i), same & (c_i >= r_i)]
    bd = [m.astype(BF16) for m in causal]
    row_chunk = lax.broadcasted_iota(I32, (gr, 1), 0) >> sh
    col_chunk = lax.broadcasted_iota(I32, (1, gr), 1) >> sh

    def rows_of(x, r):
        return jnp.concatenate(
            [jnp.broadcast_to(x[c * tc + r:c * tc + r + 1], (tc, x.shape[1])) for c in range(cpg)], axis=0)

    def groups(units):
        n = range(len(units))
        ds_ = [u[0] for u in units]
        sl = [(pl.ds(u[2], gr), slice(u[1] * HG_K, (u[1] + 1) * HG_K)) for u in units]
        sts = [u[3] for u in units]
        parts = [_split_hi_lo(ls[ds_[i], sl[i][0], sl[i][1]]) for i in n]
        cums = [jnp.dot(bd[ds_[i]], parts[i][0], preferred_element_type=F32)
                + jnp.dot(bd[ds_[i]], parts[i][1], preferred_element_type=F32) for i in n]
        q = [qs[sl[i][0], sl[i][1]] for i in n]
        k = [ks[ds_[i], sl[i][0], sl[i][1]] for i in n]
        v = [v_ref[sl[i][0], sl[i][1]] for i in n]
        ref_b = [rows_of(cums[i], tc // 2 if ds_[i] else tc // 2 - 1) for i in n]
        end_b = [rows_of(cums[i], 0 if ds_[i] else tc - 1) for i in n]
        att = [_bdot_nt(q[i] * jnp.exp(cums[i] - ref_b[i]), k[i] * jnp.exp(ref_b[i] - cums[i])) for i in n]
        o = [_bdot(jnp.where(causal[ds_[i]], att[i], 0.0), v[i]) for i in n]
        kd_t = [(k[i] * jnp.exp(end_b[i] - cums[i])).T for i in n]
        inc = [_bdot(jnp.concatenate([jnp.where(col_chunk == c, kd_t[i], 0.0) for c in range(cpg)], axis=0), v[i])
               for i in n]
        dec_t = [jnp.exp(end_b[i]).T for i in n]
        stacks = []
        for i in n:
            st, s_in = sts[i], [None] * cpg
            for c in (range(cpg - 1, -1, -1) if ds_[i] else range(cpg)):
                s_in[c] = st
                st = st * dec_t[i][:, c * tc:c * tc + 1] + inc[i][c * HG_K:(c + 1) * HG_K]
            sts[i] = st
            stacks.append(jnp.concatenate(s_in, axis=0))
        qe = [q[i] * jnp.exp(cums[i]) for i in n]
        o = [o[i] + _bdot(jnp.concatenate([jnp.where(row_chunk == c, qe[i], 0.0) for c in range(cpg)], axis=1),
                          stacks[i]) for i in n]
        return o, sts

    def init(d, j):
        return s0_ref[d, j] if has_init else jnp.zeros((HG_K, HG_V), F32)

    heads = range(hpb)
    if ng == 1:
        o, sts = groups([(d, j, 0, init(d, j)) for j in heads for d in (0, 1)])
        outs = [o[2 * j] + o[2 * j + 1] for j in heads]
        finals = [(sts[2 * j], sts[2 * j + 1]) for j in heads]
    else:
        os_[...] = jnp.zeros_like(os_)

        def body(i, carry):
            g0 = (pl.multiple_of(i * gr, gr), pl.multiple_of((ng - 1 - i) * gr, gr))
            o, sts = groups([(d, j, g0[d], carry[j][d]) for j in heads for d in (0, 1)])
            for j in heads:
                cs = slice(j * HG_V, (j + 1) * HG_V)
                for d in (0, 1):
                    os_[pl.ds(g0[d], gr), cs] = os_[pl.ds(g0[d], gr), cs] + o[2 * j + d]
            return tuple((sts[2 * j], sts[2 * j + 1]) for j in heads)

        finals = lax.fori_loop(0, ng, body, tuple((init(0, j), init(1, j)) for j in heads))
        outs = [os_[:, j * HG_V:(j + 1) * HG_V] for j in heads]
    for j in heads:
        cs = slice(j * HG_V, (j + 1) * HG_V)
        if want_state:
            st_ref[0, j] = finals[j][0]
            st_ref[1, j] = finals[j][1]
        o_ref[:, cs] = (_rms(outs[j], nw_ref[:, cs]) * _silu(g_ref[:, cs].astype(F32))).astype(BF16)


def _gla_call(proj, projb, hg_lb, hg_nw, *, layer, L, n_seq, row0, s0, want_state, hpb):
    rb = row0 // L
    has_init = s0 is not None
    w = hpb * HG_K

    def col(c0):
        return lambda b, h: (rb + b, c0 // w + h)

    st_spec = pl.BlockSpec((None, 2, hpb, HG_K, HG_V), lambda b, h: (b, 0, h, 0, 0))
    in_specs = [
        pl.BlockSpec((L, w), col(COL_HQ)),
        pl.BlockSpec((L, w), col(COL_HF)),
        pl.BlockSpec((L, w), col(COL_HF + HG_QK)),
        pl.BlockSpec((L, w), col(COL_HI)),
        pl.BlockSpec((L, w), col(COL_HG)),
        pl.BlockSpec((DEPTH + 1, 2, w), lambda b, h: (0, 0, h)),
        pl.BlockSpec((1, w), lambda b, h: (0, h)),
    ]
    args = [proj, proj, proj, projb, projb, hg_lb, hg_nw]
    if has_init:
        in_specs.append(st_spec)
        args.append(s0)
    out_specs = [pl.BlockSpec((L, w), lambda b, h: (b, h))]
    out_shape = [jax.ShapeDtypeStruct((n_seq * L, HG_IV), BF16)]
    if want_state:
        out_specs.append(st_spec)
        out_shape.append(jax.ShapeDtypeStruct((n_seq, 2, HG_HEADS, HG_K, HG_V), F32))
    return pl.pallas_call(
        functools.partial(_gla_kernel, L=L, layer=layer, has_init=has_init, want_state=want_state, hpb=hpb),
        grid=(n_seq, HG_HEADS // hpb),
        in_specs=in_specs, out_specs=out_specs, out_shape=out_shape,
        scratch_shapes=[pltpu.VMEM((L, w), F32), pltpu.VMEM((2, L, w), F32),
                        pltpu.VMEM((2, L, w), F32), pltpu.VMEM((L, w), F32)],
        compiler_params=_cparams(("arbitrary", "arbitrary")),
        name=f"gla_L{L}",
    )(*args)


def _ssm_out_kernel(xp_ref, xs_ref, yp_ref, ys_ref, op_ref, os_ref, z_ref, mod_ref, nw_ref, w_ref,
                    out_ref, *, tm):
    i = pl.program_id(0)
    g = _pick(i, tm, yp_ref, ys_ref) * _silu(z_ref[...].astype(F32))
    mix = (_bdot(_rms(g, nw_ref[...]), w_ref[0:SSD_INNER, :])
           + _bdot(_pick(i, tm, op_ref, os_ref), w_ref[SSD_INNER:, :]))
    out_ref[...] = _pick(i, tm, xp_ref, xs_ref) + _mod_vec(mod_ref, _mod_row(i, tm), 2) * mix


def _ssm_out(xp, xs, yp, ys, op, os_, proj, mod_l, nw, w, *, tm):
    return pl.pallas_call(
        functools.partial(_ssm_out_kernel, tm=tm),
        grid=(T // tm,),
        in_specs=_two_src_specs(tm) + _two_src_specs(tm, SSD_INNER) + _two_src_specs(tm, HG_IV) + [
            pl.BlockSpec((tm, SSD_INNER), lambda i: (i, COL_Z // SSD_INNER)),
            pl.BlockSpec((16, N_MOD * D), lambda i: (0, 0)),
            pl.BlockSpec((1, SSD_INNER), lambda i: (0, 0)),
            pl.BlockSpec((SSD_INNER + HG_IV, D), lambda i: (0, 0))],
        out_specs=pl.BlockSpec((tm, D), lambda i: (i, 0)),
        out_shape=jax.ShapeDtypeStruct((T, D), F32),
        compiler_params=_cparams(("arbitrary",)),
        name="ssm_out",
    )(xp, xs, yp, ys, op, os_, proj, mod_l, nw, w)


def _attn_out_kernel(x_ref, ap_ref, as_ref, mod_ref, w_ref, out_ref, *, tm):
    i = pl.program_id(0)
    mix = jnp.dot(_pick(i, tm, ap_ref, as_ref), w_ref[...], preferred_element_type=F32)
    out_ref[...] = x_ref[...] + _mod_vec(mod_ref, _mod_row(i, tm), 2) * mix


def _attn_out(x, ap, as_, mod_l, w, *, tm):
    return pl.pallas_call(
        functools.partial(_attn_out_kernel, tm=tm),
        grid=(T // tm,),
        in_specs=[pl.BlockSpec((tm, D), lambda i: (i, 0))] + _two_src_specs(tm) + [
            pl.BlockSpec((16, N_MOD * D), lambda i: (0, 0)),
            pl.BlockSpec((D, D), lambda i: (0, 0))],
        out_specs=pl.BlockSpec((tm, D), lambda i: (i, 0)),
        out_shape=jax.ShapeDtypeStruct((T, D), F32),
        compiler_params=_cparams(("arbitrary",)),
        name="attn_out",
    )(x, ap, as_, mod_l, w)


def _route_kernel(x_ref, mod_ref, nw_ref, rw_ref, rb_ref, h_ref, meta_ref, metat_ref, cnt_ref):
    i = pl.program_id(0)
    r = _mod_row(i, TB)
    h = _rms(x_ref[...], nw_ref[...]) * (1.0 + _mod_vec(mod_ref, r, 4)) + _mod_vec(mod_ref, r, 3)
    hb = h.astype(BF16)
    h_ref[...] = hb
    logits = jnp.dot(hb, rw_ref[...], preferred_element_type=F32) + rb_ref[...]
    lane = lax.broadcasted_iota(I32, (TB, LANES), 1)
    lane_f = lane.astype(F32)
    vals, hots, idxs = [], [], []
    for _ in range(TOP_K):
        m = jnp.max(logits, axis=-1, keepdims=True)
        idx = jnp.min(jnp.where(logits == m, lane_f, float(LANES)), axis=-1, keepdims=True)
        hot = lane_f == idx
        vals.append(m)
        idxs.append(idx)
        hots.append(hot)
        logits = jnp.where(hot, -jnp.inf, logits)
    es = [jnp.exp(v - vals[0]) for v in vals]
    den = es[0] + es[1] + es[2] + es[3]

    r_i = lax.broadcasted_iota(I32, (TB, TB), 0)
    c_i = lax.broadcasted_iota(I32, (TB, TB), 1)
    below = (c_i < r_i).astype(BF16)
    base = jnp.zeros((1, LANES), F32)
    ranks = []
    for k in range(TOP_K):
        hot_f = hots[k].astype(F32)
        before = jnp.dot(below, hots[k].astype(BF16), preferred_element_type=F32)
        ranks.append(base + before)
        base = base + jnp.sum(hot_f, axis=0, keepdims=True)
    units = jnp.floor((base + (SEG - 1)) * (1.0 / SEG))
    lr = lax.broadcasted_iota(I32, (LANES, LANES), 0)
    lc = lax.broadcasted_iota(I32, (LANES, LANES), 1)
    seg_start = SEG * jnp.dot(jnp.broadcast_to(units, (SUBLANES, LANES)).astype(BF16),
                              (lr < lc).astype(BF16), preferred_element_type=F32)[0:1]
    meta = jnp.zeros((TB, LANES), F32)
    for k in range(TOP_K):
        row = jnp.sum(hots[k].astype(F32) * (seg_start + ranks[k]), axis=-1, keepdims=True)
        meta = jnp.where(lane == k, idxs[k], meta)
        meta = jnp.where(lane == TOP_K + k, es[k] / den, meta)
        meta = jnp.where(lane == 2 * TOP_K + k, row, meta)
    meta_ref[...] = meta
    metat_ref[...] = meta.T
    cnt_ref[...] = base


def _route(x, mod_l, nw, rw, rb):
    return pl.pallas_call(
        _route_kernel,
        grid=(NB,),
        in_specs=[pl.BlockSpec((TB, D), lambda i: (i, 0)),
                  pl.BlockSpec((16, N_MOD * D), lambda i: (0, 0)),
                  pl.BlockSpec((1, D), lambda i: (0, 0)),
                  pl.BlockSpec((D, LANES), lambda i: (0, 0)),
                  pl.BlockSpec((1, LANES), lambda i: (0, 0))],
        out_specs=[pl.BlockSpec((TB, D), lambda i: (i, 0)),
                   pl.BlockSpec((TB, LANES), lambda i: (i, 0)),
                   pl.BlockSpec((LANES, TB), lambda i: (0, i)),
                   pl.BlockSpec((None, 1, LANES), lambda i: (i, 0, 0))],
        out_shape=[jax.ShapeDtypeStruct((T, D), BF16),
                   jax.ShapeDtypeStruct((T, LANES), F32),
                   jax.ShapeDtypeStruct((LANES, T), F32),
                   jax.ShapeDtypeStruct((NB, 1, LANES), F32)],
        compiler_params=_cparams(("arbitrary",)),
        name="moe_route",
    )(x, mod_l, nw, rw, rb)


def _copy(src, dst, sem):
    return pltpu.make_async_copy(src, dst, sem)


def _segment_copies(n, src, dst, sizes, make, wait):
    src = jnp.asarray(src, I32)
    dst = jnp.asarray(dst, I32)

    def pieces(src, dst, group):
        for size in group:
            take = n & size

            @pl.when(take != 0)
            def _(src=src, dst=dst, size=size):
                cp = make(pl.multiple_of(src, SEG), pl.multiple_of(dst, SEG), size)
                if wait:
                    cp.wait()
                else:
                    cp.start()

            src = src + take
            dst = dst + take

    large = tuple(s for s in sizes if s >= SEG_LARGE)
    if large:
        @pl.when(n >= SEG_LARGE)
        def _():
            pieces(src, dst, large)

        skip = n & ~(SEG_LARGE - 1)
        src = src + skip
        dst = dst + skip
    pieces(src, dst, tuple(s for s in sizes if s < SEG_LARGE))


def _sort_push_kernel(tab_ref, tabp_ref, ztab_ref, nv_ref, metat_ref, h_ref, xs_hbm, buf, zb, sem, zsem):
    b = pl.program_id(0)
    slot = b & 1

    def zero_copies(wait):
        def per_expert(e, carry):
            _segment_copies(ztab_ref[1, e], 0, ztab_ref[0, e], PAD_SIZES,
                            lambda s, d, size: _copy(zb.at[pl.ds(0, size)], xs_hbm.at[pl.ds(d, size)], zsem), wait)
            return carry

        lax.fori_loop(0, N_EXP, per_expert, 0)

        def per_tile(j, carry):
            cp = _copy(zb, xs_hbm.at[pl.ds(pl.multiple_of(j * TM_E, TM_E), TM_E)], zsem)
            if wait:
                cp.wait()
            else:
                cp.start()
            return carry

        lax.fori_loop(nv_ref[0], N_ETILES, per_tile, 0)

    @pl.when(b == 0)
    def _():
        zb[...] = jnp.zeros_like(zb)
        zero_copies(False)

    ch = 256
    for c in range(RB // ch):
        rows = (lax.broadcasted_iota(I32, (ch, TB), 0) + c * ch).astype(F32)
        hit = rows == metat_ref[2 * TOP_K:2 * TOP_K + 1, :]
        for k in range(1, TOP_K):
            hit = hit | (rows == metat_ref[2 * TOP_K + k:2 * TOP_K + k + 1, :])
        buf[slot, c * ch:(c + 1) * ch, :] = jnp.dot(hit.astype(BF16), h_ref[...], preferred_element_type=F32)

    def push(t_ref, s, wait):
        def per_expert(e, carry):
            _segment_copies(
                t_ref[0, e], t_ref[1, e], t_ref[2, e], SEG_SIZES,
                lambda so, do, size: _copy(buf.at[s, pl.ds(so, size)], xs_hbm.at[pl.ds(do, size)], sem.at[s]), wait)
            return carry

        lax.fori_loop(0, N_EXP, per_expert, 0)

    def drain(t_ref, s):
        _segment_copies(t_ref[1, N_EXP - 1] + t_ref[0, N_EXP - 1], 0, 0, BUF_SIZES,
                        lambda so, do, size: _copy(buf.at[s, pl.ds(0, size)], xs_hbm.at[pl.ds(0, size)], sem.at[s]),
                        True)

    @pl.when(b > 0)
    def _():
        drain(tabp_ref, 1 - slot)

    push(tab_ref, slot, False)

    @pl.when(b == NB - 1)
    def _():
        drain(tab_ref, slot)

    @pl.when(b == 0)
    def _():
        zero_copies(True)


def _sort_push(tab, ztab, n_valid, metat, h):
    return pl.pallas_call(
        _sort_push_kernel,
        grid=(NB,),
        in_specs=[pl.BlockSpec((None, 3, N_EXP), lambda b: (b, 0, 0), memory_space=pltpu.SMEM),
                  pl.BlockSpec((None, 3, N_EXP), lambda b: (jnp.maximum(b - 1, 0), 0, 0),
                               memory_space=pltpu.SMEM),
                  pl.BlockSpec(memory_space=pltpu.SMEM),
                  pl.BlockSpec(memory_space=pltpu.SMEM),
                  pl.BlockSpec((LANES, TB), lambda b: (0, b)),
                  pl.BlockSpec((TB, D), lambda b: (b, 0))],
        out_specs=pl.BlockSpec(memory_space=pl.ANY),
        out_shape=jax.ShapeDtypeStruct((P_ROWS, D), F32),
        scratch_shapes=[pltpu.VMEM((2, RB, D), F32), pltpu.VMEM((TM_E, D), F32),
                        pltpu.SemaphoreType.DMA((2,)), pltpu.SemaphoreType.DMA(())],
        compiler_params=_cparams(("arbitrary",)),
        name="moe_sort_push",
    )(tab, tab, ztab, n_valid, metat, h)


def _expert_kernel(te_ref, tr_ref, nv_ref, par_ref, nxt_ref, x_ref, w1_hbm, b1_ref, w2_hbm, b2_ref, y_ref,
                   wf1, wf2, w1_s, w2_s, sem, *, layer):
    j = pl.program_id(0)
    valid = j < nv_ref[0]
    fresh = jnp.logical_or(j == 0, te_ref[j] != te_ref[jnp.maximum(j - 1, 0)])

    def fetch(e, s):
        return (_copy(w1_hbm.at[layer, e], wf1.at[s], sem.at[0, s]),
                _copy(w2_hbm.at[layer, e], wf2.at[s], sem.at[1, s]))

    @pl.when(jnp.logical_and(valid, fresh))
    def _():
        s = par_ref[j]

        @pl.when(j == 0)
        def _():
            for cp in fetch(te_ref[0], 0):
                cp.start()

        for cp in fetch(te_ref[j], s):
            cp.wait()

        @pl.when(nxt_ref[j] >= 0)
        def _():
            for cp in fetch(nxt_ref[j], 1 - s):
                cp.start()

        w1_s[...] = wf1[s].astype(BF16)
        w2_s[...] = wf2[s].astype(BF16)

    @pl.when(valid)
    def _():
        rows = lax.broadcasted_iota(I32, (TM_E, 1), 0)
        x = jnp.where(rows < tr_ref[j], x_ref[...], 0.0).astype(BF16)
        gu = jnp.dot(x, w1_s[...], preferred_element_type=F32) + b1_ref[...]
        gate = jnp.minimum(gu[:, :FF], SWIGLU_LIMIT)
        up = jnp.clip(gu[:, FF:], -SWIGLU_LIMIT, SWIGLU_LIMIT)
        act = (up + 1.0) * gate * jax.nn.sigmoid(SWIGLU_ALPHA * gate)
        y_ref[...] = jnp.dot(act.astype(BF16), w2_s[...], preferred_element_type=F32) + b2_ref[...]

    @pl.when(jnp.logical_not(valid))
    def _():
        y_ref[...] = jnp.zeros_like(y_ref)


def _experts(layer, tile_exp, tile_rows, n_valid, run_par, run_next, xs, w1, b1, w2, b2):
    def wmap(j, te, *_):
        return (layer, te[j], 0, 0)

    return pl.pallas_call(
        functools.partial(_expert_kernel, layer=layer),
        grid_spec=pltpu.PrefetchScalarGridSpec(
            num_scalar_prefetch=5,
            grid=(N_ETILES,),
            in_specs=[pl.BlockSpec((TM_E, D), lambda j, *_: (j, 0)),
                      pl.BlockSpec(memory_space=pl.ANY),
                      pl.BlockSpec((None, None, 1, 2 * FF), wmap),
                      pl.BlockSpec(memory_space=pl.ANY),
                      pl.BlockSpec((None, None, 1, D), wmap)],
            out_specs=pl.BlockSpec((TM_E, D), lambda j, *_: (j, 0)),
            scratch_shapes=[pltpu.VMEM((2, D, 2 * FF), F32), pltpu.VMEM((2, FF, D), F32),
                            pltpu.VMEM((D, 2 * FF), BF16), pltpu.VMEM((FF, D), BF16),
                            pltpu.SemaphoreType.DMA((2, 2))]),
        out_shape=jax.ShapeDtypeStruct((P_ROWS, D), F32),
        compiler_params=_cparams(("arbitrary",)),
        name="moe_experts",
    )(tile_exp, tile_rows, n_valid, run_par, run_next, xs, w1, b1.reshape(DEPTH, N_EXP, 1, 2 * FF), w2,
      b2.reshape(DEPTH, N_EXP, 1, D))


def _combine_kernel(*refs, final):
    if final:
        tab_ref, tabn_ref, y_hbm, x_ref, meta_ref, mod_ref, fw_ref, outp_ref, outs_ref, buf, sem = refs
    else:
        tab_ref, tabn_ref, y_hbm, x_ref, meta_ref, mod_ref, out_ref, buf, sem = refs
    b = pl.program_id(0)
    slot = b & 1

    def pull(t_ref, s, wait):
        def per_expert(e, carry):
            _segment_copies(
                t_ref[0, e], t_ref[2, e], t_ref[1, e], SEG_SIZES,
                lambda so, do, size: _copy(y_hbm.at[pl.ds(so, size)], buf.at[s, pl.ds(do, size)], sem.at[s]), wait)
            return carry

        lax.fori_loop(0, N_EXP, per_expert, 0)

    @pl.when(b == 0)
    def _():
        pull(tab_ref, slot, False)

    @pl.when(b + 1 < NB)
    def _():
        pull(tabn_ref, 1 - slot, False)

    used = tab_ref[1, N_EXP - 1] + tab_ref[0, N_EXP - 1]
    _segment_copies(used, 0, 0, BUF_SIZES,
                    lambda so, do, size: _copy(y_hbm.at[pl.ds(0, size)], buf.at[slot, pl.ds(0, size)], sem.at[slot]),
                    True)

    meta = meta_ref[...]
    ch = 256
    acc = jnp.zeros((TB, D), F32)
    for c in range(RB // ch):
        cols = (lax.broadcasted_iota(I32, (TB, ch), 1) + c * ch).astype(F32)
        g = jnp.zeros((TB, ch), F32)
        for k in range(TOP_K):
            g = g + jnp.where(cols == meta[:, 2 * TOP_K + k:2 * TOP_K + k + 1],
                              meta[:, TOP_K + k:TOP_K + k + 1], 0.0)
        rows = lax.broadcasted_iota(I32, (ch, 1), 0) + c * ch
        y = jnp.where(rows < used, buf[slot, c * ch:(c + 1) * ch, :], 0.0)
        acc = acc + _bdot(g, y)
    out = x_ref[...] + _mod_vec(mod_ref, _mod_row(b, TB), 5) * acc
    if final:
        y_out = _rms(out, fw_ref[...])

        @pl.when(b < NP_TOK // TB)
        def _():
            outp_ref[...] = y_out

        @pl.when(b >= NP_TOK // TB)
        def _():
            outs_ref[...] = y_out
    else:
        out_ref[...] = out


def _combine(tab, y, x, meta, mod_l, final_w=None):
    final = final_w is not None
    nbp = NP_TOK // TB
    in_specs = [pl.BlockSpec((None, 3, N_EXP), lambda b: (b, 0, 0), memory_space=pltpu.SMEM),
                pl.BlockSpec((None, 3, N_EXP), lambda b: (jnp.minimum(b + 1, NB - 1), 0, 0),
                             memory_space=pltpu.SMEM),
                pl.BlockSpec(memory_space=pl.ANY),
                pl.BlockSpec((TB, D), lambda b: (b, 0)),
                pl.BlockSpec((TB, LANES), lambda b: (b, 0)),
                pl.BlockSpec((16, N_MOD * D), lambda b: (0, 0))]
    args = [tab, tab, y, x, meta, mod_l]
    if final:
        in_specs.append(pl.BlockSpec((1, D), lambda b: (0, 0)))
        args.append(final_w)
        out_specs = [pl.BlockSpec((TB, D), lambda b: (jnp.minimum(b, nbp - 1), 0)),
                     pl.BlockSpec((TB, D), lambda b: (jnp.maximum(b - nbp, 0), 0))]
        out_shape = [jax.ShapeDtypeStruct((NP_TOK, D), F32), jax.ShapeDtypeStruct((NS_TOK, D), F32)]
    else:
        out_specs = pl.BlockSpec((TB, D), lambda b: (b, 0))
        out_shape = jax.ShapeDtypeStruct((T, D), F32)
    return pl.pallas_call(
        functools.partial(_combine_kernel, final=final),
        grid=(NB,),
        in_specs=in_specs, out_specs=out_specs, out_shape=out_shape,
        scratch_shapes=[pltpu.VMEM((2, RB, D), F32), pltpu.SemaphoreType.DMA((2,))],
        compiler_params=_cparams(("arbitrary",)),
        name="moe_combine_final" if final else "moe_combine",
    )(*args)


def _moe_layer(layer, x, mod_l, nw, rw, rb, w1, b1, w2, b2, final_w=None):
    rw_p = jnp.zeros((D, LANES), BF16).at[:, :N_EXP].set(rw.astype(BF16))
    rb_p = jnp.full((1, LANES), -jnp.inf, F32).at[0, :N_EXP].set(rb)
    h, meta, metat, cnt = _route(x, mod_l, nw, rw_p, rb_p)

    counts = cnt[:, 0, :N_EXP].astype(I32)
    n_seg = (counts + SEG - 1) // SEG * SEG
    seg_start = jnp.cumsum(n_seg, axis=1) - n_seg
    tot = jnp.sum(n_seg, axis=0)
    padded = (tot + TM_E - 1) // TM_E * TM_E
    ends = jnp.cumsum(padded)
    offs = ends - padded
    glob = offs[None, :] + jnp.cumsum(n_seg, axis=0) - n_seg
    tab = jnp.stack([n_seg, seg_start, glob], axis=1)
    ztab = jnp.stack([offs + tot, padded - tot], axis=0)
    n_valid = ends[-1] // TM_E
    starts = jnp.arange(N_ETILES, dtype=I32) * TM_E
    te = jnp.minimum(jnp.sum((ends[None, :] <= starts[:, None]).astype(I32), axis=1), N_EXP - 1)
    last = jnp.sum((ends[:-1] < ends[-1]).astype(I32))
    te = jnp.where(starts < ends[-1], te, last)
    tr = jnp.clip(offs[te] + tot[te] - starts, 0, TM_E)
    nv = n_valid.reshape(1)
    run_start = jnp.concatenate([jnp.ones((1,), I32), (te[1:] != te[:-1]).astype(I32)])
    run_par = (jnp.cumsum(run_start) - 1) & 1
    eid = jnp.arange(N_EXP, dtype=I32)
    later = (eid[None, :] > eid[:, None]) & (padded[None, :] > 0)
    nxt_e = jnp.min(jnp.where(later, eid[None, :], N_EXP), axis=1)
    run_next = jnp.where(nxt_e < N_EXP, nxt_e, -1)[te]

    xs = _sort_push(tab, ztab, nv, metat, h)
    y = _experts(layer, te, tr, nv, run_par, run_next, xs, w1, b1, w2, b2)
    return _combine(tab, y, x, meta, mod_l, final_w)


def _head_sel():
    g = np.zeros((D, LANES), np.float32)
    for c in range(D):
        g[c, c // HEAD_DIM] = 1.0
    return jnp.asarray(g, BF16)


def _head_rms(x, sel, sel_t, w):
    n = x.shape[1]
    hi, lo = _split_hi_lo(x * x)
    ss = (jnp.dot(hi, sel[:n], preferred_element_type=F32)
          + jnp.dot(lo, sel[:n], preferred_element_type=F32))
    inv = lax.rsqrt(ss * (1.0 / HEAD_DIM) + EPS)
    return x * _bdot(inv, sel_t[:, :n]) * w


def _rope(x, cos, sin):
    n = x.shape[1]
    lane = lax.broadcasted_iota(I32, x.shape, 1)
    first = (lane % (HEAD_DIM // 2)) < (HEAD_DIM // 4)
    rot = jnp.where(first, -pltpu.roll(x, n - HEAD_DIM // 4, axis=1), pltpu.roll(x, HEAD_DIM // 4, axis=1))
    return x * cos + rot * sin


def _qkv_kernel(x_ref, mod_ref, nw_ref, w_ref, sel_ref, selt_ref, qw_ref, kw_ref, cos_ref, sin_ref,
                q_ref, k_ref, v_ref, kf_ref, vf_ref, *, tm):
    i = pl.program_id(0)
    r = _mod_row(i, tm)
    h = _rms(x_ref[...], nw_ref[...]) * (1.0 + _mod_vec(mod_ref, r, 1)) + _mod_vec(mod_ref, r, 0)
    qkv = _bdot(h, w_ref[...])
    nq, nk = ATT_HEADS * HEAD_DIM, KV_HEADS * HEAD_DIM
    q = _head_rms(qkv[:, :nq], sel_ref[...], selt_ref[...], qw_ref[...])
    k = _head_rms(qkv[:, nq:nq + nk], sel_ref[...], selt_ref[...], kw_ref[...])
    v = qkv[:, nq + nk:]
    kf_ref[...] = k
    vf_ref[...] = v
    latent = i * tm >= NP_TOK
    cos = jnp.where(latent, cos_ref[...], 1.0)
    sin = jnp.where(latent, sin_ref[...], 0.0)
    q = _rope(q, cos, sin) * (HEAD_DIM ** -0.5)
    k = _rope(k, cos[:, :nk], sin[:, :nk])
    q_ref[...] = q.T.astype(BF16)
    v_ref[...] = v.T.astype(BF16)
    for hd in range(KV_HEADS):
        k_ref[hd] = k[:, hd * HEAD_DIM:(hd + 1) * HEAD_DIM].astype(BF16)


def _qkv(x, mod_l, nw, w, qw, kw, cos, sin, *, tm):
    nk = KV_HEADS * HEAD_DIM
    sel = _head_sel()
    nblk = DEC_SEQ // tm

    def cmap(i):
        return (jnp.maximum(i - NP_TOK // tm, 0) % nblk, 0)

    return pl.pallas_call(
        functools.partial(_qkv_kernel, tm=tm),
        grid=(T // tm,),
        in_specs=[pl.BlockSpec((tm, D), lambda i: (i, 0)),
                  pl.BlockSpec((16, N_MOD * D), lambda i: (0, 0)),
                  pl.BlockSpec((1, D), lambda i: (0, 0)),
                  pl.BlockSpec((D, D + 2 * nk), lambda i: (0, 0)),
                  pl.BlockSpec((D, LANES), lambda i: (0, 0)),
                  pl.BlockSpec((LANES, D), lambda i: (0, 0)),
                  pl.BlockSpec((1, D), lambda i: (0, 0)),
                  pl.BlockSpec((1, nk), lambda i: (0, 0)),
                  pl.BlockSpec((tm, D), cmap),
                  pl.BlockSpec((tm, D), cmap)],
        out_specs=[pl.BlockSpec((D, tm), lambda i: (0, i)),
                   pl.BlockSpec((KV_HEADS, tm, HEAD_DIM), lambda i: (0, i, 0)),
                   pl.BlockSpec((nk, tm), lambda i: (0, i)),
                   pl.BlockSpec((tm, nk), lambda i: (i, 0)),
                   pl.BlockSpec((tm, nk), lambda i: (i, 0))],
        out_shape=[jax.ShapeDtypeStruct((D, T), BF16),
                   jax.ShapeDtypeStruct((KV_HEADS, T, HEAD_DIM), BF16),
                   jax.ShapeDtypeStruct((nk, T), BF16),
                   jax.ShapeDtypeStruct((T, nk), F32),
                   jax.ShapeDtypeStruct((T, nk), F32)],
        compiler_params=_cparams(("arbitrary",)),
        name="attn_qkv",
    )(x, mod_l, nw, w, sel, sel.T, qw, kw, cos, sin)


def _attn_kernel(*refs, has_cache):
    if has_cache:
        q_ref, k_ref, v_ref, kc_ref, vc_ref, o_ref = refs
    else:
        q_ref, k_ref, v_ref, o_ref = refs
    k = k_ref[0]
    vt = v_ref[...]
    heads = range(Q_PER_KV)
    qts = [q_ref[j * HEAD_DIM:(j + 1) * HEAD_DIM, :] for j in heads]
    ss = [jnp.dot(k, qt, preferred_element_type=F32) for qt in qts]
    ms = [jnp.max(s, axis=0, keepdims=True) for s in ss]
    if has_cache:
        scs = [jnp.dot(kc_ref[0, 0], qt, preferred_element_type=F32) for qt in qts]
        ms = [jnp.maximum(m, jnp.max(sc, axis=0, keepdims=True)) for m, sc in zip(ms, scs)]
    ps = [jnp.exp(s - m) for s, m in zip(ss, ms)]
    dens = [jnp.sum(p, axis=0, keepdims=True) for p in ps]
    os_ = [jnp.dot(vt, p.astype(BF16), preferred_element_type=F32) for p in ps]
    if has_cache:
        pcs = [jnp.exp(sc - m) for sc, m in zip(scs, ms)]
        dens = [d + jnp.sum(pc, axis=0, keepdims=True) for d, pc in zip(dens, pcs)]
        os_ = [o + jnp.dot(vc_ref[0, 0], pc.astype(BF16), preferred_element_type=F32) for o, pc in zip(os_, pcs)]
    o_ref[...] = jnp.concatenate([o / d for o, d in zip(os_, dens)], axis=0).T.astype(BF16)


def _attn_call(q, k, v, *, L, n_seq, row0, tq, cache):
    rb_q = row0 // tq
    rb_k = row0 // L
    nq = L // tq
    has_cache = cache is not None
    in_specs = [pl.BlockSpec((Q_PER_KV * HEAD_DIM, tq), lambda b, g, t: (g, rb_q + b * nq + t)),
                pl.BlockSpec((1, L, HEAD_DIM), lambda b, g, t: (g, rb_k + b, 0)),
                pl.BlockSpec((HEAD_DIM, L), lambda b, g, t: (g, rb_k + b))]
    args = [q, k, v]
    if has_cache:
        in_specs += [pl.BlockSpec((1, 1, PAST, HEAD_DIM), lambda b, g, t: (b, g, 0, 0)),
                     pl.BlockSpec((1, 1, HEAD_DIM, PAST), lambda b, g, t: (b, g, 0, 0))]
        args += list(cache)
    gw = Q_PER_KV * HEAD_DIM
    return pl.pallas_call(
        functools.partial(_attn_kernel, has_cache=has_cache),
        grid=(n_seq, KV_HEADS, nq),
        in_specs=in_specs,
        out_specs=pl.BlockSpec((tq, gw), lambda b, g, t: (b * nq + t, g)),
        out_shape=jax.ShapeDtypeStruct((n_seq * L, D), BF16),
        compiler_params=_cparams(("arbitrary", "arbitrary", "arbitrary")),
        name=f"attn_L{L}",
    )(*args)


def _rope_tables():
    rows = DEC_SEQ // GRID_W
    row = jnp.repeat(jnp.arange(rows), GRID_W)
    colp = jnp.tile(jnp.arange(GRID_W), rows)
    pos = jnp.stack([row, colp], axis=-1).astype(F32)
    half = HEAD_DIM // 2
    inv_freq = ROPE_THETA ** (-jnp.arange(0, half, 2, dtype=F32) / half)
    ang = pos[:, :, None] * inv_freq
    ang = jnp.concatenate([ang, ang], axis=-1).reshape(DEC_SEQ, HEAD_DIM)
    ang = jnp.tile(ang, (1, ATT_HEADS))
    return jnp.cos(ang), jnp.sin(ang)


def _ssd_params(conv_w, conv_b, dt_bias, a_log, d_skip):
    def dt_layout(p):
        g = p.reshape(2, SSD_G, HPG).transpose(1, 0, 2).reshape(SSD_G, 2 * HPG)
        return jnp.zeros((SSD_G, LANES), F32).at[:, :2 * HPG].set(g).reshape(1, SSD_G * LANES)

    ni = SSD_INNER
    nb = SSD_G * SSD_N
    return dict(
        cwx=conv_w[:, :ni], cwb=conv_w[:, ni:ni + nb], cwc=conv_w[:, ni + nb:],
        cbx=conv_b[None, :ni], cbb=conv_b[None, ni:ni + nb], cbc=conv_b[None, ni + nb:],
        dtb=dt_layout(dt_bias), alog=dt_layout(a_log),
        dcol=jnp.repeat(d_skip, SSD_P)[None, :], rsel=_ssd_rsel())


def _ssm_in_weights(w):
    ni, cd = SSD_INNER, SSD_INNER + 2 * SSD_G * SSD_N
    o_xbc, o_dt = ni, ni + cd
    o_hq = o_dt + 2 * SSD_HEADS
    o_hf, o_hi = o_hq + HG_QK, o_hq + 3 * HG_QK
    o_hg = o_hi + HG_IV
    main = jnp.concatenate([w[:, o_hq:o_hf], w[:, o_hf:o_hi], w[:, o_xbc:o_dt],
                            w[:, :ni], w[:, o_hi:o_hg], w[:, o_hg:o_hg + HG_IV]], axis=1).astype(BF16)
    wdt = w[:, o_dt:o_hq].reshape(D, 2, SSD_G, HPG).transpose(0, 2, 1, 3).reshape(D, SSD_G, 2 * HPG)
    wdt = jnp.zeros((D, SSD_G, LANES), F32).at[:, :, :2 * HPG].set(wdt).reshape(D, SSD_G * LANES)
    return main, wdt.astype(BF16)


def kernel(x_prompt, x_sample, c, c_ctx, state_ssd, state_hgrn, cache_k, cache_v, mod_w, mod_b, norm1_w,
           norm2_w, ssm_in_w, ssd_conv_w, ssd_conv_b, ssd_dt_bias, ssd_a_log, ssd_d, ssd_norm_w, hg_lb,
           hg_norm_w, ssm_out_w, attn_in_w, q_norm_w, k_norm_w, attn_out_w, router_w, router_b, exp_w1,
           exp_b1, exp_w2, exp_b2, final_norm_w):
    xp = x_prompt.reshape(NP_TOK, D)
    xs = x_sample.reshape(NS_TOK, D)
    cvec = jnp.zeros((16, D), F32).at[0].set(c_ctx).at[1:1 + DEC_BATCH].set(c)
    mod = _mod_table(cvec, mod_w, mod_b)

    w_main, w_dt = _ssm_in_weights(ssm_in_w[0])
    proj, projb, dtp = _inproj(xp, xs, mod[0], norm1_w[0][None], w_main, w_dt, tm=1024, tn=1536)
    prm = _ssd_params(ssd_conv_w[0], ssd_conv_b[0], ssd_dt_bias[0], ssd_a_log[0], ssd_d[0])
    yp, st_ssd = _ssd_call(proj, dtp, prm, L=SEQ, n_seq=BATCH, row0=0, s0=None, want_state=True)
    (ys,) = _ssd_call(proj, dtp, prm, L=DEC_SEQ, n_seq=DEC_BATCH, row0=NP_TOK, s0=state_ssd[:, 0],
                      want_state=False)
    hg_nw = hg_norm_w[0][None]
    op, st_hg = _gla_call(proj, projb, hg_lb, hg_nw, layer=0, L=SEQ, n_seq=BATCH, row0=0, s0=None,
                          want_state=True, hpb=8)
    (os_,) = _gla_call(proj, projb, hg_lb, hg_nw, layer=0, L=DEC_SEQ, n_seq=DEC_BATCH, row0=NP_TOK,
                       s0=state_hgrn[:, 0], want_state=False, hpb=4)
    x = _ssm_out(xp, xs, yp, ys, op, os_, projb, mod[0], ssd_norm_w[0][None], ssm_out_w[0].astype(BF16), tm=512)
    x = _moe_layer(0, x, mod[0], norm2_w[0][None], router_w[0], router_b[0], exp_w1, exp_b1, exp_w2, exp_b2)

    cos, sin = _rope_tables()
    q, k, v, kf, vf = _qkv(x, mod[1], norm1_w[1][None], attn_in_w[0].astype(BF16),
                           jnp.tile(q_norm_w[0], ATT_HEADS)[None], jnp.tile(k_norm_w[0], KV_HEADS)[None],
                           cos, sin, tm=512)
    ap = _attn_call(q, k, v, L=SEQ, n_seq=BATCH, row0=0, tq=SEQ, cache=None)
    ck = cache_k[:, 0].transpose(0, 2, 1, 3).astype(BF16)
    cv = cache_v[:, 0].transpose(0, 2, 3, 1).astype(BF16)
    as_ = _attn_call(q, k, v, L=DEC_SEQ, n_seq=DEC_BATCH, row0=NP_TOK, tq=512, cache=(ck, cv))
    x = _attn_out(x, ap, as_, mod[1], attn_out_w[0].astype(BF16), tm=1024)
    y_prompt, y_sample = _moe_layer(1, x, mod[1], norm2_w[1][None], router_w[1], router_b[1], exp_w1, exp_b1,
                                    exp_w2, exp_b2, final_w=final_norm_w[None])
    y_prompt = y_prompt.reshape(BATCH, SEQ, D)
    y_sample = y_sample.reshape(DEC_BATCH, DEC_SEQ, D)
    new_ssd = st_ssd.reshape(BATCH, 1, 2, SSD_HEADS, SSD_P, SSD_N)
    new_hg = st_hg.reshape(BATCH, 1, 2, HG_HEADS, HG_K, HG_V)
    new_k = kf[:NP_TOK].reshape(BATCH, 1, SEQ, KV_HEADS, HEAD_DIM)
    new_v = vf[:NP_TOK].reshape(BATCH, 1, SEQ, KV_HEADS, HEAD_DIM)
    return (y_prompt, y_sample, new_ssd, new_hg, new_k, new_v)
```
